```python
import jax, jax.numpy as jnp
from jax import lax
import numpy as np

D_MODEL = 1024
BATCH = 8
SEQ = 4096
DEPTH = 4

CHUNK = 64
Q_BLOCK = 128
HEAD_DIM = 64
A_HEADS = D_MODEL // HEAD_DIM
A_KV_HEADS = 4
IDX_HEADS = 8
IDX_DIM = 64
TOPK_MAX = 256
B_HEADS = D_MODEL // HEAD_DIM
D_FF = 4 * D_MODEL
ROPE_THETA = 10000.0
EPS = 1e-6
N_A = DEPTH // 2
N_B = DEPTH - N_A

A_Q = A_HEADS * HEAD_DIM
A_KV = A_KV_HEADS * HEAD_DIM
A_QI = IDX_HEADS * IDX_DIM
A_IN = A_Q + 2 * A_KV + A_QI + IDX_DIM + IDX_HEADS
A_SPLITS = [A_Q, A_Q + A_KV, A_Q + 2 * A_KV, A_Q + 2 * A_KV + A_QI, A_Q + 2 * A_KV + A_QI + IDX_DIM]
B_QK = B_HEADS * HEAD_DIM
B_KV_IN = 2 * B_QK + B_HEADS

kernel_name = "yoco_dsa_fox_hybrid"


def rms_norm(x, g):
    x32 = x.astype(jnp.float32)
    y = x32 * lax.rsqrt(jnp.mean(x32 * x32, axis=-1, keepdims=True) + EPS)
    return (y * g.astype(jnp.float32)).astype(x.dtype)


def apply_rope(x):
    s, d = x.shape[1], x.shape[-1]
    half = d // 2
    inv = 1.0 / (ROPE_THETA ** (jnp.arange(0, d, 2, dtype=jnp.float32) / d))
    ang = jnp.arange(s, dtype=jnp.float32)[:, None] * inv[None, :]
    c = jnp.cos(ang)[None, :, None, :]
    sn = jnp.sin(ang)[None, :, None, :]
    x1 = x[..., :half].astype(jnp.float32)
    x2 = x[..., half:].astype(jnp.float32)
    return jnp.concatenate([x1 * c - x2 * sn, x1 * sn + x2 * c], axis=-1).astype(x.dtype)


def to_blocks(a):
    b, s = a.shape[:2]
    return jnp.moveaxis(a.reshape(b, s // Q_BLOCK, Q_BLOCK, *a.shape[2:]), 1, 0)


def from_blocks(a):
    a = jnp.moveaxis(a, 0, 1)
    return a.reshape(a.shape[0], a.shape[1] * a.shape[2], *a.shape[3:])


def dsa_attention(xn, w_in, w_o):
    b, s, _ = xn.shape
    q, k, v, qi, ki, wi = jnp.split(xn @ w_in, A_SPLITS, axis=-1)
    q = apply_rope(q.reshape(b, s, A_HEADS, HEAD_DIM))
    k = apply_rope(k.reshape(b, s, A_KV_HEADS, HEAD_DIM))
    v = v.reshape(b, s, A_KV_HEADS, HEAD_DIM)
    qi = apply_rope(qi.reshape(b, s, IDX_HEADS, IDX_DIM))
    ki = apply_rope(ki[:, :, None, :])[:, :, 0]
    wi = wi * (IDX_HEADS ** -0.5 * IDX_DIM ** -0.5)
    n_sel = min(TOPK_MAX, s // 4)
    key_pos = jnp.arange(s)
    t_pos = key_pos.reshape(s // Q_BLOCK, Q_BLOCK)
    group = A_HEADS // A_KV_HEADS
    scale = HEAD_DIM ** -0.5

    def block(args):
        qb, qib, wib, tb = args
        limit = (tb // CHUNK + 1) * CHUNK
        admissible = key_pos[None, :] < limit[:, None]
        idx_logits = jnp.einsum('bqhd,bsd->bqhs', qib, ki).astype(jnp.float32)
        score = jnp.einsum('bqhs,bqh->bqs', jax.nn.relu(idx_logits), wib.astype(jnp.float32))
        score = jnp.where(admissible[None], score, -jnp.inf)
        _, sel = lax.top_k(score, n_sel)
        valid = sel < limit[None, :, None]
        k_sel = jax.vmap(lambda kk, ii: kk[ii])(k, sel)
        v_sel = jax.vmap(lambda vv, ii: vv[ii])(v, sel)
        qg = qb.reshape(b, Q_BLOCK, A_KV_HEADS, group, HEAD_DIM)
        logits = jnp.einsum('bqngd,bqknd->bqngk', qg, k_sel).astype(jnp.float32) * scale
        logits = jnp.where(valid[:, :, None, None, :], logits, -jnp.inf)
        p = jax.nn.softmax(logits, axis=-1).astype(v.dtype)
        o = jnp.einsum('bqngk,bqknd->bqngd', p, v_sel)
        return o.reshape(b, Q_BLOCK, A_Q)

    o = lax.map(block, (to_blocks(q), to_blocks(qi), to_blocks(wi), t_pos))
    return from_blocks(o) @ w_o


def shared_kv(h, g_kv, w_kv, b_f):
    b, s, _ = h.shape
    k, v, f_logit = jnp.split(rms_norm(h, g_kv) @ w_kv, [B_QK, 2 * B_QK], axis=-1)
    log_f = jax.nn.log_sigmoid((f_logit + b_f).astype(jnp.float32))
    cum = jnp.moveaxis(jnp.cumsum(log_f, axis=1), 2, 1)
    return (k.reshape(b, s, B_HEADS, HEAD_DIM), v.reshape(b, s, B_HEADS, HEAD_DIM), cum)


def fox_attention(xn, w_q, w_o, k, v, cum):
    b, s, _ = xn.shape
    q = (xn @ w_q).reshape(b, s, B_HEADS, HEAD_DIM)
    key_pos = jnp.arange(s)
    t_pos = key_pos.reshape(s // Q_BLOCK, Q_BLOCK)
    cum_q = to_blocks(jnp.moveaxis(cum, 1, 2))
    scale = HEAD_DIM ** -0.5

    def block(args):
        qb, cqb, tb = args
        logits = jnp.einsum('bqhd,bshd->bhqs', qb, k).astype(jnp.float32) * scale
        decay = jnp.moveaxis(cqb, 1, 2)[..., None] - cum[:, :, None, :]
        causal = key_pos[None, :] <= tb[:, None]
        logits = jnp.where(causal[None, None], logits + decay, -jnp.inf)
        p = jax.nn.softmax(logits, axis=-1).astype(v.dtype)
        o = jnp.einsum('bhqs,bshd->bqhd', p, v)
        return o.reshape(b, Q_BLOCK, B_QK)

    o = lax.map(block, (to_blocks(q), cum_q, t_pos))
    return from_blocks(o) @ w_o


def sq_relu_mlp(xn, w_up, w_down):
    return jnp.square(jax.nn.relu(xn @ w_up)) @ w_down


def setup_inputs(seed: int = 0) -> dict:
    key = jax.random.key(seed)
    ks = jax.random.split(key, 16)
    f32 = jnp.float32

    def w(k, shape, fan_in):
        return jax.random.normal(k, shape, f32) * (fan_in ** -0.5)

    def gain(k, shape):
        return 1.0 + 0.01 * jax.random.normal(k, shape, f32)

    return {
        "x": jax.random.normal(ks[0], (BATCH, SEQ, D_MODEL), f32),
        "g_attn_a": gain(ks[1], (N_A, D_MODEL)),
        "w_in_a": w(ks[2], (N_A, D_MODEL, A_IN), D_MODEL),
        "w_o_a": w(ks[3], (N_A, A_Q, D_MODEL), A_Q),
        "g_kv": gain(ks[4], (D_MODEL,)),
        "w_kv_b": w(ks[5], (D_MODEL, B_KV_IN), D_MODEL),
        "b_f": jax.random.uniform(ks[6], (B_HEADS,), f32, minval=1.0, maxval=5.0),
        "g_attn_b": gain(ks[7], (N_B, D_MODEL)),
        "w_q_b": w(ks[8], (N_B, D_MODEL, B_QK), D_MODEL),
        "w_o_b": w(ks[9], (N_B, B_QK, D_MODEL), B_QK),
        "g_mlp": gain(ks[10], (DEPTH, D_MODEL)),
        "w_up": w(ks[11], (DEPTH, D_MODEL, D_FF), D_MODEL),
        "w_down": w(ks[12], (DEPTH, D_FF, D_MODEL), D_FF),
        "g_final": gain(ks[13], (D_MODEL,)),
    }


def reference(x, g_attn_a, w_in_a, w_o_a, g_kv, w_kv_b, b_f, g_attn_b, w_q_b, w_o_b,
              g_mlp, w_up, w_down, g_final):
    h = x
    k_sh = v_sh = cum_sh = None
    for layer in range(DEPTH):
        if layer < N_A:
            h = h + dsa_attention(rms_norm(h, g_attn_a[layer]), w_in_a[layer], w_o_a[layer])
        else:
            if layer == N_A:
                k_sh, v_sh, cum_sh = shared_kv(h, g_kv, w_kv_b, b_f)
            j = layer - N_A
            h = h + fox_attention(rms_norm(h, g_attn_b[j]), w_q_b[j], w_o_b[j], k_sh, v_sh, cum_sh)
        h = h + sq_relu_mlp(rms_norm(h, g_mlp[layer]), w_up[layer], w_down[layer])
    return rms_norm(h, g_final)
```

```python
import functools

import jax
import jax.numpy as jnp
from jax import lax
from jax.experimental import pallas as pl
from jax.experimental.pallas import tpu as pltpu

LANES = 128
HEAD_DIM = 64
ROPE_HALF = HEAD_DIM // 2
CHUNK = 64
A_HEADS = 16
A_KV_HEADS = 4
A_GROUP = A_HEADS // A_KV_HEADS
IDX_HEADS = 8
TOPK_MAX = 256
B_HEADS = 16
ROPE_THETA = 10000.0
EPS = 1e-6
INT_MIN = -(2 ** 31)
MASKED = -1e30
VMEM_LIMIT_BYTES = 56 * 1024 * 1024

F32 = jnp.float32
BF16 = jnp.bfloat16
I32 = jnp.int32

_NT = (((1,), (1,)), ((), ()))


def _params(*sem):
    return pltpu.CompilerParams(dimension_semantics=sem,
                                vmem_limit_bytes=VMEM_LIMIT_BYTES)


def _resident(shape):
    nd = len(shape)
    return pl.BlockSpec(shape, lambda *_: (0,) * nd, pipeline_mode=pl.Buffered(1))


def _rms(x, g):
    return x * lax.rsqrt(jnp.mean(x * x, axis=-1, keepdims=True) + EPS) * g


A_Q = A_HEADS * HEAD_DIM
A_KD = A_KV_HEADS * LANES
A_QI = IDX_HEADS * HEAD_DIM
A_OFF_K = A_Q
A_OFF_V = A_OFF_K + A_KD
A_OFF_QI = A_OFF_V + A_KD
A_OFF_KI = A_OFF_QI + A_QI
A_OFF_WI = A_OFF_KI + LANES
A_COLS = A_OFF_WI + LANES


def _aproj_kernel(h_ref, g_ref, w_ref, cos_ref, sin_ref,
                  q_ref, k_ref, v_ref, qi_ref, ki_ref, wi_ref, *, wi_scale):
    tm = h_ref.shape[0]
    xn = _rms(h_ref[...], g_ref[...]).astype(BF16)
    cos = cos_ref[...]
    sin = sin_ref[...]
    lane = lax.broadcasted_iota(I32, (tm, LANES), 1)
    first_half = (lane & ROPE_HALF) == 0

    def rope(y):
        ys = jnp.where(first_half, pltpu.roll(y, LANES - ROPE_HALF, 1),
                       pltpu.roll(y, ROPE_HALF, 1))
        return y * cos + ys * sin

    def proj(c0, n):
        return jnp.dot(xn, w_ref[:, c0:c0 + n], preferred_element_type=F32)

    def emit(out_ref, c0, n, fn):
        step = 4 * LANES
        for s0 in range(0, n, step):
            sn = min(step, n - s0)
            y = proj(c0 + s0, sn)
            for j in range(sn // LANES):
                out_ref[:, s0 + j * LANES:s0 + (j + 1) * LANES] = fn(
                    y[:, j * LANES:(j + 1) * LANES]).astype(out_ref.dtype)

    emit(q_ref, 0, A_Q, lambda y: rope(y) * (HEAD_DIM ** -0.5))
    emit(k_ref, A_OFF_K, A_KD, rope)
    emit(v_ref, A_OFF_V, A_KD, lambda y: y)
    emit(qi_ref, A_OFF_QI, A_QI, rope)
    emit(ki_ref, A_OFF_KI, LANES, rope)
    emit(wi_ref, A_OFF_WI, LANES, lambda y: y * wi_scale)


def _aproj(h, g, w, cos, sin, seq, tm):
    t, d = h.shape
    npos = seq // tm
    row = lambda n: pl.BlockSpec((tm, n), lambda i: (i, 0))
    tab = pl.BlockSpec((tm, LANES), lambda i: (i % npos, 0))
    wi_scale = IDX_HEADS ** -0.5 * HEAD_DIM ** -0.5
    return pl.pallas_call(
        functools.partial(_aproj_kernel, wi_scale=wi_scale),
        grid=(t // tm,),
        in_specs=[row(d), _resident((1, d)), _resident((d, A_COLS)), tab, tab],
        out_specs=[row(A_Q), row(A_KD), row(A_KD), row(A_QI), row(LANES), row(LANES)],
        out_shape=[jax.ShapeDtypeStruct((t, A_Q), BF16),
                   jax.ShapeDtypeStruct((t, A_KD), BF16),
                   jax.ShapeDtypeStruct((t, A_KD), BF16),
                   jax.ShapeDtypeStruct((t, A_QI), BF16),
                   jax.ShapeDtypeStruct((t, LANES), BF16),
                   jax.ShapeDtypeStruct((t, LANES), F32)],
        compiler_params=_params("parallel"),
        name="a_proj",
    )(h, g, w, cos, sin)


def _nproj_kernel(h_ref, g_ref, w_ref, *out_refs, scales):
    xn = _rms(h_ref[...], g_ref[...]).astype(BF16)
    c0 = 0
    for out_ref, scale in zip(out_refs, scales):
        n = out_ref.shape[1]
        step = 4 * LANES
        for s0 in range(0, n, step):
            sn = min(step, n - s0)
            y = jnp.dot(xn, w_ref[:, c0 + s0:c0 + s0 + sn], preferred_element_type=F32)
            if scale != 1.0:
                y = y * scale
            out_ref[:, s0:s0 + sn] = y.astype(out_ref.dtype)
        c0 += n


def _nproj(h, g, w, segs, tm, name):
    t, d = h.shape
    row = lambda n: pl.BlockSpec((tm, n), lambda i: (i, 0))
    return pl.pallas_call(
        functools.partial(_nproj_kernel, scales=tuple(s[2] for s in segs)),
        grid=(t // tm,),
        in_specs=[row(d), _resident((1, d)), _resident(w.shape)],
        out_specs=[row(s[0]) for s in segs],
        out_shape=[jax.ShapeDtypeStruct((t, s[0]), s[1]) for s in segs],
        compiler_params=_params("parallel"),
        name=name,
    )(h, g, w)


def _dsa_kernel(q_ref, qi_ref, wi_ref, ki_ref, k_ref, v_ref, o_ref,
                key_sc, qis_sc, wb_sc, qst_sc, m_sc, l_sc, acc_sc, *, n_sel, idx_bits):
    tq = q_ref.shape[0]
    tk = tq
    j = pl.program_id(1)
    nkb = j + 1
    lane = lax.broadcasted_iota(I32, (tq, LANES), 1)
    lo_half = lane < HEAD_DIM
    rowpos = j * tq + lax.broadcasted_iota(I32, (tq, 1), 0)
    limit = (rowpos // CHUNK + 1) * CHUNK
    kiota = lax.broadcasted_iota(I32, (tq, tk), 1)

    def half_masked(chunk, h):
        return jnp.where(lo_half if h % 2 == 0 else ~lo_half, chunk, jnp.zeros_like(chunk))

    for h in range(IDX_HEADS):
        chunk = qi_ref[:, (h // 2) * LANES:(h // 2 + 1) * LANES]
        qis_sc[h * tq:(h + 1) * tq, :] = half_masked(chunk, h)
        wb_sc[h] = jnp.broadcast_to(wi_ref[:, h:h + 1], (tq, tk))

    def score_body(c, carry):
        kc = ki_ref[pl.ds(pl.multiple_of(c * tk, tk), tk), :]
        lg = lax.dot_general(qis_sc[...], kc, _NT, preferred_element_type=F32)
        sc = jnp.zeros((tq, tk), F32)
        for h in range(IDX_HEADS):
            sc = sc + jnp.maximum(lg[h * tq:(h + 1) * tq], 0.0) * wb_sc[h]
        sc = jnp.where(sc == 0.0, 0.0, sc)
        bits = pltpu.bitcast(sc, I32)
        key = bits ^ ((bits >> 31) & 0x7FFFFFFF)
        kpos = c * tk + kiota
        key_sc[c] = jnp.where(kpos < limit, key, INT_MIN)
        return carry

    lax.fori_loop(0, nkb, score_body, 0)

    def count(pred):
        def body(c, acc):
            return acc + jnp.where(pred(c, key_sc[c]), 1.0, 0.0)
        acc = lax.fori_loop(0, nkb, body, jnp.zeros((tq, tk), F32))
        return jnp.sum(acc, axis=1, keepdims=True)

    def try_threshold(cand, thr, cnt_thr):
        cand_b = jnp.broadcast_to(cand, (tq, tk))
        cnt = count(lambda c, key: key >= cand_b)
        ok = cnt >= n_sel
        return jnp.where(ok, cand, thr), jnp.where(ok, cnt, cnt_thr)

    thr0 = jnp.full((tq, 1), INT_MIN, I32)
    thr, cnt_thr = try_threshold(jnp.zeros((tq, 1), I32), thr0, jnp.zeros((tq, 1), F32))

    def bit_body(i, carry):
        thr, cnt_thr = carry
        cand = thr | jnp.left_shift(jnp.int32(1), 30 - i)
        return try_threshold(cand, thr, cnt_thr)

    thr, cnt_thr = lax.fori_loop(0, 31, bit_body, (thr, cnt_thr))
    thr = jnp.maximum(thr, INT_MIN + 1)
    thr_b = jnp.broadcast_to(thr, (tq, tk))

    @pl.when(jnp.max(cnt_thr) > n_sel)
    def _():
        need = n_sel - count(lambda c, key: key > thr_b)
        pos = jnp.zeros((tq, 1), I32)
        for b in range(idx_bits - 1, -1, -1):
            cand_b = jnp.broadcast_to(pos | (1 << b), (tq, tk))
            cnt = count(lambda c, key: jnp.where(key == thr_b, c * tk + kiota, 2 ** 30) < cand_b)
            pos = jnp.where(cnt < need, pos | (1 << b), pos)
        pos_b = jnp.broadcast_to(pos, (tq, tk))

        def demote(c, carry):
            key = key_sc[c]
            tie_pos = jnp.where(key == thr_b, c * tk + kiota, -1)
            key_sc[c] = jnp.where(tie_pos > pos_b, INT_MIN, key)
            return carry

        lax.fori_loop(0, nkb, demote, 0)

    for h in range(A_HEADS):
        chunk = q_ref[:, (h // 2) * LANES:(h // 2 + 1) * LANES]
        qst_sc[h * tq:(h + 1) * tq, :] = half_masked(chunk, h)
    m_sc[...] = jnp.full(m_sc.shape, MASKED, F32)
    l_sc[...] = jnp.zeros(l_sc.shape, F32)
    acc_sc[...] = jnp.zeros(acc_sc.shape, F32)

    def attn_body(c, carry):
        r0 = pl.multiple_of(c * tk, tk)
        sel = key_sc[c] >= thr_b
        for n in range(A_KV_HEADS):
            kc = k_ref[pl.ds(r0, tk), n * LANES:(n + 1) * LANES]
            vc = v_ref[pl.ds(r0, tk), n * LANES:(n + 1) * LANES]
            g0 = n * A_GROUP * tq
            lg = lax.dot_general(qst_sc[g0:g0 + A_GROUP * tq, :], kc, _NT,
                                 preferred_element_type=F32)
            for g in range(A_GROUP):
                h = n * A_GROUP + g
                s = jnp.where(sel, lg[g * tq:(g + 1) * tq], MASKED)
                m_old = m_sc[h]
                m_new = jnp.maximum(m_old, jnp.max(s, axis=1, keepdims=True))
                alpha = jnp.exp(m_old - m_new)
                p = jnp.exp(s - m_new)
                l_sc[h] = alpha * l_sc[h] + jnp.sum(p, axis=1, keepdims=True)
                acc_sc[h] = alpha * acc_sc[h] + jnp.dot(
                    p.astype(BF16), vc, preferred_element_type=F32)
                m_sc[h] = m_new
        return carry

    lax.fori_loop(0, nkb, attn_body, 0)

    for i in range(A_HEADS // 2):
        oa = acc_sc[2 * i] / l_sc[2 * i]
        ob = acc_sc[2 * i + 1] / l_sc[2 * i + 1]
        o_ref[:, i * LANES:(i + 1) * LANES] = jnp.where(lo_half, oa, ob).astype(o_ref.dtype)


def _dsa(q, qi, wi, ki, k, v, tq):
    b, s, _ = q.shape
    n_sel = min(TOPK_MAX, s // 4)
    idx_bits = max(1, (s - 1).bit_length())
    qblk = lambda n: pl.BlockSpec((None, tq, n), lambda bi, j: (bi, j, 0))
    full = lambda n: pl.BlockSpec((None, s, n), lambda bi, j: (bi, 0, 0))
    return pl.pallas_call(
        functools.partial(_dsa_kernel, n_sel=float(n_sel), idx_bits=idx_bits),
        grid=(b, s // tq),
        in_specs=[qblk(A_Q), qblk(A_QI), qblk(LANES), full(LANES), full(A_KD), full(A_KD)],
        out_specs=qblk(A_Q),
        out_shape=jax.ShapeDtypeStruct((b, s, A_Q), BF16),
        scratch_shapes=[
            pltpu.VMEM((s // tq, tq, tq), I32),
            pltpu.VMEM((IDX_HEADS * tq, LANES), BF16),
            pltpu.VMEM((IDX_HEADS, tq, tq), F32),
            pltpu.VMEM((A_HEADS * tq, LANES), BF16),
            pltpu.VMEM((A_HEADS, tq, 1), F32),
            pltpu.VMEM((A_HEADS, tq, 1), F32),
            pltpu.VMEM((A_HEADS, tq, LANES), F32),
        ],
        compiler_params=_params("parallel", "arbitrary"),
        name="dsa_attention",
    )(q, qi, wi, ki, k, v)


def _cumsum_kernel(f_ref, b_ref, o_ref, carry_sc):
    tc = f_ref.shape[0]

    @pl.when(pl.program_id(1) == 0)
    def _():
        carry_sc[...] = jnp.zeros(carry_sc.shape, F32)

    z = f_ref[...] + b_ref[...]
    logf = jnp.minimum(z, 0.0) - jnp.log1p(jnp.exp(-jnp.abs(z)))
    tri = (lax.broadcasted_iota(I32, (tc, tc), 0) >=
           lax.broadcasted_iota(I32, (tc, tc), 1)).astype(BF16)
    hi = logf.astype(BF16)
    r1 = logf - hi.astype(F32)
    mid = r1.astype(BF16)
    low = (r1 - mid.astype(F32)).astype(BF16)
    cs = (jnp.dot(tri, hi, preferred_element_type=F32) +
          jnp.dot(tri, mid, preferred_element_type=F32) +
          jnp.dot(tri, low, preferred_element_type=F32))
    out = cs + carry_sc[...]
    o_ref[...] = out
    carry_sc[...] = out[tc - 1:tc, :]


def _cumsum(f, bias, tc):
    b, s, n = f.shape
    blk = pl.BlockSpec((None, tc, n), lambda bi, i: (bi, i, 0))
    return pl.pallas_call(
        _cumsum_kernel,
        grid=(b, s // tc),
        in_specs=[blk, _resident((1, n))],
        out_specs=blk,
        out_shape=jax.ShapeDtypeStruct((b, s, n), F32),
        scratch_shapes=[pltpu.VMEM((1, n), F32)],
        compiler_params=_params("parallel", "arbitrary"),
        name="forget_cumsum",
    )(f, bias)


def _fox_kernel(q_ref, k_ref, v_ref, cq_ref, ck_ref, o_ref, qst_sc, m_sc, l_sc, acc_sc):
    tq = q_ref.shape[0]
    tk = tq
    j = pl.program_id(2)
    lane = lax.broadcasted_iota(I32, (tq, LANES), 1)
    lo_half = lane < HEAD_DIM
    q2 = q_ref[...]
    qst_sc[0:tq, :] = jnp.where(lo_half, q2, jnp.zeros_like(q2))
    qst_sc[tq:2 * tq, :] = jnp.where(lo_half, jnp.zeros_like(q2), q2)
    m_sc[...] = jnp.full(m_sc.shape, MASKED, F32)
    l_sc[...] = jnp.zeros(l_sc.shape, F32)
    acc_sc[...] = jnp.zeros(acc_sc.shape, F32)
    cq = cq_ref[...]
    causal = (lax.broadcasted_iota(I32, (tq, tk), 1) <=
              lax.broadcasted_iota(I32, (tq, tk), 0))

    def block(c, diagonal):
        r0 = pl.multiple_of(c * tk, tk)
        kc = k_ref[pl.ds(r0, tk), :]
        vc = v_ref[pl.ds(r0, tk), :]
        ckc = ck_ref[c]
        lg = lax.dot_general(qst_sc[...], kc, _NT, preferred_element_type=F32)
        for hh in range(2):
            s = lg[hh * tq:(hh + 1) * tq] + (cq[:, hh:hh + 1] - ckc[hh:hh + 1, :])
            if diagonal:
                s = jnp.where(causal, s, MASKED)
            m_old = m_sc[hh]
            m_new = jnp.maximum(m_old, jnp.max(s, axis=1, keepdims=True))
            alpha = jnp.exp(m_old - m_new)
            p = jnp.exp(s - m_new)
            l_sc[hh] = alpha * l_sc[hh] + jnp.sum(p, axis=1, keepdims=True)
            acc_sc[hh] = alpha * acc_sc[hh] + jnp.dot(
                p.astype(BF16), vc, preferred_element_type=F32)
            m_sc[hh] = m_new

    def off_diagonal(c, carry):
        block(c, False)
        return carry

    lax.fori_loop(0, j, off_diagonal, 0)
    block(j, True)
    oa = acc_sc[0] / l_sc[0]
    ob = acc_sc[1] / l_sc[1]
    o_ref[...] = jnp.where(lo_half, oa, ob).astype(o_ref.dtype)


def _fox(q, k, v, cq, ck, tq):
    b, s, dq = q.shape
    pairs = dq // LANES
    return pl.pallas_call(
        _fox_kernel,
        grid=(b, pairs, s // tq),
        in_specs=[
            pl.BlockSpec((None, tq, LANES), lambda bi, p, j: (bi, j, p)),
            pl.BlockSpec((None, s, LANES), lambda bi, p, j: (bi, 0, p)),
            pl.BlockSpec((None, s, LANES), lambda bi, p, j: (bi, 0, p)),
            pl.BlockSpec((None, None, tq, 2), lambda bi, p, j: (bi, p, j, 0)),
            pl.BlockSpec((None, None, s // tq, 2, tq), lambda bi, p, j: (bi, p, 0, 0, 0)),
        ],
        out_specs=pl.BlockSpec((None, tq, LANES), lambda bi, p, j: (bi, j, p)),
        out_shape=jax.ShapeDtypeStruct((b, s, dq), BF16),
        scratch_shapes=[
            pltpu.VMEM((2 * tq, LANES), BF16),
            pltpu.VMEM((2, tq, 1), F32),
            pltpu.VMEM((2, tq, 1), F32),
            pltpu.VMEM((2, tq, LANES), F32),
        ],
        compiler_params=_params("parallel", "parallel", "arbitrary"),
        name="fox_attention",
    )(q, k, v, cq, ck)


def _mlp_kernel(h_ref, o_ref, wo_ref, g_ref, wup_ref, wdn_ref, gf_ref, out_ref, *, tf, final):
    h1 = h_ref[...] + jnp.dot(o_ref[...], wo_ref[...], preferred_element_type=F32)
    xn = _rms(h1, g_ref[...]).astype(BF16)
    acc = h1
    for c0 in range(0, wup_ref.shape[1], tf):
        u = jnp.dot(xn, wup_ref[:, c0:c0 + tf], preferred_element_type=F32)
        a = jnp.square(jnp.maximum(u, 0.0)).astype(BF16)
        acc = acc + jnp.dot(a, wdn_ref[c0:c0 + tf, :], preferred_element_type=F32)
    if final:
        acc = _rms(acc, gf_ref[...])
    out_ref[...] = acc


def _mlp(h, o, wo, g, wup, wdn, gf, final, tm, tf):
    t, d = h.shape
    row = pl.BlockSpec((tm, d), lambda i: (i, 0))
    return pl.pallas_call(
        functools.partial(_mlp_kernel, tf=tf, final=final),
        grid=(t // tm,),
        in_specs=[row, row, _resident(wo.shape), _resident((1, d)), _resident(wup.shape),
                  _resident(wdn.shape), _resident((1, d))],
        out_specs=row,
        out_shape=jax.ShapeDtypeStruct((t, d), F32),
        compiler_params=_params("parallel"),
        name="oproj_mlp",
    )(h, o, wo, g, wup, wdn, gf)


def _dup_heads(w, heads):
    d = w.shape[0]
    w = w.reshape(d, heads, HEAD_DIM)
    return jnp.concatenate([w, w], axis=-1).reshape(d, heads * LANES)


def _prep_w_in_a(w):
    d = w.shape[0]
    kv = A_KV_HEADS * HEAD_DIM
    o_k, o_v, o_qi = A_Q, A_Q + kv, A_Q + 2 * kv
    o_ki, o_wi = o_qi + A_QI, o_qi + A_QI + HEAD_DIM
    wi = w[:, o_wi:o_wi + IDX_HEADS]
    return jnp.concatenate([
        w[:, :A_Q],
        _dup_heads(w[:, o_k:o_v], A_KV_HEADS),
        _dup_heads(w[:, o_v:o_qi], A_KV_HEADS),
        w[:, o_qi:o_ki],
        _dup_heads(w[:, o_ki:o_wi], 1),
        jnp.pad(wi, ((0, 0), (0, LANES - IDX_HEADS))),
    ], axis=1).astype(BF16)


def _rope_tables(seq):
    inv = 1.0 / (ROPE_THETA ** (jnp.arange(0, HEAD_DIM, 2, dtype=F32) / HEAD_DIM))
    ang = jnp.arange(seq, dtype=F32)[:, None] * inv[None, :]
    c, s = jnp.cos(ang), jnp.sin(ang)
    return jnp.concatenate([c, c, c, c], axis=1), jnp.concatenate([-s, s, -s, s], axis=1)


def kernel(x, g_attn_a, w_in_a, w_o_a, g_kv, w_kv_b, b_f, g_attn_b, w_q_b, w_o_b,
           g_mlp, w_up, w_down, g_final):
    b, s, d = x.shape
    t = b * s
    depth = g_mlp.shape[0]
    n_a = g_attn_a.shape[0]
    tm = min(512, s)
    tq_a = min(128, s)
    tq_b = min(256, s)
    tf = 512

    cos, sin = _rope_tables(s)
    gf = g_final.reshape(1, d)
    h = x.reshape(t, d)
    k_sh = v_sh = cq = ck = None

    for layer in range(depth):
        if layer < n_a:
            q, k, v, qi, ki, wi = _aproj(h, g_attn_a[layer].reshape(1, d),
                                         _prep_w_in_a(w_in_a[layer]), cos, sin, s, tm)
            r3 = lambda a: a.reshape(b, s, a.shape[-1])
            o = _dsa(r3(q), r3(qi), r3(wi), r3(ki), r3(k), r3(v), tq_a)
            w_o = w_o_a[layer]
        else:
            jb = layer - n_a
            if jb == 0:
                bqk = B_HEADS * HEAD_DIM
                w_kv = jnp.concatenate(
                    [w_kv_b, jnp.zeros((d, LANES - B_HEADS), w_kv_b.dtype)], axis=1).astype(BF16)
                k_sh, v_sh, f_logit = _nproj(
                    h, g_kv.reshape(1, d), w_kv,
                    [(bqk, BF16, 1.0), (bqk, BF16, 1.0), (LANES, F32, 1.0)], tm, "kv_proj")
                bias = jnp.pad(b_f, (0, LANES - B_HEADS)).reshape(1, LANES)
                cum = _cumsum(f_logit.reshape(b, s, LANES), bias, tm)[:, :, :B_HEADS]
                pairs = B_HEADS // 2
                cq = cum.reshape(b, s, pairs, 2).transpose(0, 2, 1, 3)
                ck = cum.reshape(b, s // tq_b, tq_b, pairs, 2).transpose(0, 3, 1, 4, 2)
                k_sh = k_sh.reshape(b, s, bqk)
                v_sh = v_sh.reshape(b, s, bqk)
            (q,) = _nproj(h, g_attn_b[jb].reshape(1, d), w_q_b[jb].astype(BF16),
                          [(B_HEADS * HEAD_DIM, BF16, HEAD_DIM ** -0.5)], tm, "q_proj")
            o = _fox(q.reshape(b, s, -1), k_sh, v_sh, cq, ck, tq_b)
            w_o = w_o_b[jb]
        h = _mlp(h, o.reshape(t, -1), w_o.astype(BF16), g_mlp[layer].reshape(1, d),
                 w_up[layer].astype(BF16), w_down[layer].astype(BF16), gf,
                 layer == depth - 1, tm, tf)
    return h.reshape(b, s, d)
```

```python
import functools

import numpy as np
import jax
import jax.numpy as jnp
from jax import lax
from jax.experimental import pallas as pl
from jax.experimental.pallas import tpu as pltpu

LANES = 128
SUBLANES = 8
BF16_ROWS = 16
HEAD_DIM = 64
ROPE_HALF = HEAD_DIM // 2
CHUNK = 64
A_HEADS = 16
A_KV_HEADS = 4
A_GROUP = A_HEADS // A_KV_HEADS
IDX_HEADS = 8
TOPK_MAX = 256
B_HEADS = 16
ROPE_THETA = 10000.0
EPS = 1e-6
INT_MIN = -(2 ** 31)
MASKED = -1e30
VMEM_LIMIT_BYTES = 56 * 1024 * 1024
ATTN_BLOCK = 256

F32 = jnp.float32
BF16 = jnp.bfloat16
I32 = jnp.int32


def _params(*sem):
    return pltpu.CompilerParams(dimension_semantics=sem,
                                vmem_limit_bytes=VMEM_LIMIT_BYTES)


def _resident(shape):
    nd = len(shape)
    return pl.BlockSpec(shape, lambda *_: (0,) * nd, pipeline_mode=pl.Buffered(1))


def _rms(x, g):
    return x * lax.rsqrt(jnp.mean(x * x, axis=-1, keepdims=True) + EPS) * g


def _colsum8(x):
    n, w = x.shape
    return x.reshape(n // SUBLANES, SUBLANES, w).sum(axis=0)


A_Q = A_HEADS * HEAD_DIM
A_KD = A_KV_HEADS * LANES
A_V = A_KV_HEADS * HEAD_DIM
A_QI = IDX_HEADS * HEAD_DIM
A_OFF_K = A_Q
A_OFF_V = A_OFF_K + A_KD
A_OFF_QI = A_OFF_V + A_V
A_OFF_KI = A_OFF_QI + A_QI
A_OFF_WI = A_OFF_KI + LANES
A_COLS = A_OFF_WI + LANES


def _aproj_kernel(h_ref, g_ref, w_ref, cos_ref, sin_ref,
                  qt_ref, k_ref, vt_ref, qit_ref, ki_ref, wit_ref, *, wi_scale):
    tm = h_ref.shape[0]
    tk = vt_ref.shape[2]
    xn = _rms(h_ref[...], g_ref[...]).astype(BF16)
    cos = cos_ref[...]
    sin = sin_ref[...]
    lane = lax.broadcasted_iota(I32, (tm, LANES), 1)
    first_half = (lane & ROPE_HALF) == 0

    def rope(y):
        ys = jnp.where(first_half, pltpu.roll(y, LANES - ROPE_HALF, 1),
                       pltpu.roll(y, ROPE_HALF, 1))
        return y * cos + ys * sin

    def chunks(c0, n):
        step = 4 * LANES
        for s0 in range(0, n, step):
            sn = min(step, n - s0)
            y = jnp.dot(xn, w_ref[:, c0 + s0:c0 + s0 + sn], preferred_element_type=F32)
            for j in range(sn // LANES):
                yield (s0 // LANES + j), y[:, j * LANES:(j + 1) * LANES]

    for j, y in chunks(0, A_Q):
        qt_ref[j * LANES:(j + 1) * LANES, :] = (rope(y) * (HEAD_DIM ** -0.5)).T.astype(BF16)
    for j, y in chunks(A_OFF_K, A_KD):
        k_ref[:, j * LANES:(j + 1) * LANES] = rope(y).astype(BF16)
    for j, y in chunks(A_OFF_V, A_V):
        for sb in range(tm // tk):
            vt_ref[sb, j * LANES:(j + 1) * LANES, :] = y[sb * tk:(sb + 1) * tk].T.astype(BF16)
    for j, y in chunks(A_OFF_QI, A_QI):
        qit_ref[j * LANES:(j + 1) * LANES, :] = rope(y).T.astype(BF16)
    for j, y in chunks(A_OFF_KI, LANES):
        ki_ref[...] = rope(y).astype(BF16)
    for j, y in chunks(A_OFF_WI, LANES):
        wit_ref[...] = (y * wi_scale).T[:IDX_HEADS, :]


def _aproj(h, g, w, cos, sin, seq, tm, tk):
    t, d = h.shape
    npos = seq // tm
    row = lambda n: pl.BlockSpec((tm, n), lambda i: (i, 0))
    col = lambda n: pl.BlockSpec((n, tm), lambda i: (0, i))
    tab = pl.BlockSpec((tm, LANES), lambda i: (i % npos, 0))
    wi_scale = IDX_HEADS ** -0.5 * HEAD_DIM ** -0.5
    return pl.pallas_call(
        functools.partial(_aproj_kernel, wi_scale=wi_scale),
        grid=(t // tm,),
        in_specs=[row(d), _resident((1, d)), _resident((d, A_COLS)), tab, tab],
        out_specs=[col(A_Q), row(A_KD),
                   pl.BlockSpec((tm // tk, A_V, tk), lambda i: (i, 0, 0)),
                   col(A_QI), row(LANES), col(IDX_HEADS)],
        out_shape=[jax.ShapeDtypeStruct((A_Q, t), BF16),
                   jax.ShapeDtypeStruct((t, A_KD), BF16),
                   jax.ShapeDtypeStruct((t // tk, A_V, tk), BF16),
                   jax.ShapeDtypeStruct((A_QI, t), BF16),
                   jax.ShapeDtypeStruct((t, LANES), BF16),
                   jax.ShapeDtypeStruct((IDX_HEADS, t), F32)],
        compiler_params=_params("parallel"),
        name="a_proj",
    )(h, g, w, cos, sin)


def _dsa_kernel(qt_ref, qit_ref, wit_ref, ki_ref, k_ref, vt_ref, o_ref,
                key_sc, qiw_sc, qw_sc, m_sc, l_sc, acc_sc, *, n_sel, idx_bits):
    tq = qt_ref.shape[1]
    tk = tq
    j = pl.program_id(1)
    nkb = j + 1
    qpos = j * tq + lax.broadcasted_iota(I32, (1, tq), 1)
    limit = (qpos // CHUNK + 1) * CHUNK
    kiota = lax.broadcasted_iota(I32, (tk, tq), 0)
    zeros_half = jnp.zeros((HEAD_DIM, tq), BF16)

    for h in range(IDX_HEADS):
        qiw_sc[h, 0:HEAD_DIM, :] = qit_ref[h * HEAD_DIM:(h + 1) * HEAD_DIM, :]
        qiw_sc[h, HEAD_DIM:LANES, :] = zeros_half
    for h in range(A_HEADS):
        qw_sc[h, 0:HEAD_DIM, :] = qt_ref[h * HEAD_DIM:(h + 1) * HEAD_DIM, :]
        qw_sc[h, HEAD_DIM:LANES, :] = zeros_half

    def score_body(c, carry):
        kc = ki_ref[pl.ds(pl.multiple_of(c * tk, tk), tk), :]
        sc = jnp.zeros((tk, tq), F32)
        for h in range(IDX_HEADS):
            lg = jnp.dot(kc, qiw_sc[h], preferred_element_type=F32)
            sc = sc + jnp.maximum(lg, 0.0) * wit_ref[h:h + 1, :]
        sc = jnp.where(sc == 0.0, 0.0, sc)
        bits = pltpu.bitcast(sc, I32)
        key = bits ^ ((bits >> 31) & 0x7FFFFFFF)
        key_sc[c] = jnp.where(c * tk + kiota < limit, key, INT_MIN)
        return carry

    lax.fori_loop(0, nkb, score_body, 0)

    def count(pred):
        def body(c, acc):
            return acc + _colsum8(jnp.where(pred(c, key_sc[c]), 1.0, 0.0))
        acc = lax.fori_loop(0, nkb, body, jnp.zeros((SUBLANES, tq), F32))
        return jnp.sum(acc, axis=0, keepdims=True)

    def try_threshold(cand, thr, cnt_thr):
        cnt = count(lambda c, key: key >= cand)
        ok = cnt >= n_sel
        return jnp.where(ok, cand, thr), jnp.where(ok, cnt, cnt_thr)

    thr0 = jnp.full((1, tq), INT_MIN, I32)
    thr, cnt_thr = try_threshold(jnp.zeros((1, tq), I32), thr0, jnp.zeros((1, tq), F32))

    def bit_body(i, carry):
        thr, cnt_thr = carry
        cand = thr | jnp.left_shift(jnp.int32(1), 30 - i)
        return try_threshold(cand, thr, cnt_thr)

    thr, cnt_thr = lax.fori_loop(0, 31, bit_body, (thr, cnt_thr))
    thr = jnp.maximum(thr, INT_MIN + 1)

    @pl.when(jnp.max(cnt_thr) > n_sel)
    def _():
        need = n_sel - count(lambda c, key: key > thr)
        pos = jnp.zeros((1, tq), I32)
        for b in range(idx_bits - 1, -1, -1):
            cand = pos | (1 << b)
            cnt = count(lambda c, key: jnp.where(key == thr, c * tk + kiota, 2 ** 30) < cand)
            pos = jnp.where(cnt < need, cand, pos)

        def demote(c, carry):
            key = key_sc[c]
            tie_pos = jnp.where(key == thr, c * tk + kiota, -1)
            key_sc[c] = jnp.where(tie_pos > pos, INT_MIN, key)
            return carry

        lax.fori_loop(0, nkb, demote, 0)

    m_sc[...] = jnp.full(m_sc.shape, MASKED, F32)
    l_sc[...] = jnp.zeros(l_sc.shape, F32)
    acc_sc[...] = jnp.zeros(acc_sc.shape, F32)

    def attn_body(c, carry):
        r0 = pl.multiple_of(c * tk, tk)
        sel = key_sc[c] >= thr
        vt = vt_ref[c]
        for n in range(A_KV_HEADS):
            kc = k_ref[pl.ds(r0, tk), n * LANES:(n + 1) * LANES]
            vtn = vt[n * HEAD_DIM:(n + 1) * HEAD_DIM, :]
            ss = [jnp.dot(kc, qw_sc[n * A_GROUP + g], preferred_element_type=F32)
                  for g in range(A_GROUP)]
            for g in range(A_GROUP):
                h = n * A_GROUP + g
                s = jnp.where(sel, ss[g], MASKED)
                m_old = m_sc[h]
                m_new = jnp.maximum(m_old, jnp.max(s, axis=0, keepdims=True))
                alpha = jnp.exp(m_old - m_new)
                p = jnp.exp(s - m_new)
                l_sc[h] = alpha * l_sc[h] + _colsum8(p)
                acc_sc[h] = alpha * acc_sc[h] + jnp.dot(
                    vtn, p.astype(BF16), preferred_element_type=F32)
                m_sc[h] = m_new
        return carry

    lax.fori_loop(0, nkb, attn_body, 0)

    for i in range(A_HEADS // 2):
        pair = [acc_sc[h] / jnp.sum(l_sc[h], axis=0, keepdims=True) for h in (2 * i, 2 * i + 1)]
        o_ref[:, i * LANES:(i + 1) * LANES] = jnp.concatenate(pair, axis=0).T.astype(o_ref.dtype)


def _dsa(qt, qit, wit, ki, k, vt, b, s, tq):
    t = b * s
    nq = s // tq
    n_sel = min(TOPK_MAX, s // 4)
    idx_bits = max(1, (s - 1).bit_length())
    qcol = lambda n: pl.BlockSpec((n, tq), lambda bi, j: (0, bi * nq + j))
    full = lambda n: pl.BlockSpec((s, n), lambda bi, j: (bi, 0))
    return pl.pallas_call(
        functools.partial(_dsa_kernel, n_sel=float(n_sel), idx_bits=idx_bits),
        grid=(b, nq),
        in_specs=[qcol(A_Q), qcol(A_QI), qcol(IDX_HEADS), full(LANES), full(A_KD),
                  pl.BlockSpec((nq, A_V, tq), lambda bi, j: (bi, 0, 0))],
        out_specs=pl.BlockSpec((tq, A_Q), lambda bi, j: (bi * nq + j, 0)),
        out_shape=jax.ShapeDtypeStruct((t, A_Q), BF16),
        scratch_shapes=[
            pltpu.VMEM((nq, tq, tq), I32),
            pltpu.VMEM((IDX_HEADS, LANES, tq), BF16),
            pltpu.VMEM((A_HEADS, LANES, tq), BF16),
            pltpu.VMEM((A_HEADS, 1, tq), F32),
            pltpu.VMEM((A_HEADS, SUBLANES, tq), F32),
            pltpu.VMEM((A_HEADS, HEAD_DIM, tq), F32),
        ],
        compiler_params=_params("parallel", "arbitrary"),
        name="dsa_attention",
    )(qt, qit, wit, ki, k, vt)


B_QK = B_HEADS * HEAD_DIM
B_KAUG = B_HEADS * LANES
BIAS_PIECES = 3


def _split3(x):
    hi = x.astype(BF16)
    r1 = x - hi.astype(F32)
    mid = r1.astype(BF16)
    low = (r1 - mid.astype(F32)).astype(BF16)
    return hi, mid, low


def _kvproj_kernel(h_ref, g_ref, wk_ref, wv_ref, wf_ref, bf_ref, place_ref, ones_ref,
                   kaug_ref, vt_ref, chi_ref, cmid_ref, clo_ref, carry_sc):
    tm = h_ref.shape[0]
    tk = vt_ref.shape[2]

    @pl.when(pl.program_id(1) == 0)
    def _():
        carry_sc[...] = jnp.zeros(carry_sc.shape, F32)

    xn = _rms(h_ref[...], g_ref[...]).astype(BF16)
    z = jnp.dot(xn, wf_ref[...], preferred_element_type=F32) + bf_ref[...]
    logf = jnp.minimum(z, 0.0) - jnp.log1p(jnp.exp(-jnp.abs(z)))
    tri = (lax.broadcasted_iota(I32, (tm, tm), 0) >=
           lax.broadcasted_iota(I32, (tm, tm), 1)).astype(BF16)
    cs = sum(jnp.dot(tri, piece, preferred_element_type=F32) for piece in _split3(logf))
    cum = cs + carry_sc[...]
    carry_sc[...] = cum[tm - 1:tm, :]
    hi, mid, low = _split3(cum)
    chi_ref[...] = hi
    cmid_ref[...] = mid
    clo_ref[...] = low
    pieces = jnp.concatenate([hi, mid, low], axis=1)

    step = 4 * LANES
    for c0 in range(0, B_KAUG, step):
        y = (jnp.dot(xn, wk_ref[:, c0:c0 + step], preferred_element_type=F32) +
             jnp.dot(pieces, place_ref[:, c0:c0 + step], preferred_element_type=F32) +
             ones_ref[:, c0:c0 + step])
        kaug_ref[:, c0:c0 + step] = y.astype(BF16)
    for c0 in range(0, B_QK, step):
        y = jnp.dot(xn, wv_ref[:, c0:c0 + step], preferred_element_type=F32)
        for jj in range(step // LANES):
            r = c0 + jj * LANES
            for sb in range(tm // tk):
                vt_ref[sb, r:r + LANES, :] = (
                    y[sb * tk:(sb + 1) * tk, jj * LANES:(jj + 1) * LANES].T.astype(BF16))


def _kvproj(h3, g, wk, wv, wf, bf, place, ones, tm, tk):
    b, s, d = h3.shape
    ns = s // tm
    blk = lambda n: pl.BlockSpec((None, tm, n), lambda bi, i: (bi, i, 0))
    return pl.pallas_call(
        _kvproj_kernel,
        grid=(b, ns),
        in_specs=[blk(d), _resident((1, d)), _resident(wk.shape), _resident(wv.shape),
                  _resident(wf.shape), _resident((1, LANES)), _resident(place.shape),
                  _resident(ones.shape)],
        out_specs=[blk(B_KAUG),
                   pl.BlockSpec((tm // tk, B_QK, tk), lambda bi, i: (bi * ns + i, 0, 0)),
                   blk(LANES), blk(LANES), blk(LANES)],
        out_shape=[jax.ShapeDtypeStruct((b, s, B_KAUG), BF16),
                   jax.ShapeDtypeStruct((b * s // tk, B_QK, tk), BF16),
                   jax.ShapeDtypeStruct((b, s, LANES), BF16),
                   jax.ShapeDtypeStruct((b, s, LANES), BF16),
                   jax.ShapeDtypeStruct((b, s, LANES), BF16)],
        scratch_shapes=[pltpu.VMEM((1, LANES), F32)],
        compiler_params=_params("parallel", "arbitrary"),
        name="kv_proj_cumsum",
    )(h3, g, wk, wv, wf, bf, place, ones)


def _qproj_kernel(h_ref, g_ref, w_ref, rows_ref, qt_ref):
    tm = h_ref.shape[0]
    xn = _rms(h_ref[...], g_ref[...]).astype(BF16)
    pad = jnp.zeros((LANES - HEAD_DIM - BF16_ROWS, tm), BF16)
    step = 4 * LANES
    for c0 in range(0, B_QK, step):
        y = jnp.dot(xn, w_ref[:, c0:c0 + step], preferred_element_type=F32) * (HEAD_DIM ** -0.5)
        for jj in range(step // LANES):
            yt = y[:, jj * LANES:(jj + 1) * LANES].T.astype(BF16)
            for e in range(2):
                h = (c0 + jj * LANES) // HEAD_DIM + e
                r = h * LANES
                qt_ref[r:r + HEAD_DIM, :] = yt[e * HEAD_DIM:(e + 1) * HEAD_DIM, :]
                qt_ref[r + HEAD_DIM:r + HEAD_DIM + BF16_ROWS, :] = (
                    rows_ref[h * BF16_ROWS:(h + 1) * BF16_ROWS, :])
                qt_ref[r + HEAD_DIM + BF16_ROWS:r + LANES, :] = pad


def _qproj(h, g, w, rows, tm):
    t, d = h.shape
    return pl.pallas_call(
        _qproj_kernel,
        grid=(t // tm,),
        in_specs=[pl.BlockSpec((tm, d), lambda i: (i, 0)), _resident((1, d)), _resident(w.shape),
                  pl.BlockSpec((B_HEADS * BF16_ROWS, tm), lambda i: (0, i))],
        out_specs=pl.BlockSpec((B_KAUG, tm), lambda i: (0, i)),
        out_shape=jax.ShapeDtypeStruct((B_KAUG, t), BF16),
        compiler_params=_params("parallel"),
        name="q_proj",
    )(h, g, w, rows)


FOX_HEADS_PER_STEP = 4


def _fox_kernel(qt_ref, k_ref, vt_ref, o_ref, m_sc, l_sc, acc_sc):
    tq = qt_ref.shape[1]
    tk = tq
    hps = FOX_HEADS_PER_STEP
    j = pl.program_id(2)
    m_sc[...] = jnp.full(m_sc.shape, MASKED, F32)
    l_sc[...] = jnp.zeros(l_sc.shape, F32)
    acc_sc[...] = jnp.zeros(acc_sc.shape, F32)
    causal = (lax.broadcasted_iota(I32, (tk, tq), 0) <=
              lax.broadcasted_iota(I32, (tk, tq), 1))

    def block(c, diagonal):
        r0 = pl.multiple_of(c * tk, tk)
        vt = vt_ref[c]
        ss = [jnp.dot(k_ref[pl.ds(r0, tk), h * LANES:(h + 1) * LANES],
                      qt_ref[h * LANES:(h + 1) * LANES, :], preferred_element_type=F32)
              for h in range(hps)]
        for h in range(hps):
            s = ss[h]
            if diagonal:
                s = jnp.where(causal, s, MASKED)
            m_old = m_sc[h]
            m_new = jnp.maximum(m_old, jnp.max(s, axis=0, keepdims=True))
            alpha = jnp.exp(m_old - m_new)
            p = jnp.exp(s - m_new)
            l_sc[h] = alpha * l_sc[h] + _colsum8(p)
            acc_sc[h] = alpha * acc_sc[h] + jnp.dot(
                vt[h * HEAD_DIM:(h + 1) * HEAD_DIM, :], p.astype(BF16),
                preferred_element_type=F32)
            m_sc[h] = m_new

    def off_diagonal(c, carry):
        block(c, False)
        return carry

    lax.fori_loop(0, j, off_diagonal, 0)
    block(j, True)
    outs = [acc_sc[h] / jnp.sum(l_sc[h], axis=0, keepdims=True) for h in range(hps)]
    o_ref[...] = jnp.concatenate(outs, axis=0).T.astype(o_ref.dtype)


def _fox(qt, kaug, vt, b, s, tq):
    t = b * s
    nq = s // tq
    hps = FOX_HEADS_PER_STEP
    return pl.pallas_call(
        _fox_kernel,
        grid=(b, B_HEADS // hps, nq),
        in_specs=[
            pl.BlockSpec((hps * LANES, tq), lambda bi, p, j: (p, bi * nq + j)),
            pl.BlockSpec((None, s, hps * LANES), lambda bi, p, j: (bi, 0, p)),
            pl.BlockSpec((nq, hps * HEAD_DIM, tq), lambda bi, p, j: (bi, p, 0)),
        ],
        out_specs=pl.BlockSpec((tq, hps * HEAD_DIM), lambda bi, p, j: (bi * nq + j, p)),
        out_shape=jax.ShapeDtypeStruct((t, B_QK), BF16),
        scratch_shapes=[
            pltpu.VMEM((hps, 1, tq), F32),
            pltpu.VMEM((hps, SUBLANES, tq), F32),
            pltpu.VMEM((hps, HEAD_DIM, tq), F32),
        ],
        compiler_params=_params("parallel", "parallel", "arbitrary"),
        name="fox_attention",
    )(qt, kaug, vt)


def _mlp_kernel(h_ref, o_ref, wo_ref, g_ref, wup_ref, wdn_ref, gf_ref, out_ref, *, tf, final):
    h1 = h_ref[...] + jnp.dot(o_ref[...], wo_ref[...], preferred_element_type=F32)
    xn = _rms(h1, g_ref[...]).astype(BF16)
    acc = h1
    for c0 in range(0, wup_ref.shape[1], tf):
        u = jnp.dot(xn, wup_ref[:, c0:c0 + tf], preferred_element_type=F32)
        a = jnp.square(jnp.maximum(u, 0.0)).astype(BF16)
        acc = acc + jnp.dot(a, wdn_ref[c0:c0 + tf, :], preferred_element_type=F32)
    if final:
        acc = _rms(acc, gf_ref[...])
    out_ref[...] = acc


def _mlp(h, o, wo, g, wup, wdn, gf, final, tm, tf):
    t, d = h.shape
    row = pl.BlockSpec((tm, d), lambda i: (i, 0))
    return pl.pallas_call(
        functools.partial(_mlp_kernel, tf=tf, final=final),
        grid=(t // tm,),
        in_specs=[row, row, _resident(wo.shape), _resident((1, d)), _resident(wup.shape),
                  _resident(wdn.shape), _resident((1, d))],
        out_specs=row,
        out_shape=jax.ShapeDtypeStruct((t, d), F32),
        compiler_params=_params("parallel"),
        name="oproj_mlp",
    )(h, o, wo, g, wup, wdn, gf)


def _dup_heads(w, heads):
    d = w.shape[0]
    w = w.reshape(d, heads, HEAD_DIM)
    return jnp.concatenate([w, w], axis=-1).reshape(d, heads * LANES)


def _prep_w_in_a(w):
    kv = A_KV_HEADS * HEAD_DIM
    o_k, o_v, o_qi = A_Q, A_Q + kv, A_Q + 2 * kv
    o_ki, o_wi = o_qi + A_QI, o_qi + A_QI + HEAD_DIM
    wi = w[:, o_wi:o_wi + IDX_HEADS]
    return jnp.concatenate([
        w[:, :A_Q],
        _dup_heads(w[:, o_k:o_v], A_KV_HEADS),
        w[:, o_v:o_qi],
        w[:, o_qi:o_ki],
        _dup_heads(w[:, o_ki:o_wi], 1),
        jnp.pad(wi, ((0, 0), (0, LANES - IDX_HEADS))),
    ], axis=1).astype(BF16)


def _rope_tables(seq):
    inv = 1.0 / (ROPE_THETA ** (jnp.arange(0, HEAD_DIM, 2, dtype=F32) / HEAD_DIM))
    ang = jnp.arange(seq, dtype=F32)[:, None] * inv[None, :]
    c, s = jnp.cos(ang), jnp.sin(ang)
    return jnp.concatenate([c, c, c, c], axis=1), jnp.concatenate([-s, s, -s, s], axis=1)


def _bias_placement():
    place = np.zeros((BIAS_PIECES * LANES, B_KAUG), np.float32)
    ones = np.zeros((1, B_KAUG), np.float32)
    for h in range(B_HEADS):
        for p in range(BIAS_PIECES):
            place[p * LANES + h, h * LANES + HEAD_DIM + p] = -1.0
            ones[0, h * LANES + HEAD_DIM + BIAS_PIECES + p] = 1.0
    return jnp.asarray(place, BF16), jnp.asarray(ones, F32)


def _query_bias_rows(chi, cmid, clo):
    b, s, _ = chi.shape
    t = b * s
    pcs = [x[:, :, :B_HEADS].reshape(t, B_HEADS).T for x in (chi, cmid, clo)]
    one = jnp.ones((B_HEADS, t), BF16)
    zero = jnp.zeros((B_HEADS, t), BF16)
    rows = [one] * BIAS_PIECES + pcs + [zero] * (BF16_ROWS - 2 * BIAS_PIECES)
    return jnp.stack(rows, axis=1).reshape(B_HEADS * BF16_ROWS, t)


def kernel(x, g_attn_a, w_in_a, w_o_a, g_kv, w_kv_b, b_f, g_attn_b, w_q_b, w_o_b,
           g_mlp, w_up, w_down, g_final):
    b, s, d = x.shape
    t = b * s
    depth = g_mlp.shape[0]
    n_a = g_attn_a.shape[0]
    tm = min(512, s)
    tq = min(ATTN_BLOCK, s)
    tf = 512

    cos, sin = _rope_tables(s)
    gf = g_final.reshape(1, d)
    h = x.reshape(t, d)
    kaug = vt_b = bias_rows = None

    for layer in range(depth):
        if layer < n_a:
            qt, k, vt, qit, ki, wit = _aproj(h, g_attn_a[layer].reshape(1, d),
                                             _prep_w_in_a(w_in_a[layer]), cos, sin, s, tm, tq)
            o = _dsa(qt, qit, wit, ki, k, vt, b, s, tq)
            w_o = w_o_a[layer]
        else:
            jb = layer - n_a
            if jb == 0:
                wk = jnp.pad(w_kv_b[:, :B_QK].reshape(d, B_HEADS, HEAD_DIM),
                             ((0, 0), (0, 0), (0, LANES - HEAD_DIM))).reshape(d, B_KAUG)
                wf = jnp.pad(w_kv_b[:, 2 * B_QK:], ((0, 0), (0, LANES - B_HEADS)))
                bias = jnp.pad(b_f, (0, LANES - B_HEADS)).reshape(1, LANES)
                place, ones = _bias_placement()
                kaug, vt_b, chi, cmid, clo = _kvproj(
                    h.reshape(b, s, d), g_kv.reshape(1, d), wk.astype(BF16),
                    w_kv_b[:, B_QK:2 * B_QK].astype(BF16), wf.astype(BF16), bias,
                    place, ones, tm, tq)
                bias_rows = _query_bias_rows(chi, cmid, clo)
            qt = _qproj(h, g_attn_b[jb].reshape(1, d), w_q_b[jb].astype(BF16), bias_rows, tm)
            o = _fox(qt, kaug, vt_b, b, s, tq)
            w_o = w_o_b[jb]
        h = _mlp(h, o, w_o.astype(BF16), g_mlp[layer].reshape(1, d),
                 w_up[layer].astype(BF16), w_down[layer].astype(BF16), gf,
                 layer == depth - 1, tm, tf)
    return h.reshape(b, s, d)
```

```python
import functools

import numpy as np
import jax
import jax.numpy as jnp
from jax import lax
from jax.experimental import pallas as pl
from jax.experimental.pallas import tpu as pltpu

LANES = 128
SUBLANES = 8
BF16_ROWS = 16
HEAD_DIM = 64
ROPE_HALF = HEAD_DIM // 2
CHUNK = 64
A_HEADS = 16
A_KV_HEADS = 4
A_GROUP = A_HEADS // A_KV_HEADS
IDX_HEADS = 8
TOPK_MAX = 256
B_HEADS = 16
ROPE_THETA = 10000.0
EPS = 1e-6
INT_MIN = -(2 ** 31)
MASKED = -1e30
LOG2E = 1.4426950408889634
QK_SCALE = HEAD_DIM ** -0.5 * LOG2E
VMEM_LIMIT_BYTES = 56 * 1024 * 1024
ATTN_BLOCK = 256

F32 = jnp.float32
BF16 = jnp.bfloat16
I32 = jnp.int32


def _params(*sem):
    return pltpu.CompilerParams(dimension_semantics=sem,
                                vmem_limit_bytes=VMEM_LIMIT_BYTES)


def _resident(shape):
    nd = len(shape)
    return pl.BlockSpec(shape, lambda *_: (0,) * nd, pipeline_mode=pl.Buffered(1))


def _rms(x, g):
    return x * lax.rsqrt(jnp.mean(x * x, axis=-1, keepdims=True) + EPS) * g


def _colsum8(x):
    n, w = x.shape
    return x.reshape(n // SUBLANES, SUBLANES, w).sum(axis=0)


def _colmax8(x):
    n, w = x.shape
    return x.reshape(n // SUBLANES, SUBLANES, w).max(axis=0)


def _softmax_pv(s_buf, mx_buf, vt, h, m_sc, l_sc, acc_sc):
    m_old = m_sc[h]
    m_new = jnp.maximum(m_old, jnp.max(mx_buf[h], axis=0, keepdims=True))
    alpha = jnp.exp2(m_old - m_new)
    p = jnp.exp2(s_buf[h] - m_new)
    l_sc[h] = alpha * l_sc[h] + _colsum8(p)
    acc_sc[h] = alpha * acc_sc[h] + jnp.dot(vt, p.astype(BF16), preferred_element_type=F32)
    m_sc[h] = m_new


A_Q = A_HEADS * HEAD_DIM
A_KD = A_KV_HEADS * LANES
A_V = A_KV_HEADS * HEAD_DIM
A_QI = IDX_HEADS * HEAD_DIM
A_OFF_K = A_Q
A_OFF_V = A_OFF_K + A_KD
A_OFF_QI = A_OFF_V + A_V
A_OFF_KI = A_OFF_QI + A_QI
A_OFF_WI = A_OFF_KI + LANES
A_COLS = A_OFF_WI + LANES


def _aproj_kernel(h_ref, g_ref, w_ref, cos_ref, sin_ref,
                  qt_ref, k_ref, vt_ref, qit_ref, ki_ref, wit_ref, *, wi_scale):
    tm = h_ref.shape[0]
    tk = vt_ref.shape[2]
    xn = _rms(h_ref[...], g_ref[...]).astype(BF16)
    cos = cos_ref[...]
    sin = sin_ref[...]
    lane = lax.broadcasted_iota(I32, (tm, LANES), 1)
    first_half = (lane & ROPE_HALF) == 0

    def rope(y):
        ys = jnp.where(first_half, pltpu.roll(y, LANES - ROPE_HALF, 1),
                       pltpu.roll(y, ROPE_HALF, 1))
        return y * cos + ys * sin

    def chunks(c0, n):
        step = 4 * LANES
        for s0 in range(0, n, step):
            sn = min(step, n - s0)
            y = jnp.dot(xn, w_ref[:, c0 + s0:c0 + s0 + sn], preferred_element_type=F32)
            for j in range(sn // LANES):
                yield (s0 // LANES + j), y[:, j * LANES:(j + 1) * LANES]

    for j, y in chunks(0, A_Q):
        qt_ref[j * LANES:(j + 1) * LANES, :] = (rope(y) * QK_SCALE).T.astype(BF16)
    for j, y in chunks(A_OFF_K, A_KD):
        k_ref[:, j * LANES:(j + 1) * LANES] = rope(y).astype(BF16)
    for j, y in chunks(A_OFF_V, A_V):
        for sb in range(tm // tk):
            vt_ref[sb, j * LANES:(j + 1) * LANES, :] = y[sb * tk:(sb + 1) * tk].T.astype(BF16)
    for j, y in chunks(A_OFF_QI, A_QI):
        qit_ref[j * LANES:(j + 1) * LANES, :] = rope(y).T.astype(BF16)
    for j, y in chunks(A_OFF_KI, LANES):
        ki_ref[...] = rope(y).astype(BF16)
    for j, y in chunks(A_OFF_WI, LANES):
        wit_ref[...] = (y * wi_scale).T[:IDX_HEADS, :]


def _aproj(h, g, w, cos, sin, seq, tm, tk):
    t, d = h.shape
    npos = seq // tm
    row = lambda n: pl.BlockSpec((tm, n), lambda i: (i, 0))
    col = lambda n: pl.BlockSpec((n, tm), lambda i: (0, i))
    tab = pl.BlockSpec((tm, LANES), lambda i: (i % npos, 0))
    wi_scale = IDX_HEADS ** -0.5 * HEAD_DIM ** -0.5
    return pl.pallas_call(
        functools.partial(_aproj_kernel, wi_scale=wi_scale),
        grid=(t // tm,),
        in_specs=[row(d), _resident((1, d)), _resident((d, A_COLS)), tab, tab],
        out_specs=[col(A_Q), row(A_KD),
                   pl.BlockSpec((tm // tk, A_V, tk), lambda i: (i, 0, 0)),
                   col(A_QI), row(LANES), col(IDX_HEADS)],
        out_shape=[jax.ShapeDtypeStruct((A_Q, t), BF16),
                   jax.ShapeDtypeStruct((t, A_KD), BF16),
                   jax.ShapeDtypeStruct((t // tk, A_V, tk), BF16),
                   jax.ShapeDtypeStruct((A_QI, t), BF16),
                   jax.ShapeDtypeStruct((t, LANES), BF16),
                   jax.ShapeDtypeStruct((IDX_HEADS, t), F32)],
        compiler_params=_params("parallel"),
        name="a_proj",
    )(h, g, w, cos, sin)


def _dsa_kernel(qt_ref, qit_ref, wit_ref, ki_ref, k_ref, vt_ref, o_ref,
                key_sc, qiw_sc, qw_sc, sa_sc, sb_sc, mxa_sc, mxb_sc, m_sc, l_sc, acc_sc,
                *, n_sel, idx_bits):
    tq = qt_ref.shape[1]
    tk = tq
    j = pl.program_id(1)
    nkb = j + 1
    qpos = j * tq + lax.broadcasted_iota(I32, (1, tq), 1)
    limit = (qpos // CHUNK + 1) * CHUNK
    kiota = lax.broadcasted_iota(I32, (tk, tq), 0)
    zeros_half = jnp.zeros((HEAD_DIM, tq), BF16)

    for h in range(IDX_HEADS):
        qiw_sc[h, 0:HEAD_DIM, :] = qit_ref[h * HEAD_DIM:(h + 1) * HEAD_DIM, :]
        qiw_sc[h, HEAD_DIM:LANES, :] = zeros_half
    for h in range(A_HEADS):
        qw_sc[h, 0:HEAD_DIM, :] = qt_ref[h * HEAD_DIM:(h + 1) * HEAD_DIM, :]
        qw_sc[h, HEAD_DIM:LANES, :] = zeros_half

    def score_body(c, carry):
        kc = ki_ref[pl.ds(pl.multiple_of(c * tk, tk), tk), :]
        sc = jnp.zeros((tk, tq), F32)
        for h in range(IDX_HEADS):
            lg = jnp.dot(kc, qiw_sc[h], preferred_element_type=F32)
            sc = sc + jnp.maximum(lg, 0.0) * wit_ref[h:h + 1, :]
        sc = jnp.where(sc == 0.0, 0.0, sc)
        bits = pltpu.bitcast(sc, I32)
        key = bits ^ ((bits >> 31) & 0x7FFFFFFF)
        key_sc[c] = jnp.where(c * tk + kiota < limit, key, INT_MIN)
        return carry

    lax.fori_loop(0, nkb, score_body, 0)

    def count(pred):
        def body(c, acc):
            return acc + _colsum8(jnp.where(pred(c, key_sc[c]), 1.0, 0.0))
        acc = lax.fori_loop(0, nkb, body, jnp.zeros((SUBLANES, tq), F32))
        return jnp.sum(acc, axis=0, keepdims=True)

    def try_threshold(cand, thr, cnt_thr):
        cnt = count(lambda c, key: key >= cand)
        ok = cnt >= n_sel
        return jnp.where(ok, cand, thr), jnp.where(ok, cnt, cnt_thr)

    thr0 = jnp.full((1, tq), INT_MIN, I32)
    thr, cnt_thr = try_threshold(jnp.zeros((1, tq), I32), thr0, jnp.zeros((1, tq), F32))

    def bit_body(i, carry):
        thr, cnt_thr = carry
        cand = thr | jnp.left_shift(jnp.int32(1), 30 - i)
        return try_threshold(cand, thr, cnt_thr)

    thr, cnt_thr = lax.fori_loop(0, 31, bit_body, (thr, cnt_thr))
    thr = jnp.maximum(thr, INT_MIN + 1)

    @pl.when(jnp.max(cnt_thr) > n_sel)
    def _():
        need = n_sel - count(lambda c, key: key > thr)
        pos = jnp.zeros((1, tq), I32)
        for b in range(idx_bits - 1, -1, -1):
            cand = pos | (1 << b)
            cnt = count(lambda c, key: jnp.where(key == thr, c * tk + kiota, 2 ** 30) < cand)
            pos = jnp.where(cnt < need, cand, pos)

        def demote(c, carry):
            key = key_sc[c]
            tie_pos = jnp.where(key == thr, c * tk + kiota, -1)
            key_sc[c] = jnp.where(tie_pos > pos, INT_MIN, key)
            return carry

        lax.fori_loop(0, nkb, demote, 0)

    m_sc[...] = jnp.full(m_sc.shape, MASKED, F32)
    l_sc[...] = jnp.zeros(l_sc.shape, F32)
    acc_sc[...] = jnp.zeros(acc_sc.shape, F32)

    bufs = ((sa_sc, mxa_sc), (sb_sc, mxb_sc))

    def select_bias(c):
        return jnp.where(key_sc[c] >= thr, 0.0, MASKED)

    def scores(c, h, buf, bias):
        n = h // A_GROUP
        r0 = pl.multiple_of(c * tk, tk)
        s = jnp.dot(k_ref[pl.ds(r0, tk), n * LANES:(n + 1) * LANES], qw_sc[h],
                    preferred_element_type=F32) + bias
        buf[0][h] = s
        buf[1][h] = _colmax8(s)

    def update(c, h, buf):
        n = h // A_GROUP
        _softmax_pv(buf[0], buf[1], vt_ref[c, n * HEAD_DIM:(n + 1) * HEAD_DIM, :],
                    h, m_sc, l_sc, acc_sc)

    def by_parity(c, fn):
        for par in range(2):
            @pl.when(c % 2 == par)
            def _():
                fn(bufs[par], bufs[1 - par])

    bias0 = select_bias(0)
    for h in range(A_HEADS):
        scores(0, h, bufs[0], bias0)

    def attn_body(c, carry):
        def fn(cur, nxt):
            bias = select_bias(c + 1)
            for h in range(A_HEADS):
                scores(c + 1, h, nxt, bias)
                update(c, h, cur)
        by_parity(c, fn)
        return carry

    lax.fori_loop(0, j, attn_body, 0)
    by_parity(j, lambda cur, nxt: [update(j, h, cur) for h in range(A_HEADS)])

    for i in range(A_HEADS // 2):
        pair = [acc_sc[h] / jnp.sum(l_sc[h], axis=0, keepdims=True) for h in (2 * i, 2 * i + 1)]
        o_ref[:, i * LANES:(i + 1) * LANES] = jnp.concatenate(pair, axis=0).T.astype(o_ref.dtype)


def _dsa(qt, qit, wit, ki, k, vt, b, s, tq):
    t = b * s
    nq = s // tq
    n_sel = min(TOPK_MAX, s // 4)
    idx_bits = max(1, (s - 1).bit_length())
    qcol = lambda n: pl.BlockSpec((n, tq), lambda bi, j: (0, bi * nq + j))
    full = lambda n: pl.BlockSpec((s, n), lambda bi, j: (bi, 0))
    return pl.pallas_call(
        functools.partial(_dsa_kernel, n_sel=float(n_sel), idx_bits=idx_bits),
        grid=(b, nq),
        in_specs=[qcol(A_Q), qcol(A_QI), qcol(IDX_HEADS), full(LANES), full(A_KD),
                  pl.BlockSpec((nq, A_V, tq), lambda bi, j: (bi, 0, 0))],
        out_specs=pl.BlockSpec((tq, A_Q), lambda bi, j: (bi * nq + j, 0)),
        out_shape=jax.ShapeDtypeStruct((t, A_Q), BF16),
        scratch_shapes=[
            pltpu.VMEM((nq, tq, tq), I32),
            pltpu.VMEM((IDX_HEADS, LANES, tq), BF16),
            pltpu.VMEM((A_HEADS, LANES, tq), BF16),
            pltpu.VMEM((A_HEADS, tq, tq), F32),
            pltpu.VMEM((A_HEADS, tq, tq), F32),
            pltpu.VMEM((A_HEADS, SUBLANES, tq), F32),
            pltpu.VMEM((A_HEADS, SUBLANES, tq), F32),
            pltpu.VMEM((A_HEADS, 1, tq), F32),
            pltpu.VMEM((A_HEADS, SUBLANES, tq), F32),
            pltpu.VMEM((A_HEADS, HEAD_DIM, tq), F32),
        ],
        compiler_params=_params("parallel", "arbitrary"),
        name="dsa_attention",
    )(qt, qit, wit, ki, k, vt)


B_QK = B_HEADS * HEAD_DIM
B_KAUG = B_HEADS * LANES
BIAS_PIECES = 3


def _split3(x):
    hi = x.astype(BF16)
    r1 = x - hi.astype(F32)
    mid = r1.astype(BF16)
    low = (r1 - mid.astype(F32)).astype(BF16)
    return hi, mid, low


def _kvproj_kernel(h_ref, g_ref, wk_ref, wv_ref, wf_ref, bf_ref, place_ref, ones_ref,
                   kaug_ref, vt_ref, chi_ref, cmid_ref, clo_ref, carry_sc):
    tm = h_ref.shape[0]
    tk = vt_ref.shape[2]

    @pl.when(pl.program_id(1) == 0)
    def _():
        carry_sc[...] = jnp.zeros(carry_sc.shape, F32)

    xn = _rms(h_ref[...], g_ref[...]).astype(BF16)
    z = jnp.dot(xn, wf_ref[...], preferred_element_type=F32) + bf_ref[...]
    logf = jnp.minimum(z, 0.0) - jnp.log1p(jnp.exp(-jnp.abs(z)))
    tri = (lax.broadcasted_iota(I32, (tm, tm), 0) >=
           lax.broadcasted_iota(I32, (tm, tm), 1)).astype(BF16)
    cs = sum(jnp.dot(tri, piece, preferred_element_type=F32) for piece in _split3(logf))
    cum = cs + carry_sc[...]
    carry_sc[...] = cum[tm - 1:tm, :]
    hi, mid, low = _split3(cum * LOG2E)
    chi_ref[...] = hi
    cmid_ref[...] = mid
    clo_ref[...] = low
    pieces = jnp.concatenate([hi, mid, low], axis=1)

    step = 4 * LANES
    for c0 in range(0, B_KAUG, step):
        y = (jnp.dot(xn, wk_ref[:, c0:c0 + step], preferred_element_type=F32) +
             jnp.dot(pieces, place_ref[:, c0:c0 + step], preferred_element_type=F32) +
             ones_ref[:, c0:c0 + step])
        kaug_ref[:, c0:c0 + step] = y.astype(BF16)
    for c0 in range(0, B_QK, step):
        y = jnp.dot(xn, wv_ref[:, c0:c0 + step], preferred_element_type=F32)
        for jj in range(step // LANES):
            r = c0 + jj * LANES
            for sb in range(tm // tk):
                vt_ref[sb, r:r + LANES, :] = (
                    y[sb * tk:(sb + 1) * tk, jj * LANES:(jj + 1) * LANES].T.astype(BF16))


def _kvproj(h3, g, wk, wv, wf, bf, place, ones, tm, tk):
    b, s, d = h3.shape
    ns = s // tm
    blk = lambda n: pl.BlockSpec((None, tm, n), lambda bi, i: (bi, i, 0))
    return pl.pallas_call(
        _kvproj_kernel,
        grid=(b, ns),
        in_specs=[blk(d), _resident((1, d)), _resident(wk.shape), _resident(wv.shape),
                  _resident(wf.shape), _resident((1, LANES)), _resident(place.shape),
                  _resident(ones.shape)],
        out_specs=[blk(B_KAUG),
                   pl.BlockSpec((tm // tk, B_QK, tk), lambda bi, i: (bi * ns + i, 0, 0)),
                   blk(LANES), blk(LANES), blk(LANES)],
        out_shape=[jax.ShapeDtypeStruct((b, s, B_KAUG), BF16),
                   jax.ShapeDtypeStruct((b * s // tk, B_QK, tk), BF16),
                   jax.ShapeDtypeStruct((b, s, LANES), BF16),
                   jax.ShapeDtypeStruct((b, s, LANES), BF16),
                   jax.ShapeDtypeStruct((b, s, LANES), BF16)],
        scratch_shapes=[pltpu.VMEM((1, LANES), F32)],
        compiler_params=_params("parallel", "arbitrary"),
        name="kv_proj_cumsum",
    )(h3, g, wk, wv, wf, bf, place, ones)


def _qproj_kernel(h_ref, g_ref, w_ref, rows_ref, qt_ref):
    tm = h_ref.shape[0]
    xn = _rms(h_ref[...], g_ref[...]).astype(BF16)
    pad = jnp.zeros((LANES - HEAD_DIM - BF16_ROWS, tm), BF16)
    step = 4 * LANES
    for c0 in range(0, B_QK, step):
        y = jnp.dot(xn, w_ref[:, c0:c0 + step], preferred_element_type=F32) * QK_SCALE
        for jj in range(step // LANES):
            yt = y[:, jj * LANES:(jj + 1) * LANES].T.astype(BF16)
            for e in range(2):
                h = (c0 + jj * LANES) // HEAD_DIM + e
                r = h * LANES
                qt_ref[r:r + HEAD_DIM, :] = yt[e * HEAD_DIM:(e + 1) * HEAD_DIM, :]
                qt_ref[r + HEAD_DIM:r + HEAD_DIM + BF16_ROWS, :] = (
                    rows_ref[h * BF16_ROWS:(h + 1) * BF16_ROWS, :])
                qt_ref[r + HEAD_DIM + BF16_ROWS:r + LANES, :] = pad


def _qproj(h, g, w, rows, tm):
    t, d = h.shape
    return pl.pallas_call(
        _qproj_kernel,
        grid=(t // tm,),
        in_specs=[pl.BlockSpec((tm, d), lambda i: (i, 0)), _resident((1, d)), _resident(w.shape),
                  pl.BlockSpec((B_HEADS * BF16_ROWS, tm), lambda i: (0, i))],
        out_specs=pl.BlockSpec((B_KAUG, tm), lambda i: (0, i)),
        out_shape=jax.ShapeDtypeStruct((B_KAUG, t), BF16),
        compiler_params=_params("parallel"),
        name="q_proj",
    )(h, g, w, rows)


FOX_HEADS_PER_STEP = 4


def _fox_kernel(qt_ref, k_ref, vt_ref, o_ref, sa_sc, sb_sc, mxa_sc, mxb_sc, m_sc, l_sc, acc_sc):
    tq = qt_ref.shape[1]
    tk = tq
    hps = FOX_HEADS_PER_STEP
    j = pl.program_id(2)
    m_sc[...] = jnp.full(m_sc.shape, MASKED, F32)
    l_sc[...] = jnp.zeros(l_sc.shape, F32)
    acc_sc[...] = jnp.zeros(acc_sc.shape, F32)
    kiota = lax.broadcasted_iota(I32, (tk, tq), 0)
    qiota = lax.broadcasted_iota(I32, (1, tq), 1)
    bufs = ((sa_sc, mxa_sc), (sb_sc, mxb_sc))

    def scores(c, h, buf, qlimit):
        r0 = pl.multiple_of(c * tk, tk)
        s = jnp.dot(k_ref[pl.ds(r0, tk), h * LANES:(h + 1) * LANES],
                    qt_ref[h * LANES:(h + 1) * LANES, :], preferred_element_type=F32)
        if qlimit is not None:
            s = jnp.where(kiota <= qlimit, s, MASKED)
        buf[0][h] = s
        buf[1][h] = _colmax8(s)

    def update(c, h, buf):
        _softmax_pv(buf[0], buf[1], vt_ref[c, h * HEAD_DIM:(h + 1) * HEAD_DIM, :],
                    h, m_sc, l_sc, acc_sc)

    def by_parity(c, fn):
        for par in range(2):
            @pl.when(c % 2 == par)
            def _():
                fn(bufs[par], bufs[1 - par])

    first_limit = qiota + jnp.where(j == 0, 0, tk)
    for h in range(hps):
        scores(0, h, bufs[0], first_limit)

    def stage(c, next_limit):
        def fn(cur, nxt):
            for h in range(hps):
                scores(c + 1, h, nxt, next_limit)
                update(c, h, cur)
        by_parity(c, fn)

    def main_body(c, carry):
        stage(c, None)
        return carry

    lax.fori_loop(0, j - 1, main_body, 0)

    @pl.when(j >= 1)
    def _():
        stage(j - 1, qiota)

    by_parity(j, lambda cur, nxt: [update(j, h, cur) for h in range(hps)])
    outs = [acc_sc[h] / jnp.sum(l_sc[h], axis=0, keepdims=True) for h in range(hps)]
    o_ref[...] = jnp.concatenate(outs, axis=0).T.astype(o_ref.dtype)


def _fox(qt, kaug, vt, b, s, tq):
    t = b * s
    nq = s // tq
    hps = FOX_HEADS_PER_STEP
    return pl.pallas_call(
        _fox_kernel,
        grid=(b, B_HEADS // hps, nq),
        in_specs=[
            pl.BlockSpec((hps * LANES, tq), lambda bi, p, j: (p, bi * nq + j)),
            pl.BlockSpec((None, s, hps * LANES), lambda bi, p, j: (bi, 0, p)),
            pl.BlockSpec((nq, hps * HEAD_DIM, tq), lambda bi, p, j: (bi, p, 0)),
        ],
        out_specs=pl.BlockSpec((tq, hps * HEAD_DIM), lambda bi, p, j: (bi * nq + j, p)),
        out_shape=jax.ShapeDtypeStruct((t, B_QK), BF16),
        scratch_shapes=[
            pltpu.VMEM((hps, tq, tq), F32),
            pltpu.VMEM((hps, tq, tq), F32),
            pltpu.VMEM((hps, SUBLANES, tq), F32),
            pltpu.VMEM((hps, SUBLANES, tq), F32),
            pltpu.VMEM((hps, 1, tq), F32),
            pltpu.VMEM((hps, SUBLANES, tq), F32),
            pltpu.VMEM((hps, HEAD_DIM, tq), F32),
        ],
        compiler_params=_params("parallel", "parallel", "arbitrary"),
        name="fox_attention",
    )(qt, kaug, vt)


def _mlp_kernel(h_ref, o_ref, wo_ref, g_ref, wup_ref, wdn_ref, gf_ref, out_ref, *, tf, final):
    h1 = h_ref[...] + jnp.dot(o_ref[...], wo_ref[...], preferred_element_type=F32)
    xn = _rms(h1, g_ref[...]).astype(BF16)
    acc = h1
    for c0 in range(0, wup_ref.shape[1], tf):
        u = jnp.dot(xn, wup_ref[:, c0:c0 + tf], preferred_element_type=F32)
        a = jnp.square(jnp.maximum(u, 0.0)).astype(BF16)
        acc = acc + jnp.dot(a, wdn_ref[c0:c0 + tf, :], preferred_element_type=F32)
    if final:
        acc = _rms(acc, gf_ref[...])
    out_ref[...] = acc


def _mlp(h, o, wo, g, wup, wdn, gf, final, tm, tf):
    t, d = h.shape
    row = pl.BlockSpec((tm, d), lambda i: (i, 0))
    return pl.pallas_call(
        functools.partial(_mlp_kernel, tf=tf, final=final),
        grid=(t // tm,),
        in_specs=[row, row, _resident(wo.shape), _resident((1, d)), _resident(wup.shape),
                  _resident(wdn.shape), _resident((1, d))],
        out_specs=row,
        out_shape=jax.ShapeDtypeStruct((t, d), F32),
        compiler_params=_params("parallel"),
        name="oproj_mlp",
    )(h, o, wo, g, wup, wdn, gf)


def _dup_heads(w, heads):
    d = w.shape[0]
    w = w.reshape(d, heads, HEAD_DIM)
    return jnp.concatenate([w, w], axis=-1).reshape(d, heads * LANES)


def _prep_w_in_a(w):
    kv = A_KV_HEADS * HEAD_DIM
    o_k, o_v, o_qi = A_Q, A_Q + kv, A_Q + 2 * kv
    o_ki, o_wi = o_qi + A_QI, o_qi + A_QI + HEAD_DIM
    wi = w[:, o_wi:o_wi + IDX_HEADS]
    return jnp.concatenate([
        w[:, :A_Q],
        _dup_heads(w[:, o_k:o_v], A_KV_HEADS),
        w[:, o_v:o_qi],
        w[:, o_qi:o_ki],
        _dup_heads(w[:, o_ki:o_wi], 1),
        jnp.pad(wi, ((0, 0), (0, LANES - IDX_HEADS))),
    ], axis=1).astype(BF16)


def _rope_tables(seq):
    inv = 1.0 / (ROPE_THETA ** (jnp.arange(0, HEAD_DIM, 2, dtype=F32) / HEAD_DIM))
    ang = jnp.arange(seq, dtype=F32)[:, None] * inv[None, :]
    c, s = jnp.cos(ang), jnp.sin(ang)
    return jnp.concatenate([c, c, c, c], axis=1), jnp.concatenate([-s, s, -s, s], axis=1)


def _bias_placement():
    place = np.zeros((BIAS_PIECES * LANES, B_KAUG), np.float32)
    ones = np.zeros((1, B_KAUG), np.float32)
    for h in range(B_HEADS):
        for p in range(BIAS_PIECES):
            place[p * LANES + h, h * LANES + HEAD_DIM + p] = -1.0
            ones[0, h * LANES + HEAD_DIM + BIAS_PIECES + p] = 1.0
    return jnp.asarray(place, BF16), jnp.asarray(ones, F32)


def _query_bias_rows(chi, cmid, clo):
    b, s, _ = chi.shape
    t = b * s
    pcs = [x[:, :, :B_HEADS].reshape(t, B_HEADS).T for x in (chi, cmid, clo)]
    one = jnp.ones((B_HEADS, t), BF16)
    zero = jnp.zeros((B_HEADS, t), BF16)
    rows = [one] * BIAS_PIECES + pcs + [zero] * (BF16_ROWS - 2 * BIAS_PIECES)
    return jnp.stack(rows, axis=1).reshape(B_HEADS * BF16_ROWS, t)


def kernel(x, g_attn_a, w_in_a, w_o_a, g_kv, w_kv_b, b_f, g_attn_b, w_q_b, w_o_b,
           g_mlp, w_up, w_down, g_final):
    b, s, d = x.shape
    t = b * s
    depth = g_mlp.shape[0]
    n_a = g_attn_a.shape[0]
    tm = min(512, s)
    tq = min(ATTN_BLOCK, s)
    tf = 512

    cos, sin = _rope_tables(s)
    gf = g_final.reshape(1, d)
    h = x.reshape(t, d)
    kaug = vt_b = bias_rows = None

    for layer in range(depth):
        if layer < n_a:
            qt, k, vt, qit, ki, wit = _aproj(h, g_attn_a[layer].reshape(1, d),
                                             _prep_w_in_a(w_in_a[layer]), cos, sin, s, tm, tq)
            o = _dsa(qt, qit, wit, ki, k, vt, b, s, tq)
            w_o = w_o_a[layer]
        else:
            jb = layer - n_a
            if jb == 0:
                wk = jnp.pad(w_kv_b[:, :B_QK].reshape(d, B_HEADS, HEAD_DIM),
                             ((0, 0), (0, 0), (0, LANES - HEAD_DIM))).reshape(d, B_KAUG)
                wf = jnp.pad(w_kv_b[:, 2 * B_QK:], ((0, 0), (0, LANES - B_HEADS)))
                bias = jnp.pad(b_f, (0, LANES - B_HEADS)).reshape(1, LANES)
                place, ones = _bias_placement()
                kaug, vt_b, chi, cmid, clo = _kvproj(
                    h.reshape(b, s, d), g_kv.reshape(1, d), wk.astype(BF16),
                    w_kv_b[:, B_QK:2 * B_QK].astype(BF16), wf.astype(BF16), bias,
                    place, ones, tm, tq)
                bias_rows = _query_bias_rows(chi, cmid, clo)
            qt = _qproj(h, g_attn_b[jb].reshape(1, d), w_q_b[jb].astype(BF16), bias_rows, tm)
            o = _fox(qt, kaug, vt_b, b, s, tq)
            w_o = w_o_b[jb]
        h = _mlp(h, o, w_o.astype(BF16), g_mlp[layer].reshape(1, d),
                 w_up[layer].astype(BF16), w_down[layer].astype(BF16), gf,
                 layer == depth - 1, tm, tf)
    return h.reshape(b, s, d)
```

```python
import functools

import numpy as np
import jax
import jax.numpy as jnp
from jax import lax
from jax.experimental import pallas as pl
from jax.experimental.pallas import tpu as pltpu

LANES = 128
SUBLANES = 8
BF16_ROWS = 16
HEAD_DIM = 64
ROPE_HALF = HEAD_DIM // 2
CHUNK = 64
A_HEADS = 16
A_KV_HEADS = 4
A_GROUP = A_HEADS // A_KV_HEADS
IDX_HEADS = 8
TOPK_MAX = 256
B_HEADS = 16
ROPE_THETA = 10000.0
EPS = 1e-6
INT_MIN = -(2 ** 31)
MASKED = -1e30
LOG2E = 1.4426950408889634
QK_SCALE = HEAD_DIM ** -0.5 * LOG2E
VMEM_LIMIT_BYTES = 56 * 1024 * 1024
ATTN_BLOCK = 256

F32 = jnp.float32
BF16 = jnp.bfloat16
I32 = jnp.int32


def _params(*sem):
    return pltpu.CompilerParams(dimension_semantics=sem,
                                vmem_limit_bytes=VMEM_LIMIT_BYTES)


def _resident(shape):
    nd = len(shape)
    return pl.BlockSpec(shape, lambda *_: (0,) * nd, pipeline_mode=pl.Buffered(1))


def _rms(x, g):
    return x * lax.rsqrt(jnp.mean(x * x, axis=-1, keepdims=True) + EPS) * g


def _colsum8(x):
    n, w = x.shape
    return x.reshape(n // SUBLANES, SUBLANES, w).sum(axis=0)


def _colmax8(x):
    n, w = x.shape
    return x.reshape(n // SUBLANES, SUBLANES, w).max(axis=0)


V_ROWS = HEAD_DIM + BF16_ROWS


def _store_value_head(vt_ref, sb, head, yt):
    r = head * V_ROWS
    tk = yt.shape[1]
    vt_ref[sb, r:r + HEAD_DIM, :] = yt
    first_row = lax.broadcasted_iota(I32, (BF16_ROWS, tk), 0) == 0
    vt_ref[sb, r + HEAD_DIM:r + V_ROWS, :] = jnp.where(first_row, 1.0, 0.0).astype(BF16)


def _softmax_pv(s_buf, mx_buf, vt, h, m_sc, acc_sc):
    m_old = m_sc[h]
    m_new = jnp.maximum(m_old, jnp.max(mx_buf[h], axis=0, keepdims=True))
    alpha = jnp.exp2(m_old - m_new)
    p = jnp.exp2(s_buf[h] - m_new)
    acc_sc[h] = alpha * acc_sc[h] + jnp.dot(vt, p.astype(BF16), preferred_element_type=F32)
    m_sc[h] = m_new


def _normalised(acc_sc, h):
    acc = acc_sc[h]
    return acc[0:HEAD_DIM, :] / acc[HEAD_DIM:HEAD_DIM + 1, :]


A_Q = A_HEADS * HEAD_DIM
A_KD = A_KV_HEADS * LANES
A_V = A_KV_HEADS * HEAD_DIM
A_VT = A_KV_HEADS * V_ROWS
A_QI = IDX_HEADS * HEAD_DIM
A_OFF_K = A_Q
A_OFF_V = A_OFF_K + A_KD
A_OFF_QI = A_OFF_V + A_V
A_OFF_KI = A_OFF_QI + A_QI
A_OFF_WI = A_OFF_KI + LANES
A_COLS = A_OFF_WI + LANES


def _aproj_kernel(h_ref, g_ref, w_ref, cos_ref, sin_ref,
                  qt_ref, k_ref, vt_ref, qit_ref, ki_ref, wit_ref, *, wi_scale):
    tm = h_ref.shape[0]
    tk = vt_ref.shape[2]
    xn = _rms(h_ref[...], g_ref[...]).astype(BF16)
    cos = cos_ref[...]
    sin = sin_ref[...]
    lane = lax.broadcasted_iota(I32, (tm, LANES), 1)
    first_half = (lane & ROPE_HALF) == 0

    def rope(y):
        ys = jnp.where(first_half, pltpu.roll(y, LANES - ROPE_HALF, 1),
                       pltpu.roll(y, ROPE_HALF, 1))
        return y * cos + ys * sin

    def chunks(c0, n):
        step = 4 * LANES
        for s0 in range(0, n, step):
            sn = min(step, n - s0)
            y = jnp.dot(xn, w_ref[:, c0 + s0:c0 + s0 + sn], preferred_element_type=F32)
            for j in range(sn // LANES):
                yield (s0 // LANES + j), y[:, j * LANES:(j + 1) * LANES]

    for j, y in chunks(0, A_Q):
        qt_ref[j * LANES:(j + 1) * LANES, :] = (rope(y) * QK_SCALE).T.astype(BF16)
    for j, y in chunks(A_OFF_K, A_KD):
        k_ref[:, j * LANES:(j + 1) * LANES] = rope(y).astype(BF16)
    for j, y in chunks(A_OFF_V, A_V):
        for sb in range(tm // tk):
            yt = y[sb * tk:(sb + 1) * tk].T.astype(BF16)
            for e in range(2):
                _store_value_head(vt_ref, sb, 2 * j + e, yt[e * HEAD_DIM:(e + 1) * HEAD_DIM, :])
    for j, y in chunks(A_OFF_QI, A_QI):
        qit_ref[j * LANES:(j + 1) * LANES, :] = rope(y).T.astype(BF16)
    for j, y in chunks(A_OFF_KI, LANES):
        ki_ref[...] = rope(y).astype(BF16)
    for j, y in chunks(A_OFF_WI, LANES):
        wit_ref[...] = (y * wi_scale).T[:IDX_HEADS, :]


def _aproj(h, g, w, cos, sin, seq, tm, tk):
    t, d = h.shape
    npos = seq // tm
    row = lambda n: pl.BlockSpec((tm, n), lambda i: (i, 0))
    col = lambda n: pl.BlockSpec((n, tm), lambda i: (0, i))
    tab = pl.BlockSpec((tm, LANES), lambda i: (i % npos, 0))
    wi_scale = IDX_HEADS ** -0.5 * HEAD_DIM ** -0.5
    return pl.pallas_call(
        functools.partial(_aproj_kernel, wi_scale=wi_scale),
        grid=(t // tm,),
        in_specs=[row(d), _resident((1, d)), _resident((d, A_COLS)), tab, tab],
        out_specs=[col(A_Q), row(A_KD),
                   pl.BlockSpec((tm // tk, A_VT, tk), lambda i: (i, 0, 0)),
                   col(A_QI), row(LANES), col(IDX_HEADS)],
        out_shape=[jax.ShapeDtypeStruct((A_Q, t), BF16),
                   jax.ShapeDtypeStruct((t, A_KD), BF16),
                   jax.ShapeDtypeStruct((t // tk, A_VT, tk), BF16),
                   jax.ShapeDtypeStruct((A_QI, t), BF16),
                   jax.ShapeDtypeStruct((t, LANES), BF16),
                   jax.ShapeDtypeStruct((IDX_HEADS, t), F32)],
        compiler_params=_params("parallel"),
        name="a_proj",
    )(h, g, w, cos, sin)


def _dsa_kernel(qt_ref, qit_ref, wit_ref, ki_ref, k_ref, vt_ref, o_ref,
                key_sc, qiw_sc, qw_sc, sa_sc, sb_sc, mxa_sc, mxb_sc, m_sc, acc_sc,
                *, n_sel, idx_bits):
    tq = qt_ref.shape[1]
    tk = tq
    j = pl.program_id(1)
    nkb = j + 1
    qpos = j * tq + lax.broadcasted_iota(I32, (1, tq), 1)
    limit = (qpos // CHUNK + 1) * CHUNK
    kiota = lax.broadcasted_iota(I32, (tk, tq), 0)
    zeros_half = jnp.zeros((HEAD_DIM, tq), BF16)

    for h in range(IDX_HEADS):
        qiw_sc[h, 0:HEAD_DIM, :] = qit_ref[h * HEAD_DIM:(h + 1) * HEAD_DIM, :]
        qiw_sc[h, HEAD_DIM:LANES, :] = zeros_half
    for h in range(A_HEADS):
        qw_sc[h, 0:HEAD_DIM, :] = qt_ref[h * HEAD_DIM:(h + 1) * HEAD_DIM, :]
        qw_sc[h, HEAD_DIM:LANES, :] = zeros_half

    def score_body(c, carry):
        kc = ki_ref[pl.ds(pl.multiple_of(c * tk, tk), tk), :]
        sc = jnp.zeros((tk, tq), F32)
        for h in range(IDX_HEADS):
            lg = jnp.dot(kc, qiw_sc[h], preferred_element_type=F32)
            sc = sc + jnp.maximum(lg, 0.0) * wit_ref[h:h + 1, :]
        sc = jnp.where(sc == 0.0, 0.0, sc)
        bits = pltpu.bitcast(sc, I32)
        key = bits ^ ((bits >> 31) & 0x7FFFFFFF)
        key_sc[c] = jnp.where(c * tk + kiota < limit, key, INT_MIN)
        return carry

    lax.fori_loop(0, nkb, score_body, 0)

    @pl.when(nkb % 2 == 1)
    def _():
        key_sc[nkb] = jnp.full((tk, tq), INT_MIN, I32)

    def count(pred):
        def body(cc, acc):
            for u in range(2):
                c = 2 * cc + u
                acc = acc + _colsum8(jnp.where(pred(c, key_sc[c]), 1.0, 0.0))
            return acc
        acc = lax.fori_loop(0, (nkb + 1) // 2, body, jnp.zeros((SUBLANES, tq), F32))
        return jnp.sum(acc, axis=0, keepdims=True)

    def try_threshold(cand, thr, cnt_thr):
        cnt = count(lambda c, key: key >= cand)
        ok = cnt >= n_sel
        return jnp.where(ok, cand, thr), jnp.where(ok, cnt, cnt_thr)

    thr0 = jnp.full((1, tq), INT_MIN, I32)
    thr, cnt_thr = try_threshold(jnp.zeros((1, tq), I32), thr0, jnp.zeros((1, tq), F32))

    def bit_body(i, carry):
        thr, cnt_thr = carry
        cand = thr | jnp.left_shift(jnp.int32(1), 30 - i)
        return try_threshold(cand, thr, cnt_thr)

    thr, cnt_thr = lax.fori_loop(0, 31, bit_body, (thr, cnt_thr))
    thr = jnp.maximum(thr, INT_MIN + 1)

    @pl.when(jnp.max(cnt_thr) > n_sel)
    def _():
        need = n_sel - count(lambda c, key: key > thr)
        pos = jnp.zeros((1, tq), I32)
        for b in range(idx_bits - 1, -1, -1):
            cand = pos | (1 << b)
            cnt = count(lambda c, key: jnp.where(key == thr, c * tk + kiota, 2 ** 30) < cand)
            pos = jnp.where(cnt < need, cand, pos)

        def demote(c, carry):
            key = key_sc[c]
            tie_pos = jnp.where(key == thr, c * tk + kiota, -1)
            key_sc[c] = jnp.where(tie_pos > pos, INT_MIN, key)
            return carry

        lax.fori_loop(0, nkb, demote, 0)

    m_sc[...] = jnp.full(m_sc.shape, MASKED, F32)
    acc_sc[...] = jnp.zeros(acc_sc.shape, F32)

    bufs = ((sa_sc, mxa_sc), (sb_sc, mxb_sc))

    def select_bias(c):
        return jnp.where(key_sc[c] >= thr, 0.0, MASKED)

    def scores(c, h, buf, bias):
        n = h // A_GROUP
        r0 = pl.multiple_of(c * tk, tk)
        s = jnp.dot(k_ref[pl.ds(r0, tk), n * LANES:(n + 1) * LANES], qw_sc[h],
                    preferred_element_type=F32) + bias
        buf[0][h] = s
        buf[1][h] = _colmax8(s)

    def update(c, h, buf):
        n = h // A_GROUP
        _softmax_pv(buf[0], buf[1], vt_ref[c, n * V_ROWS:(n + 1) * V_ROWS, :], h, m_sc, acc_sc)

    def by_parity(c, fn):
        for par in range(2):
            @pl.when(c % 2 == par)
            def _():
                fn(bufs[par], bufs[1 - par])

    bias0 = select_bias(0)
    for h in range(A_HEADS):
        scores(0, h, bufs[0], bias0)

    def attn_body(c, carry):
        def fn(cur, nxt):
            bias = select_bias(c + 1)
            for h in range(A_HEADS):
                scores(c + 1, h, nxt, bias)
                update(c, h, cur)
        by_parity(c, fn)
        return carry

    lax.fori_loop(0, j, attn_body, 0)
    by_parity(j, lambda cur, nxt: [update(j, h, cur) for h in range(A_HEADS)])

    for i in range(A_HEADS // 2):
        pair = [_normalised(acc_sc, h) for h in (2 * i, 2 * i + 1)]
        o_ref[:, i * LANES:(i + 1) * LANES] = jnp.concatenate(pair, axis=0).T.astype(o_ref.dtype)


def _dsa(qt, qit, wit, ki, k, vt, b, s, tq):
    t = b * s
    nq = s // tq
    n_sel = min(TOPK_MAX, s // 4)
    idx_bits = max(1, (s - 1).bit_length())
    qcol = lambda n: pl.BlockSpec((n, tq), lambda bi, j: (0, bi * nq + j))
    full = lambda n: pl.BlockSpec((s, n), lambda bi, j: (bi, 0))
    return pl.pallas_call(
        functools.partial(_dsa_kernel, n_sel=float(n_sel), idx_bits=idx_bits),
        grid=(b, nq),
        in_specs=[qcol(A_Q), qcol(A_QI), qcol(IDX_HEADS), full(LANES), full(A_KD),
                  pl.BlockSpec((nq, A_VT, tq), lambda bi, j: (bi, 0, 0))],
        out_specs=pl.BlockSpec((tq, A_Q), lambda bi, j: (bi * nq + j, 0)),
        out_shape=jax.ShapeDtypeStruct((t, A_Q), BF16),
        scratch_shapes=[
            pltpu.VMEM((nq + nq % 2, tq, tq), I32),
            pltpu.VMEM((IDX_HEADS, LANES, tq), BF16),
            pltpu.VMEM((A_HEADS, LANES, tq), BF16),
            pltpu.VMEM((A_HEADS, tq, tq), F32),
            pltpu.VMEM((A_HEADS, tq, tq), F32),
            pltpu.VMEM((A_HEADS, SUBLANES, tq), F32),
            pltpu.VMEM((A_HEADS, SUBLANES, tq), F32),
            pltpu.VMEM((A_HEADS, 1, tq), F32),
            pltpu.VMEM((A_HEADS, V_ROWS, tq), F32),
        ],
        compiler_params=_params("parallel", "arbitrary"),
        name="dsa_attention",
    )(qt, qit, wit, ki, k, vt)


B_QK = B_HEADS * HEAD_DIM
B_KAUG = B_HEADS * LANES
B_VT = B_HEADS * V_ROWS
BIAS_PIECES = 3


def _split3(x):
    hi = x.astype(BF16)
    r1 = x - hi.astype(F32)
    mid = r1.astype(BF16)
    low = (r1 - mid.astype(F32)).astype(BF16)
    return hi, mid, low


def _kvproj_kernel(h_ref, g_ref, wk_ref, wv_ref, wf_ref, bf_ref, place_ref, ones_ref,
                   kaug_ref, vt_ref, chi_ref, cmid_ref, clo_ref, carry_sc):
    tm = h_ref.shape[0]
    tk = vt_ref.shape[2]

    @pl.when(pl.program_id(1) == 0)
    def _():
        carry_sc[...] = jnp.zeros(carry_sc.shape, F32)

    xn = _rms(h_ref[...], g_ref[...]).astype(BF16)
    z = jnp.dot(xn, wf_ref[...], preferred_element_type=F32) + bf_ref[...]
    logf = jnp.minimum(z, 0.0) - jnp.log1p(jnp.exp(-jnp.abs(z)))
    tri = (lax.broadcasted_iota(I32, (tm, tm), 0) >=
           lax.broadcasted_iota(I32, (tm, tm), 1)).astype(BF16)
    cs = sum(jnp.dot(tri, piece, preferred_element_type=F32) for piece in _split3(logf))
    cum = cs + carry_sc[...]
    carry_sc[...] = cum[tm - 1:tm, :]
    hi, mid, low = _split3(cum * LOG2E)
    chi_ref[...] = hi
    cmid_ref[...] = mid
    clo_ref[...] = low
    pieces = jnp.concatenate([hi, mid, low], axis=1)

    step = 4 * LANES
    for c0 in range(0, B_KAUG, step):
        y = (jnp.dot(xn, wk_ref[:, c0:c0 + step], preferred_element_type=F32) +
             jnp.dot(pieces, place_ref[:, c0:c0 + step], preferred_element_type=F32) +
             ones_ref[:, c0:c0 + step])
        kaug_ref[:, c0:c0 + step] = y.astype(BF16)
    for c0 in range(0, B_QK, step):
        y = jnp.dot(xn, wv_ref[:, c0:c0 + step], preferred_element_type=F32)
        for jj in range(step // LANES):
            head = (c0 + jj * LANES) // HEAD_DIM
            for sb in range(tm // tk):
                yt = y[sb * tk:(sb + 1) * tk, jj * LANES:(jj + 1) * LANES].T.astype(BF16)
                for e in range(2):
                    _store_value_head(vt_ref, sb, head + e, yt[e * HEAD_DIM:(e + 1) * HEAD_DIM, :])


def _kvproj(h3, g, wk, wv, wf, bf, place, ones, tm, tk):
    b, s, d = h3.shape
    ns = s // tm
    blk = lambda n: pl.BlockSpec((None, tm, n), lambda bi, i: (bi, i, 0))
    return pl.pallas_call(
        _kvproj_kernel,
        grid=(b, ns),
        in_specs=[blk(d), _resident((1, d)), _resident(wk.shape), _resident(wv.shape),
                  _resident(wf.shape), _resident((1, LANES)), _resident(place.shape),
                  _resident(ones.shape)],
        out_specs=[blk(B_KAUG),
                   pl.BlockSpec((tm // tk, B_VT, tk), lambda bi, i: (bi * ns + i, 0, 0)),
                   blk(LANES), blk(LANES), blk(LANES)],
        out_shape=[jax.ShapeDtypeStruct((b, s, B_KAUG), BF16),
                   jax.ShapeDtypeStruct((b * s // tk, B_VT, tk), BF16),
                   jax.ShapeDtypeStruct((b, s, LANES), BF16),
                   jax.ShapeDtypeStruct((b, s, LANES), BF16),
                   jax.ShapeDtypeStruct((b, s, LANES), BF16)],
        scratch_shapes=[pltpu.VMEM((1, LANES), F32)],
        compiler_params=_params("parallel", "arbitrary"),
        name="kv_proj_cumsum",
    )(h3, g, wk, wv, wf, bf, place, ones)


def _qproj_kernel(h_ref, g_ref, w_ref, rows_ref, qt_ref):
    tm = h_ref.shape[0]
    xn = _rms(h_ref[...], g_ref[...]).astype(BF16)
    pad = jnp.zeros((LANES - HEAD_DIM - BF16_ROWS, tm), BF16)
    step = 4 * LANES
    for c0 in range(0, B_QK, step):
        y = jnp.dot(xn, w_ref[:, c0:c0 + step], preferred_element_type=F32) * QK_SCALE
        for jj in range(step // LANES):
            yt = y[:, jj * LANES:(jj + 1) * LANES].T.astype(BF16)
            for e in range(2):
                h = (c0 + jj * LANES) // HEAD_DIM + e
                r = h * LANES
                qt_ref[r:r + HEAD_DIM, :] = yt[e * HEAD_DIM:(e + 1) * HEAD_DIM, :]
                qt_ref[r + HEAD_DIM:r + HEAD_DIM + BF16_ROWS, :] = (
                    rows_ref[h * BF16_ROWS:(h + 1) * BF16_ROWS, :])
                qt_ref[r + HEAD_DIM + BF16_ROWS:r + LANES, :] = pad


def _qproj(h, g, w, rows, tm):
    t, d = h.shape
    return pl.pallas_call(
        _qproj_kernel,
        grid=(t // tm,),
        in_specs=[pl.BlockSpec((tm, d), lambda i: (i, 0)), _resident((1, d)), _resident(w.shape),
                  pl.BlockSpec((B_HEADS * BF16_ROWS, tm), lambda i: (0, i))],
        out_specs=pl.BlockSpec((B_KAUG, tm), lambda i: (0, i)),
        out_shape=jax.ShapeDtypeStruct((B_KAUG, t), BF16),
        compiler_params=_params("parallel"),
        name="q_proj",
    )(h, g, w, rows)


FOX_HEADS_PER_STEP = 4


def _fox_kernel(qt_ref, k_ref, vt_ref, o_ref, sa_sc, sb_sc, mxa_sc, mxb_sc, m_sc, acc_sc):
    tq = qt_ref.shape[1]
    tk = tq
    hps = FOX_HEADS_PER_STEP
    j = pl.program_id(2)
    m_sc[...] = jnp.full(m_sc.shape, MASKED, F32)
    acc_sc[...] = jnp.zeros(acc_sc.shape, F32)
    kiota = lax.broadcasted_iota(I32, (tk, tq), 0)
    qiota = lax.broadcasted_iota(I32, (1, tq), 1)
    bufs = ((sa_sc, mxa_sc), (sb_sc, mxb_sc))

    def scores(c, h, buf, qlimit):
        r0 = pl.multiple_of(c * tk, tk)
        s = jnp.dot(k_ref[pl.ds(r0, tk), h * LANES:(h + 1) * LANES],
                    qt_ref[h * LANES:(h + 1) * LANES, :], preferred_element_type=F32)
        if qlimit is not None:
            s = jnp.where(kiota <= qlimit, s, MASKED)
        buf[0][h] = s
        buf[1][h] = _colmax8(s)

    def update(c, h, buf):
        _softmax_pv(buf[0], buf[1], vt_ref[c, h * V_ROWS:(h + 1) * V_ROWS, :], h, m_sc, acc_sc)

    def by_parity(c, fn):
        for par in range(2):
            @pl.when(c % 2 == par)
            def _():
                fn(bufs[par], bufs[1 - par])

    first_limit = qiota + jnp.where(j == 0, 0, tk)
    for h in range(hps):
        scores(0, h, bufs[0], first_limit)

    def stage(c, next_limit):
        def fn(cur, nxt):
            for h in range(hps):
                scores(c + 1, h, nxt, next_limit)
                update(c, h, cur)
        by_parity(c, fn)

    def main_body(c, carry):
        stage(c, None)
        return carry

    lax.fori_loop(0, j - 1, main_body, 0)

    @pl.when(j >= 1)
    def _():
        stage(j - 1, qiota)

    by_parity(j, lambda cur, nxt: [update(j, h, cur) for h in range(hps)])
    outs = [_normalised(acc_sc, h) for h in range(hps)]
    o_ref[...] = jnp.concatenate(outs, axis=0).T.astype(o_ref.dtype)


def _fox(qt, kaug, vt, b, s, tq):
    t = b * s
    nq = s // tq
    hps = FOX_HEADS_PER_STEP
    return pl.pallas_call(
        _fox_kernel,
        grid=(b, B_HEADS // hps, nq),
        in_specs=[
            pl.BlockSpec((hps * LANES, tq), lambda bi, p, j: (p, bi * nq + j)),
            pl.BlockSpec((None, s, hps * LANES), lambda bi, p, j: (bi, 0, p)),
            pl.BlockSpec((nq, hps * V_ROWS, tq), lambda bi, p, j: (bi, p, 0)),
        ],
        out_specs=pl.BlockSpec((tq, hps * HEAD_DIM), lambda bi, p, j: (bi * nq + j, p)),
        out_shape=jax.ShapeDtypeStruct((t, B_QK), BF16),
        scratch_shapes=[
            pltpu.VMEM((hps, tq, tq), F32),
            pltpu.VMEM((hps, tq, tq), F32),
            pltpu.VMEM((hps, SUBLANES, tq), F32),
            pltpu.VMEM((hps, SUBLANES, tq), F32),
            pltpu.VMEM((hps, 1, tq), F32),
            pltpu.VMEM((hps, V_ROWS, tq), F32),
        ],
        compiler_params=_params("parallel", "parallel", "arbitrary"),
        name="fox_attention",
    )(qt, kaug, vt)


def _mlp_kernel(h_ref, o_ref, wo_ref, g_ref, wup_ref, wdn_ref, gf_ref, out_ref, *, tf, final):
    h1 = h_ref[...] + jnp.dot(o_ref[...], wo_ref[...], preferred_element_type=F32)
    xn = _rms(h1, g_ref[...]).astype(BF16)
    acc = h1
    for c0 in range(0, wup_ref.shape[1], tf):
        u = jnp.dot(xn, wup_ref[:, c0:c0 + tf], preferred_element_type=F32)
        a = jnp.square(jnp.maximum(u, 0.0)).astype(BF16)
        acc = acc + jnp.dot(a, wdn_ref[c0:c0 + tf, :], preferred_element_type=F32)
    if final:
        acc = _rms(acc, gf_ref[...])
    out_ref[...] = acc


def _mlp(h, o, wo, g, wup, wdn, gf, final, tm, tf):
    t, d = h.shape
    row = pl.BlockSpec((tm, d), lambda i: (i, 0))
    return pl.pallas_call(
        functools.partial(_mlp_kernel, tf=tf, final=final),
        grid=(t // tm,),
        in_specs=[row, row, _resident(wo.shape), _resident((1, d)), _resident(wup.shape),
                  _resident(wdn.shape), _resident((1, d))],
        out_specs=row,
        out_shape=jax.ShapeDtypeStruct((t, d), F32),
        compiler_params=_params("parallel"),
        name="oproj_mlp",
    )(h, o, wo, g, wup, wdn, gf)


def _dup_heads(w, heads):
    d = w.shape[0]
    w = w.reshape(d, heads, HEAD_DIM)
    return jnp.concatenate([w, w], axis=-1).reshape(d, heads * LANES)


def _prep_w_in_a(w):
    kv = A_KV_HEADS * HEAD_DIM
    o_k, o_v, o_qi = A_Q, A_Q + kv, A_Q + 2 * kv
    o_ki, o_wi = o_qi + A_QI, o_qi + A_QI + HEAD_DIM
    wi = w[:, o_wi:o_wi + IDX_HEADS]
    return jnp.concatenate([
        w[:, :A_Q],
        _dup_heads(w[:, o_k:o_v], A_KV_HEADS),
        w[:, o_v:o_qi],
        w[:, o_qi:o_ki],
        _dup_heads(w[:, o_ki:o_wi], 1),
        jnp.pad(wi, ((0, 0), (0, LANES - IDX_HEADS))),
    ], axis=1).astype(BF16)


def _rope_tables(seq):
    inv = 1.0 / (ROPE_THETA ** (jnp.arange(0, HEAD_DIM, 2, dtype=F32) / HEAD_DIM))
    ang = jnp.arange(seq, dtype=F32)[:, None] * inv[None, :]
    c, s = jnp.cos(ang), jnp.sin(ang)
    return jnp.concatenate([c, c, c, c], axis=1), jnp.concatenate([-s, s, -s, s], axis=1)


def _bias_placement():
    place = np.zeros((BIAS_PIECES * LANES, B_KAUG), np.float32)
    ones = np.zeros((1, B_KAUG), np.float32)
    for h in range(B_HEADS):
        for p in range(BIAS_PIECES):
            place[p * LANES + h, h * LANES + HEAD_DIM + p] = -1.0
            ones[0, h * LANES + HEAD_DIM + BIAS_PIECES + p] = 1.0
    return jnp.asarray(place, BF16), jnp.asarray(ones, F32)


def _query_bias_rows(chi, cmid, clo):
    b, s, _ = chi.shape
    t = b * s
    pcs = [x[:, :, :B_HEADS].reshape(t, B_HEADS).T for x in (chi, cmid, clo)]
    one = jnp.ones((B_HEADS, t), BF16)
    zero = jnp.zeros((B_HEADS, t), BF16)
    rows = [one] * BIAS_PIECES + pcs + [zero] * (BF16_ROWS - 2 * BIAS_PIECES)
    return jnp.stack(rows, axis=1).reshape(B_HEADS * BF16_ROWS, t)


def kernel(x, g_attn_a, w_in_a, w_o_a, g_kv, w_kv_b, b_f, g_attn_b, w_q_b, w_o_b,
           g_mlp, w_up, w_down, g_final):
    b, s, d = x.shape
    t = b * s
    depth = g_mlp.shape[0]
    n_a = g_attn_a.shape[0]
    tm = min(512, s)
    tq = min(ATTN_BLOCK, s)
    tf = 512

    cos, sin = _rope_tables(s)
    gf = g_final.reshape(1, d)
    h = x.reshape(t, d)
    kaug = vt_b = bias_rows = None

    for layer in range(depth):
        if layer < n_a:
            qt, k, vt, qit, ki, wit = _aproj(h, g_attn_a[layer].reshape(1, d),
                                             _prep_w_in_a(w_in_a[layer]), cos, sin, s, tm, tq)
            o = _dsa(qt, qit, wit, ki, k, vt, b, s, tq)
            w_o = w_o_a[layer]
        else:
            jb = layer - n_a
            if jb == 0:
                wk = jnp.pad(w_kv_b[:, :B_QK].reshape(d, B_HEADS, HEAD_DIM),
                             ((0, 0), (0, 0), (0, LANES - HEAD_DIM))).reshape(d, B_KAUG)
                wf = jnp.pad(w_kv_b[:, 2 * B_QK:], ((0, 0), (0, LANES - B_HEADS)))
                bias = jnp.pad(b_f, (0, LANES - B_HEADS)).reshape(1, LANES)
                place, ones = _bias_placement()
                kaug, vt_b, chi, cmid, clo = _kvproj(
                    h.reshape(b, s, d), g_kv.reshape(1, d), wk.astype(BF16),
                    w_kv_b[:, B_QK:2 * B_QK].astype(BF16), wf.astype(BF16), bias,
                    place, ones, tm, tq)
                bias_rows = _query_bias_rows(chi, cmid, clo)
            qt = _qproj(h, g_attn_b[jb].reshape(1, d), w_q_b[jb].astype(BF16), bias_rows, tm)
            o = _fox(qt, kaug, vt_b, b, s, tq)
            w_o = w_o_b[jb]
        h = _mlp(h, o, w_o.astype(BF16), g_mlp[layer].reshape(1, d),
                 w_up[layer].astype(BF16), w_down[layer].astype(BF16), gf,
                 layer == depth - 1, tm, tf)
    return h.reshape(b, s, d)
```

```python
import functools

import numpy as np
import jax
import jax.numpy as jnp
from jax import lax
from jax.experimental import pallas as pl
from jax.experimental.pallas import tpu as pltpu

LANES = 128
SUBLANES = 8
BF16_ROWS = 16
HEAD_DIM = 64
ROPE_HALF = HEAD_DIM // 2
CHUNK = 64
A_HEADS = 16
A_KV_HEADS = 4
A_GROUP = A_HEADS // A_KV_HEADS
IDX_HEADS = 8
TOPK_MAX = 256
B_HEADS = 16
ROPE_THETA = 10000.0
EPS = 1e-6
INT_MIN = -(2 ** 31)
MASKED = -1e30
LOG2E = 1.4426950408889634
QK_SCALE = HEAD_DIM ** -0.5 * LOG2E
VMEM_LIMIT_BYTES = 56 * 1024 * 1024
ATTN_BLOCK = 256

F32 = jnp.float32
BF16 = jnp.bfloat16
I32 = jnp.int32


def _params(*sem):
    return pltpu.CompilerParams(dimension_semantics=sem,
                                vmem_limit_bytes=VMEM_LIMIT_BYTES)


def _resident(shape):
    nd = len(shape)
    return pl.BlockSpec(shape, lambda *_: (0,) * nd, pipeline_mode=pl.Buffered(1))


def _rms(x, g):
    return x * lax.rsqrt(jnp.mean(x * x, axis=-1, keepdims=True) + EPS) * g


def _colsum8(x):
    n, w = x.shape
    return x.reshape(n // SUBLANES, SUBLANES, w).sum(axis=0)


def _colmax8(x):
    n, w = x.shape
    return x.reshape(n // SUBLANES, SUBLANES, w).max(axis=0)


V_ROWS = HEAD_DIM + BF16_ROWS


def _store_value_head(vt_ref, sb, head, yt):
    r = head * V_ROWS
    tk = yt.shape[1]
    vt_ref[sb, r:r + HEAD_DIM, :] = yt
    first_row = lax.broadcasted_iota(I32, (BF16_ROWS, tk), 0) == 0
    vt_ref[sb, r + HEAD_DIM:r + V_ROWS, :] = jnp.where(first_row, 1.0, 0.0).astype(BF16)


def _softmax_pv(s_buf, mx_buf, vt, h, m_sc, acc_sc):
    m_old = m_sc[h]
    m_new = jnp.maximum(m_old, jnp.max(mx_buf[h], axis=0, keepdims=True))
    alpha = jnp.exp2(m_old - m_new)
    p = jnp.exp2(s_buf[h] - m_new)
    acc_sc[h] = alpha * acc_sc[h] + jnp.dot(vt, p.astype(BF16), preferred_element_type=F32)
    m_sc[h] = m_new


def _normalised(acc_sc, h):
    acc = acc_sc[h]
    return acc[0:HEAD_DIM, :] / acc[HEAD_DIM:HEAD_DIM + 1, :]


A_Q = A_HEADS * HEAD_DIM
A_KD = A_KV_HEADS * LANES
A_V = A_KV_HEADS * HEAD_DIM
A_VT = A_KV_HEADS * V_ROWS
A_QI = IDX_HEADS * HEAD_DIM
A_OFF_K = A_Q
A_OFF_V = A_OFF_K + A_KD
A_OFF_QI = A_OFF_V + A_V
A_OFF_KI = A_OFF_QI + A_QI
A_OFF_WI = A_OFF_KI + LANES
A_COLS = A_OFF_WI + LANES


def _aproj_kernel(h_ref, g_ref, w_ref, cos_ref, sin_ref,
                  qt_ref, k_ref, vt_ref, qit_ref, ki_ref, wit_ref, *, wi_scale):
    tm = h_ref.shape[0]
    tk = vt_ref.shape[2]
    xn = _rms(h_ref[...], g_ref[...]).astype(BF16)
    cos = cos_ref[...]
    sin = sin_ref[...]
    lane = lax.broadcasted_iota(I32, (tm, LANES), 1)
    first_half = (lane & ROPE_HALF) == 0

    def rope(y):
        ys = jnp.where(first_half, pltpu.roll(y, LANES - ROPE_HALF, 1),
                       pltpu.roll(y, ROPE_HALF, 1))
        return y * cos + ys * sin

    def chunks(c0, n):
        step = 4 * LANES
        for s0 in range(0, n, step):
            sn = min(step, n - s0)
            y = jnp.dot(xn, w_ref[:, c0 + s0:c0 + s0 + sn], preferred_element_type=F32)
            for j in range(sn // LANES):
                yield (s0 // LANES + j), y[:, j * LANES:(j + 1) * LANES]

    for j, y in chunks(0, A_Q):
        qt_ref[j * LANES:(j + 1) * LANES, :] = (rope(y) * QK_SCALE).T.astype(BF16)
    for j, y in chunks(A_OFF_K, A_KD):
        k_ref[:, j * LANES:(j + 1) * LANES] = rope(y).astype(BF16)
    for j, y in chunks(A_OFF_V, A_V):
        for sb in range(tm // tk):
            yt = y[sb * tk:(sb + 1) * tk].T.astype(BF16)
            for e in range(2):
                _store_value_head(vt_ref, sb, 2 * j + e, yt[e * HEAD_DIM:(e + 1) * HEAD_DIM, :])
    for j, y in chunks(A_OFF_QI, A_QI):
        qit_ref[j * LANES:(j + 1) * LANES, :] = rope(y).T.astype(BF16)
    for j, y in chunks(A_OFF_KI, LANES):
        ki_ref[...] = rope(y).astype(BF16)
    for j, y in chunks(A_OFF_WI, LANES):
        wit_ref[...] = (y * wi_scale).T[:IDX_HEADS, :]


def _aproj(h, g, w, cos, sin, seq, tm, tk):
    t, d = h.shape
    npos = seq // tm
    row = lambda n: pl.BlockSpec((tm, n), lambda i: (i, 0))
    col = lambda n: pl.BlockSpec((n, tm), lambda i: (0, i))
    tab = pl.BlockSpec((tm, LANES), lambda i: (i % npos, 0))
    wi_scale = IDX_HEADS ** -0.5 * HEAD_DIM ** -0.5
    return pl.pallas_call(
        functools.partial(_aproj_kernel, wi_scale=wi_scale),
        grid=(t // tm,),
        in_specs=[row(d), _resident((1, d)), _resident((d, A_COLS)), tab, tab],
        out_specs=[col(A_Q), row(A_KD),
                   pl.BlockSpec((tm // tk, A_VT, tk), lambda i: (i, 0, 0)),
                   col(A_QI), row(LANES), col(IDX_HEADS)],
        out_shape=[jax.ShapeDtypeStruct((A_Q, t), BF16),
                   jax.ShapeDtypeStruct((t, A_KD), BF16),
                   jax.ShapeDtypeStruct((t // tk, A_VT, tk), BF16),
                   jax.ShapeDtypeStruct((A_QI, t), BF16),
                   jax.ShapeDtypeStruct((t, LANES), BF16),
                   jax.ShapeDtypeStruct((IDX_HEADS, t), F32)],
        compiler_params=_params("parallel"),
        name="a_proj",
    )(h, g, w, cos, sin)


def _dsa_kernel(qt_ref, qit_ref, wit_ref, ki_ref, k_ref, vt_ref, o_ref,
                key_sc, qiw_sc, qw_sc, sa_sc, sb_sc, mxa_sc, mxb_sc, m_sc, acc_sc,
                *, n_sel, idx_bits):
    tq = qt_ref.shape[1]
    tk = tq
    j = pl.program_id(1)
    nkb = j + 1
    qpos = j * tq + lax.broadcasted_iota(I32, (1, tq), 1)
    limit = (qpos // CHUNK + 1) * CHUNK
    kiota = lax.broadcasted_iota(I32, (tk, tq), 0)
    zeros_half = jnp.zeros((HEAD_DIM, tq), BF16)

    for h in range(IDX_HEADS):
        qiw_sc[h, 0:HEAD_DIM, :] = qit_ref[h * HEAD_DIM:(h + 1) * HEAD_DIM, :]
        qiw_sc[h, HEAD_DIM:LANES, :] = zeros_half
    for h in range(A_HEADS):
        qw_sc[h, 0:HEAD_DIM, :] = qt_ref[h * HEAD_DIM:(h + 1) * HEAD_DIM, :]
        qw_sc[h, HEAD_DIM:LANES, :] = zeros_half

    def score_body(c, carry):
        kc = ki_ref[pl.ds(pl.multiple_of(c * tk, tk), tk), :]
        sc = jnp.zeros((tk, tq), F32)
        for h in range(IDX_HEADS):
            lg = jnp.dot(kc, qiw_sc[h], preferred_element_type=F32)
            sc = sc + jnp.maximum(lg, 0.0) * wit_ref[h:h + 1, :]
        sc = jnp.where(sc == 0.0, 0.0, sc)
        bits = pltpu.bitcast(sc, I32)
        key = bits ^ ((bits >> 31) & 0x7FFFFFFF)
        key_sc[c] = jnp.where(c * tk + kiota < limit, key, INT_MIN)
        return carry

    lax.fori_loop(0, nkb, score_body, 0)

    @pl.when(nkb % 2 == 1)
    def _():
        key_sc[nkb] = jnp.full((tk, tq), INT_MIN, I32)

    def count(pred):
        def body(cc, acc):
            for u in range(2):
                c = 2 * cc + u
                acc = acc + _colsum8(jnp.where(pred(c, key_sc[c]), 1.0, 0.0))
            return acc
        acc = lax.fori_loop(0, (nkb + 1) // 2, body, jnp.zeros((SUBLANES, tq), F32))
        return jnp.sum(acc, axis=0, keepdims=True)

    def try_threshold(cand, thr, cnt_thr):
        cnt = count(lambda c, key: key >= cand)
        ok = cnt >= n_sel
        return jnp.where(ok, cand, thr), jnp.where(ok, cnt, cnt_thr)

    thr0 = jnp.full((1, tq), INT_MIN, I32)
    thr, cnt_thr = try_threshold(jnp.zeros((1, tq), I32), thr0, jnp.zeros((1, tq), F32))

    def bit_body(i, carry):
        thr, cnt_thr = carry
        cand = thr | jnp.left_shift(jnp.int32(1), 30 - i)
        return try_threshold(cand, thr, cnt_thr)

    thr, cnt_thr = lax.fori_loop(0, 31, bit_body, (thr, cnt_thr))
    thr = jnp.maximum(thr, INT_MIN + 1)

    @pl.when(jnp.max(cnt_thr) > n_sel)
    def _():
        need = n_sel - count(lambda c, key: key > thr)
        pos = jnp.zeros((1, tq), I32)
        for b in range(idx_bits - 1, -1, -1):
            cand = pos | (1 << b)
            cnt = count(lambda c, key: jnp.where(key == thr, c * tk + kiota, 2 ** 30) < cand)
            pos = jnp.where(cnt < need, cand, pos)

        def demote(c, carry):
            key = key_sc[c]
            tie_pos = jnp.where(key == thr, c * tk + kiota, -1)
            key_sc[c] = jnp.where(tie_pos > pos, INT_MIN, key)
            return carry

        lax.fori_loop(0, nkb, demote, 0)

    m_sc[...] = jnp.full(m_sc.shape, MASKED, F32)
    acc_sc[...] = jnp.zeros(acc_sc.shape, F32)

    bufs = ((sa_sc, mxa_sc), (sb_sc, mxb_sc))

    def select_bias(c):
        return jnp.where(key_sc[c] >= thr, 0.0, MASKED)

    def scores(c, h, buf, bias):
        n = h // A_GROUP
        r0 = pl.multiple_of(c * tk, tk)
        s = jnp.dot(k_ref[pl.ds(r0, tk), n * LANES:(n + 1) * LANES], qw_sc[h],
                    preferred_element_type=F32) + bias
        buf[0][h] = s
        buf[1][h] = _colmax8(s)

    def update(c, h, buf):
        n = h // A_GROUP
        _softmax_pv(buf[0], buf[1], vt_ref[c, n * V_ROWS:(n + 1) * V_ROWS, :], h, m_sc, acc_sc)

    def by_parity(c, fn):
        for par in range(2):
            @pl.when(c % 2 == par)
            def _():
                fn(bufs[par], bufs[1 - par])

    bias0 = select_bias(0)
    for h in range(A_HEADS):
        scores(0, h, bufs[0], bias0)

    def attn_body(c, carry):
        def fn(cur, nxt):
            bias = select_bias(c + 1)
            for h in range(A_HEADS):
                scores(c + 1, h, nxt, bias)
                update(c, h, cur)
        by_parity(c, fn)
        return carry

    lax.fori_loop(0, j, attn_body, 0)
    by_parity(j, lambda cur, nxt: [update(j, h, cur) for h in range(A_HEADS)])

    for i in range(A_HEADS // 2):
        pair = [_normalised(acc_sc, h) for h in (2 * i, 2 * i + 1)]
        o_ref[:, i * LANES:(i + 1) * LANES] = jnp.concatenate(pair, axis=0).T.astype(o_ref.dtype)


def _dsa(qt, qit, wit, ki, k, vt, b, s, tq):
    t = b * s
    nq = s // tq
    n_sel = min(TOPK_MAX, s // 4)
    idx_bits = max(1, (s - 1).bit_length())
    qcol = lambda n: pl.BlockSpec((n, tq), lambda bi, j: (0, bi * nq + j))
    full = lambda n: pl.BlockSpec((s, n), lambda bi, j: (bi, 0))
    return pl.pallas_call(
        functools.partial(_dsa_kernel, n_sel=float(n_sel), idx_bits=idx_bits),
        grid=(b, nq),
        in_specs=[qcol(A_Q), qcol(A_QI), qcol(IDX_HEADS), full(LANES), full(A_KD),
                  pl.BlockSpec((nq, A_VT, tq), lambda bi, j: (bi, 0, 0))],
        out_specs=pl.BlockSpec((tq, A_Q), lambda bi, j: (bi * nq + j, 0)),
        out_shape=jax.ShapeDtypeStruct((t, A_Q), BF16),
        scratch_shapes=[
            pltpu.VMEM((nq + nq % 2, tq, tq), I32),
            pltpu.VMEM((IDX_HEADS, LANES, tq), BF16),
            pltpu.VMEM((A_HEADS, LANES, tq), BF16),
            pltpu.VMEM((A_HEADS, tq, tq), F32),
            pltpu.VMEM((A_HEADS, tq, tq), F32),
            pltpu.VMEM((A_HEADS, SUBLANES, tq), F32),
            pltpu.VMEM((A_HEADS, SUBLANES, tq), F32),
            pltpu.VMEM((A_HEADS, 1, tq), F32),
            pltpu.VMEM((A_HEADS, V_ROWS, tq), F32),
        ],
        compiler_params=_params("parallel", "arbitrary"),
        name="dsa_attention",
    )(qt, qit, wit, ki, k, vt)


B_QK = B_HEADS * HEAD_DIM
B_KAUG = B_HEADS * LANES
B_VT = B_HEADS * V_ROWS
BIAS_PIECES = 3


def _split3(x):
    hi = x.astype(BF16)
    r1 = x - hi.astype(F32)
    mid = r1.astype(BF16)
    low = (r1 - mid.astype(F32)).astype(BF16)
    return hi, mid, low


def _kvproj_kernel(h_ref, g_ref, wk_ref, wv_ref, wf_ref, bf_ref, place_ref, ones_ref,
                   kaug_ref, vt_ref, chi_ref, cmid_ref, clo_ref, carry_sc):
    tm = h_ref.shape[0]
    tk = vt_ref.shape[2]

    @pl.when(pl.program_id(1) == 0)
    def _():
        carry_sc[...] = jnp.zeros(carry_sc.shape, F32)

    xn = _rms(h_ref[...], g_ref[...]).astype(BF16)
    z = jnp.dot(xn, wf_ref[...], preferred_element_type=F32) + bf_ref[...]
    logf = jnp.minimum(z, 0.0) - jnp.log1p(jnp.exp(-jnp.abs(z)))
    tri = (lax.broadcasted_iota(I32, (tm, tm), 0) >=
           lax.broadcasted_iota(I32, (tm, tm), 1)).astype(BF16)
    cs = sum(jnp.dot(tri, piece, preferred_element_type=F32) for piece in _split3(logf))
    cum = cs + carry_sc[...]
    carry_sc[...] = cum[tm - 1:tm, :]
    hi, mid, low = _split3(cum * LOG2E)
    chi_ref[...] = hi
    cmid_ref[...] = mid
    clo_ref[...] = low
    pieces = jnp.concatenate([hi, mid, low], axis=1)

    step = 4 * LANES
    for c0 in range(0, B_KAUG, step):
        y = (jnp.dot(xn, wk_ref[:, c0:c0 + step], preferred_element_type=F32) +
             jnp.dot(pieces, place_ref[:, c0:c0 + step], preferred_element_type=F32) +
             ones_ref[:, c0:c0 + step])
        kaug_ref[:, c0:c0 + step] = y.astype(BF16)
    for c0 in range(0, B_QK, step):
        y = jnp.dot(xn, wv_ref[:, c0:c0 + step], preferred_element_type=F32)
        for jj in range(step // LANES):
            head = (c0 + jj * LANES) // HEAD_DIM
            for sb in range(tm // tk):
                yt = y[sb * tk:(sb + 1) * tk, jj * LANES:(jj + 1) * LANES].T.astype(BF16)
                for e in range(2):
                    _store_value_head(vt_ref, sb, head + e, yt[e * HEAD_DIM:(e + 1) * HEAD_DIM, :])


def _kvproj(h3, g, wk, wv, wf, bf, place, ones, tm, tk):
    b, s, d = h3.shape
    ns = s // tm
    blk = lambda n: pl.BlockSpec((None, tm, n), lambda bi, i: (bi, i, 0))
    return pl.pallas_call(
        _kvproj_kernel,
        grid=(b, ns),
        in_specs=[blk(d), _resident((1, d)), _resident(wk.shape), _resident(wv.shape),
                  _resident(wf.shape), _resident((1, LANES)), _resident(place.shape),
                  _resident(ones.shape)],
        out_specs=[blk(B_KAUG),
                   pl.BlockSpec((tm // tk, B_VT, tk), lambda bi, i: (bi * ns + i, 0, 0)),
                   blk(LANES), blk(LANES), blk(LANES)],
        out_shape=[jax.ShapeDtypeStruct((b, s, B_KAUG), BF16),
                   jax.ShapeDtypeStruct((b * s // tk, B_VT, tk), BF16),
                   jax.ShapeDtypeStruct((b, s, LANES), BF16),
                   jax.ShapeDtypeStruct((b, s, LANES), BF16),
                   jax.ShapeDtypeStruct((b, s, LANES), BF16)],
        scratch_shapes=[pltpu.VMEM((1, LANES), F32)],
        compiler_params=_params("parallel", "arbitrary"),
        name="kv_proj_cumsum",
    )(h3, g, wk, wv, wf, bf, place, ones)


def _qproj_kernel(h_ref, g_ref, w_ref, rows_ref, qt_ref):
    tm = h_ref.shape[0]
    xn = _rms(h_ref[...], g_ref[...]).astype(BF16)
    pad = jnp.zeros((LANES - HEAD_DIM - BF16_ROWS, tm), BF16)
    step = 4 * LANES
    for c0 in range(0, B_QK, step):
        y = jnp.dot(xn, w_ref[:, c0:c0 + step], preferred_element_type=F32) * QK_SCALE
        for jj in range(step // LANES):
            yt = y[:, jj * LANES:(jj + 1) * LANES].T.astype(BF16)
            for e in range(2):
                h = (c0 + jj * LANES) // HEAD_DIM + e
                r = h * LANES
                qt_ref[r:r + HEAD_DIM, :] = yt[e * HEAD_DIM:(e + 1) * HEAD_DIM, :]
                qt_ref[r + HEAD_DIM:r + HEAD_DIM + BF16_ROWS, :] = (
                    rows_ref[h * BF16_ROWS:(h + 1) * BF16_ROWS, :])
                qt_ref[r + HEAD_DIM + BF16_ROWS:r + LANES, :] = pad


def _qproj(h, g, w, rows, tm):
    t, d = h.shape
    return pl.pallas_call(
        _qproj_kernel,
        grid=(t // tm,),
        in_specs=[pl.BlockSpec((tm, d), lambda i: (i, 0)), _resident((1, d)), _resident(w.shape),
                  pl.BlockSpec((B_HEADS * BF16_ROWS, tm), lambda i: (0, i))],
        out_specs=pl.BlockSpec((B_KAUG, tm), lambda i: (0, i)),
        out_shape=jax.ShapeDtypeStruct((B_KAUG, t), BF16),
        compiler_params=_params("parallel"),
        name="q_proj",
    )(h, g, w, rows)


FOX_HEADS_PER_STEP = 8


def _fox_kernel(qt_ref, k_ref, vt_ref, o_ref, sa_sc, sb_sc, mxa_sc, mxb_sc, m_sc, acc_sc):
    tq = qt_ref.shape[1]
    tk = tq
    hps = FOX_HEADS_PER_STEP
    j = pl.program_id(2)
    m_sc[...] = jnp.full(m_sc.shape, MASKED, F32)
    acc_sc[...] = jnp.zeros(acc_sc.shape, F32)
    kiota = lax.broadcasted_iota(I32, (tk, tq), 0)
    qiota = lax.broadcasted_iota(I32, (1, tq), 1)
    bufs = ((sa_sc, mxa_sc), (sb_sc, mxb_sc))

    def scores(c, h, buf, qlimit):
        r0 = pl.multiple_of(c * tk, tk)
        s = jnp.dot(k_ref[pl.ds(r0, tk), h * LANES:(h + 1) * LANES],
                    qt_ref[h * LANES:(h + 1) * LANES, :], preferred_element_type=F32)
        if qlimit is not None:
            s = jnp.where(kiota <= qlimit, s, MASKED)
        buf[0][h] = s
        buf[1][h] = _colmax8(s)

    def update(c, h, buf):
        _softmax_pv(buf[0], buf[1], vt_ref[c, h * V_ROWS:(h + 1) * V_ROWS, :], h, m_sc, acc_sc)

    def by_parity(c, fn):
        for par in range(2):
            @pl.when(c % 2 == par)
            def _():
                fn(bufs[par], bufs[1 - par])

    first_limit = qiota + jnp.where(j == 0, 0, tk)
    for h in range(hps):
        scores(0, h, bufs[0], first_limit)

    def stage(c, next_limit):
        def fn(cur, nxt):
            for h in range(hps):
                scores(c + 1, h, nxt, next_limit)
                update(c, h, cur)
        by_parity(c, fn)

    def main_body(c, carry):
        stage(c, None)
        return carry

    lax.fori_loop(0, j - 1, main_body, 0)

    @pl.when(j >= 1)
    def _():
        stage(j - 1, qiota)

    by_parity(j, lambda cur, nxt: [update(j, h, cur) for h in range(hps)])
    outs = [_normalised(acc_sc, h) for h in range(hps)]
    o_ref[...] = jnp.concatenate(outs, axis=0).T.astype(o_ref.dtype)


def _fox(qt, kaug, vt, b, s, tq):
    t = b * s
    nq = s // tq
    hps = FOX_HEADS_PER_STEP
    return pl.pallas_call(
        _fox_kernel,
        grid=(b, B_HEADS // hps, nq),
        in_specs=[
            pl.BlockSpec((hps * LANES, tq), lambda bi, p, j: (p, bi * nq + j)),
            pl.BlockSpec((None, s, hps * LANES), lambda bi, p, j: (bi, 0, p)),
            pl.BlockSpec((nq, hps * V_ROWS, tq), lambda bi, p, j: (bi, p, 0)),
        ],
        out_specs=pl.BlockSpec((tq, hps * HEAD_DIM), lambda bi, p, j: (bi * nq + j, p)),
        out_shape=jax.ShapeDtypeStruct((t, B_QK), BF16),
        scratch_shapes=[
            pltpu.VMEM((hps, tq, tq), F32),
            pltpu.VMEM((hps, tq, tq), F32),
            pltpu.VMEM((hps, SUBLANES, tq), F32),
            pltpu.VMEM((hps, SUBLANES, tq), F32),
            pltpu.VMEM((hps, 1, tq), F32),
            pltpu.VMEM((hps, V_ROWS, tq), F32),
        ],
        compiler_params=_params("parallel", "parallel", "arbitrary"),
        name="fox_attention",
    )(qt, kaug, vt)


def _mlp_kernel(h_ref, o_ref, wo_ref, g_ref, wup_ref, wdn_ref, gf_ref, out_ref, *, tf, final):
    h1 = h_ref[...] + jnp.dot(o_ref[...], wo_ref[...], preferred_element_type=F32)
    xn = _rms(h1, g_ref[...]).astype(BF16)
    acc = h1
    for c0 in range(0, wup_ref.shape[1], tf):
        u = jnp.dot(xn, wup_ref[:, c0:c0 + tf], preferred_element_type=F32)
        a = jnp.square(jnp.maximum(u, 0.0)).astype(BF16)
        acc = acc + jnp.dot(a, wdn_ref[c0:c0 + tf, :], preferred_element_type=F32)
    if final:
        acc = _rms(acc, gf_ref[...])
    out_ref[...] = acc


def _mlp(h, o, wo, g, wup, wdn, gf, final, tm, tf):
    t, d = h.shape
    row = pl.BlockSpec((tm, d), lambda i: (i, 0))
    return pl.pallas_call(
        functools.partial(_mlp_kernel, tf=tf, final=final),
        grid=(t // tm,),
        in_specs=[row, row, _resident(wo.shape), _resident((1, d)), _resident(wup.shape),
                  _resident(wdn.shape), _resident((1, d))],
        out_specs=row,
        out_shape=jax.ShapeDtypeStruct((t, d), F32),
        compiler_params=_params("parallel"),
        name="oproj_mlp",
    )(h, o, wo, g, wup, wdn, gf)


def _dup_heads(w, heads):
    d = w.shape[0]
    w = w.reshape(d, heads, HEAD_DIM)
    return jnp.concatenate([w, w], axis=-1).reshape(d, heads * LANES)


def _prep_w_in_a(w):
    kv = A_KV_HEADS * HEAD_DIM
    o_k, o_v, o_qi = A_Q, A_Q + kv, A_Q + 2 * kv
    o_ki, o_wi = o_qi + A_QI, o_qi + A_QI + HEAD_DIM
    wi = w[:, o_wi:o_wi + IDX_HEADS]
    return jnp.concatenate([
        w[:, :A_Q],
        _dup_heads(w[:, o_k:o_v], A_KV_HEADS),
        w[:, o_v:o_qi],
        w[:, o_qi:o_ki],
        _dup_heads(w[:, o_ki:o_wi], 1),
        jnp.pad(wi, ((0, 0), (0, LANES - IDX_HEADS))),
    ], axis=1).astype(BF16)


def _rope_tables(seq):
    inv = 1.0 / (ROPE_THETA ** (jnp.arange(0, HEAD_DIM, 2, dtype=F32) / HEAD_DIM))
    ang = jnp.arange(seq, dtype=F32)[:, None] * inv[None, :]
    c, s = jnp.cos(ang), jnp.sin(ang)
    return jnp.concatenate([c, c, c, c], axis=1), jnp.concatenate([-s, s, -s, s], axis=1)


def _bias_placement():
    place = np.zeros((BIAS_PIECES * LANES, B_KAUG), np.float32)
    ones = np.zeros((1, B_KAUG), np.float32)
    for h in range(B_HEADS):
        for p in range(BIAS_PIECES):
            place[p * LANES + h, h * LANES + HEAD_DIM + p] = -1.0
            ones[0, h * LANES + HEAD_DIM + BIAS_PIECES + p] = 1.0
    return jnp.asarray(place, BF16), jnp.asarray(ones, F32)


def _query_bias_rows(chi, cmid, clo):
    b, s, _ = chi.shape
    t = b * s
    pcs = [x[:, :, :B_HEADS].reshape(t, B_HEADS).T for x in (chi, cmid, clo)]
    one = jnp.ones((B_HEADS, t), BF16)
    zero = jnp.zeros((B_HEADS, t), BF16)
    rows = [one] * BIAS_PIECES + pcs + [zero] * (BF16_ROWS - 2 * BIAS_PIECES)
    return jnp.stack(rows, axis=1).reshape(B_HEADS * BF16_ROWS, t)


def kernel(x, g_attn_a, w_in_a, w_o_a, g_kv, w_kv_b, b_f, g_attn_b, w_q_b, w_o_b,
           g_mlp, w_up, w_down, g_final):
    b, s, d = x.shape
    t = b * s
    depth = g_mlp.shape[0]
    n_a = g_attn_a.shape[0]
    tm = min(512, s)
    tq = min(ATTN_BLOCK, s)
    tf = 512

    cos, sin = _rope_tables(s)
    gf = g_final.reshape(1, d)
    h = x.reshape(t, d)
    kaug = vt_b = bias_rows = None

    for layer in range(depth):
        if layer < n_a:
            qt, k, vt, qit, ki, wit = _aproj(h, g_attn_a[layer].reshape(1, d),
                                             _prep_w_in_a(w_in_a[layer]), cos, sin, s, tm, tq)
            o = _dsa(qt, qit, wit, ki, k, vt, b, s, tq)
            w_o = w_o_a[layer]
        else:
            jb = layer - n_a
            if jb == 0:
                wk = jnp.pad(w_kv_b[:, :B_QK].reshape(d, B_HEADS, HEAD_DIM),
                             ((0, 0), (0, 0), (0, LANES - HEAD_DIM))).reshape(d, B_KAUG)
                wf = jnp.pad(w_kv_b[:, 2 * B_QK:], ((0, 0), (0, LANES - B_HEADS)))
                bias = jnp.pad(b_f, (0, LANES - B_HEADS)).reshape(1, LANES)
                place, ones = _bias_placement()
                kaug, vt_b, chi, cmid, clo = _kvproj(
                    h.reshape(b, s, d), g_kv.reshape(1, d), wk.astype(BF16),
                    w_kv_b[:, B_QK:2 * B_QK].astype(BF16), wf.astype(BF16), bias,
                    place, ones, tm, tq)
                bias_rows = _query_bias_rows(chi, cmid, clo)
            qt = _qproj(h, g_attn_b[jb].reshape(1, d), w_q_b[jb].astype(BF16), bias_rows, tm)
            o = _fox(qt, kaug, vt_b, b, s, tq)
            w_o = w_o_b[jb]
        h = _mlp(h, o, w_o.astype(BF16), g_mlp[layer].reshape(1, d),
                 w_up[layer].astype(BF16), w_down[layer].astype(BF16), gf,
                 layer == depth - 1, tm, tf)
    return h.reshape(b, s, d)
```

```python
import functools

import numpy as np
import jax
import jax.numpy as jnp
from jax import lax
from jax.experimental import pallas as pl
from jax.experimental.pallas import tpu as pltpu

LANES = 128
SUBLANES = 8
BF16_ROWS = 16
HEAD_DIM = 64
ROPE_HALF = HEAD_DIM // 2
CHUNK = 64
A_HEADS = 16
A_KV_HEADS = 4
A_GROUP = A_HEADS // A_KV_HEADS
IDX_HEADS = 8
TOPK_MAX = 256
B_HEADS = 16
ROPE_THETA = 10000.0
EPS = 1e-6
INT_MIN = -(2 ** 31)
MASKED = -1e30
LOG2E = 1.4426950408889634
QK_SCALE = HEAD_DIM ** -0.5 * LOG2E
VMEM_LIMIT_BYTES = 56 * 1024 * 1024
ATTN_BLOCK = 256

F32 = jnp.float32
BF16 = jnp.bfloat16
I32 = jnp.int32


def _params(*sem):
    return pltpu.CompilerParams(dimension_semantics=sem,
                                vmem_limit_bytes=VMEM_LIMIT_BYTES)


def _resident(shape):
    nd = len(shape)
    return pl.BlockSpec(shape, lambda *_: (0,) * nd, pipeline_mode=pl.Buffered(1))


def _rms(x, g):
    return x * lax.rsqrt(jnp.mean(x * x, axis=-1, keepdims=True) + EPS) * g


def _colsum8(x):
    n, w = x.shape
    return x.reshape(n // SUBLANES, SUBLANES, w).sum(axis=0)


def _colmax8(x):
    n, w = x.shape
    return x.reshape(n // SUBLANES, SUBLANES, w).max(axis=0)


V_ROWS = HEAD_DIM + BF16_ROWS


def _store_value_head(vt_ref, sb, head, yt):
    r = head * V_ROWS
    tk = yt.shape[1]
    vt_ref[sb, r:r + HEAD_DIM, :] = yt
    first_row = lax.broadcasted_iota(I32, (BF16_ROWS, tk), 0) == 0
    vt_ref[sb, r + HEAD_DIM:r + V_ROWS, :] = jnp.where(first_row, 1.0, 0.0).astype(BF16)


def _softmax_pv(s_buf, mx_buf, vt, h, m_sc, acc_sc):
    m_old = m_sc[h]
    m_new = jnp.maximum(m_old, jnp.max(mx_buf[h], axis=0, keepdims=True))
    alpha = jnp.exp2(m_old - m_new)
    p = jnp.exp2(s_buf[h] - m_new)
    acc_sc[h] = alpha * acc_sc[h] + jnp.dot(vt, p.astype(BF16), preferred_element_type=F32)
    m_sc[h] = m_new


def _normalised(acc_sc, h):
    acc = acc_sc[h]
    return acc[0:HEAD_DIM, :] / acc[HEAD_DIM:HEAD_DIM + 1, :]


A_Q = A_HEADS * HEAD_DIM
A_KD = A_KV_HEADS * LANES
A_V = A_KV_HEADS * HEAD_DIM
A_VT = A_KV_HEADS * V_ROWS
A_QI = IDX_HEADS * HEAD_DIM
A_OFF_K = A_Q
A_OFF_V = A_OFF_K + A_KD
A_OFF_QI = A_OFF_V + A_V
A_OFF_KI = A_OFF_QI + A_QI
A_OFF_WI = A_OFF_KI + LANES
A_COLS = A_OFF_WI + LANES


def _aproj_kernel(h_ref, g_ref, w_ref, cos_ref, sin_ref,
                  qt_ref, k_ref, vt_ref, qit_ref, ki_ref, wit_ref, *, wi_scale):
    tm = h_ref.shape[0]
    tk = vt_ref.shape[2]
    xn = _rms(h_ref[...], g_ref[...]).astype(BF16)
    cos = cos_ref[...]
    sin = sin_ref[...]
    lane = lax.broadcasted_iota(I32, (tm, LANES), 1)
    first_half = (lane & ROPE_HALF) == 0

    def rope(y):
        ys = jnp.where(first_half, pltpu.roll(y, LANES - ROPE_HALF, 1),
                       pltpu.roll(y, ROPE_HALF, 1))
        return y * cos + ys * sin

    def chunks(c0, n):
        step = 4 * LANES
        for s0 in range(0, n, step):
            sn = min(step, n - s0)
            y = jnp.dot(xn, w_ref[:, c0 + s0:c0 + s0 + sn], preferred_element_type=F32)
            for j in range(sn // LANES):
                yield (s0 // LANES + j), y[:, j * LANES:(j + 1) * LANES]

    for j, y in chunks(0, A_Q):
        qt_ref[j * LANES:(j + 1) * LANES, :] = (rope(y) * QK_SCALE).T.astype(BF16)
    for j, y in chunks(A_OFF_K, A_KD):
        k_ref[:, j * LANES:(j + 1) * LANES] = rope(y).astype(BF16)
    for j, y in chunks(A_OFF_V, A_V):
        for sb in range(tm // tk):
            yt = y[sb * tk:(sb + 1) * tk].T.astype(BF16)
            for e in range(2):
                _store_value_head(vt_ref, sb, 2 * j + e, yt[e * HEAD_DIM:(e + 1) * HEAD_DIM, :])
    for j, y in chunks(A_OFF_QI, A_QI):
        qit_ref[j * LANES:(j + 1) * LANES, :] = rope(y).T.astype(BF16)
    for j, y in chunks(A_OFF_KI, LANES):
        ki_ref[...] = rope(y).astype(BF16)
    for j, y in chunks(A_OFF_WI, LANES):
        wit_ref[...] = (y * wi_scale).T[:IDX_HEADS, :]


def _aproj(h, g, w, cos, sin, seq, tm, tk):
    t, d = h.shape
    npos = seq // tm
    row = lambda n: pl.BlockSpec((tm, n), lambda i: (i, 0))
    col = lambda n: pl.BlockSpec((n, tm), lambda i: (0, i))
    tab = pl.BlockSpec((tm, LANES), lambda i: (i % npos, 0))
    wi_scale = IDX_HEADS ** -0.5 * HEAD_DIM ** -0.5
    return pl.pallas_call(
        functools.partial(_aproj_kernel, wi_scale=wi_scale),
        grid=(t // tm,),
        in_specs=[row(d), _resident((1, d)), _resident((d, A_COLS)), tab, tab],
        out_specs=[col(A_Q), row(A_KD),
                   pl.BlockSpec((tm // tk, A_VT, tk), lambda i: (i, 0, 0)),
                   col(A_QI), row(LANES), col(IDX_HEADS)],
        out_shape=[jax.ShapeDtypeStruct((A_Q, t), BF16),
                   jax.ShapeDtypeStruct((t, A_KD), BF16),
                   jax.ShapeDtypeStruct((t // tk, A_VT, tk), BF16),
                   jax.ShapeDtypeStruct((A_QI, t), BF16),
                   jax.ShapeDtypeStruct((t, LANES), BF16),
                   jax.ShapeDtypeStruct((IDX_HEADS, t), F32)],
        compiler_params=_params("parallel"),
        name="a_proj",
    )(h, g, w, cos, sin)


def _dsa_kernel(qt_ref, qit_ref, wit_ref, ki_ref, k_ref, vt_ref, o_ref,
                key_sc, qiw_sc, qw_sc, sa_sc, sb_sc, mxa_sc, mxb_sc, m_sc, acc_sc,
                *, n_sel, idx_bits):
    tq = qt_ref.shape[1]
    tk = tq
    j = pl.program_id(1)
    nkb = j + 1
    qpos = j * tq + lax.broadcasted_iota(I32, (1, tq), 1)
    limit = (qpos // CHUNK + 1) * CHUNK
    kiota = lax.broadcasted_iota(I32, (tk, tq), 0)
    zeros_half = jnp.zeros((HEAD_DIM, tq), BF16)

    for h in range(IDX_HEADS):
        qiw_sc[h, 0:HEAD_DIM, :] = qit_ref[h * HEAD_DIM:(h + 1) * HEAD_DIM, :]
        qiw_sc[h, HEAD_DIM:LANES, :] = zeros_half
    for h in range(A_HEADS):
        qw_sc[h, 0:HEAD_DIM, :] = qt_ref[h * HEAD_DIM:(h + 1) * HEAD_DIM, :]
        qw_sc[h, HEAD_DIM:LANES, :] = zeros_half

    def score_body(c, carry):
        kc = ki_ref[pl.ds(pl.multiple_of(c * tk, tk), tk), :]
        sc = jnp.zeros((tk, tq), F32)
        for h in range(IDX_HEADS):
            lg = jnp.dot(kc, qiw_sc[h], preferred_element_type=F32)
            sc = sc + jnp.maximum(lg, 0.0) * wit_ref[h:h + 1, :]
        sc = jnp.where(sc == 0.0, 0.0, sc)
        bits = pltpu.bitcast(sc, I32)
        key = bits ^ ((bits >> 31) & 0x7FFFFFFF)
        key_sc[c] = jnp.where(c * tk + kiota < limit, key, INT_MIN)
        return carry

    lax.fori_loop(0, nkb, score_body, 0)

    @pl.when(nkb % 2 == 1)
    def _():
        key_sc[nkb] = jnp.full((tk, tq), INT_MIN, I32)

    def count(pred):
        def body(cc, acc):
            for u in range(2):
                c = 2 * cc + u
                acc = acc + _colsum8(jnp.where(pred(c, key_sc[c]), 1.0, 0.0))
            return acc
        acc = lax.fori_loop(0, (nkb + 1) // 2, body, jnp.zeros((SUBLANES, tq), F32))
        return jnp.sum(acc, axis=0, keepdims=True)

    def try_threshold(cand, thr, cnt_thr):
        cnt = count(lambda c, key: key >= cand)
        ok = cnt >= n_sel
        return jnp.where(ok, cand, thr), jnp.where(ok, cnt, cnt_thr)

    thr0 = jnp.full((1, tq), INT_MIN, I32)
    thr, cnt_thr = try_threshold(jnp.zeros((1, tq), I32), thr0, jnp.zeros((1, tq), F32))

    def bit_body(i, carry):
        thr, cnt_thr = carry
        cand = thr | jnp.left_shift(jnp.int32(1), 30 - i)
        return try_threshold(cand, thr, cnt_thr)

    thr, cnt_thr = lax.fori_loop(0, 31, bit_body, (thr, cnt_thr))
    thr = jnp.maximum(thr, INT_MIN + 1)

    @pl.when(jnp.max(cnt_thr) > n_sel)
    def _():
        need = n_sel - count(lambda c, key: key > thr)
        pos = jnp.zeros((1, tq), I32)
        for b in range(idx_bits - 1, -1, -1):
            cand = pos | (1 << b)
            cnt = count(lambda c, key: jnp.where(key == thr, c * tk + kiota, 2 ** 30) < cand)
            pos = jnp.where(cnt < need, cand, pos)

        def demote(c, carry):
            key = key_sc[c]
            tie_pos = jnp.where(key == thr, c * tk + kiota, -1)
            key_sc[c] = jnp.where(tie_pos > pos, INT_MIN, key)
            return carry

        lax.fori_loop(0, nkb, demote, 0)

    m_sc[...] = jnp.full(m_sc.shape, MASKED, F32)
    acc_sc[...] = jnp.zeros(acc_sc.shape, F32)

    bufs = ((sa_sc, mxa_sc), (sb_sc, mxb_sc))

    def select_bias(c):
        return jnp.where(key_sc[c] >= thr, 0.0, MASKED)

    def scores(c, h, buf, bias):
        n = h // A_GROUP
        r0 = pl.multiple_of(c * tk, tk)
        s = jnp.dot(k_ref[pl.ds(r0, tk), n * LANES:(n + 1) * LANES], qw_sc[h],
                    preferred_element_type=F32) + bias
        buf[0][h] = s
        buf[1][h] = _colmax8(s)

    def update(c, h, buf):
        n = h // A_GROUP
        _softmax_pv(buf[0], buf[1], vt_ref[c, n * V_ROWS:(n + 1) * V_ROWS, :], h, m_sc, acc_sc)

    def by_parity(c, fn):
        for par in range(2):
            @pl.when(c % 2 == par)
            def _():
                fn(bufs[par], bufs[1 - par])

    bias0 = select_bias(0)
    for h in range(A_HEADS):
        scores(0, h, bufs[0], bias0)

    def attn_body(c, carry):
        def fn(cur, nxt):
            bias = select_bias(c + 1)
            for h in range(A_HEADS):
                scores(c + 1, h, nxt, bias)
                update(c, h, cur)
        by_parity(c, fn)
        return carry

    lax.fori_loop(0, j, attn_body, 0)
    by_parity(j, lambda cur, nxt: [update(j, h, cur) for h in range(A_HEADS)])

    for i in range(A_HEADS // 2):
        pair = [_normalised(acc_sc, h) for h in (2 * i, 2 * i + 1)]
        o_ref[:, i * LANES:(i + 1) * LANES] = jnp.concatenate(pair, axis=0).T.astype(o_ref.dtype)


def _dsa(qt, qit, wit, ki, k, vt, b, s, tq):
    t = b * s
    nq = s // tq
    n_sel = min(TOPK_MAX, s // 4)
    idx_bits = max(1, (s - 1).bit_length())
    qcol = lambda n: pl.BlockSpec((n, tq), lambda bi, j: (0, bi * nq + j))
    full = lambda n: pl.BlockSpec((s, n), lambda bi, j: (bi, 0))
    return pl.pallas_call(
        functools.partial(_dsa_kernel, n_sel=float(n_sel), idx_bits=idx_bits),
        grid=(b, nq),
        in_specs=[qcol(A_Q), qcol(A_QI), qcol(IDX_HEADS), full(LANES), full(A_KD),
                  pl.BlockSpec((nq, A_VT, tq), lambda bi, j: (bi, 0, 0))],
        out_specs=pl.BlockSpec((tq, A_Q), lambda bi, j: (bi * nq + j, 0)),
        out_shape=jax.ShapeDtypeStruct((t, A_Q), BF16),
        scratch_shapes=[
            pltpu.VMEM((nq + nq % 2, tq, tq), I32),
            pltpu.VMEM((IDX_HEADS, LANES, tq), BF16),
            pltpu.VMEM((A_HEADS, LANES, tq), BF16),
            pltpu.VMEM((A_HEADS, tq, tq), F32),
            pltpu.VMEM((A_HEADS, tq, tq), F32),
            pltpu.VMEM((A_HEADS, SUBLANES, tq), F32),
            pltpu.VMEM((A_HEADS, SUBLANES, tq), F32),
            pltpu.VMEM((A_HEADS, 1, tq), F32),
            pltpu.VMEM((A_HEADS, V_ROWS, tq), F32),
        ],
        compiler_params=_params("parallel", "arbitrary"),
        name="dsa_attention",
    )(qt, qit, wit, ki, k, vt)


B_QK = B_HEADS * HEAD_DIM
B_KAUG = B_HEADS * LANES
B_VT = B_HEADS * V_ROWS
BIAS_PIECES = 3


def _split3(x):
    hi = x.astype(BF16)
    r1 = x - hi.astype(F32)
    mid = r1.astype(BF16)
    low = (r1 - mid.astype(F32)).astype(BF16)
    return hi, mid, low


def _kvproj_kernel(h_ref, g_ref, wk_ref, wv_ref, wf_ref, bf_ref, place_ref, ones_ref,
                   kaug_ref, vt_ref, chi_ref, cmid_ref, clo_ref, carry_sc):
    tm = h_ref.shape[0]
    tk = vt_ref.shape[2]

    @pl.when(pl.program_id(1) == 0)
    def _():
        carry_sc[...] = jnp.zeros(carry_sc.shape, F32)

    xn = _rms(h_ref[...], g_ref[...]).astype(BF16)
    z = jnp.dot(xn, wf_ref[...], preferred_element_type=F32) + bf_ref[...]
    logf = jnp.minimum(z, 0.0) - jnp.log1p(jnp.exp(-jnp.abs(z)))
    tri = (lax.broadcasted_iota(I32, (tm, tm), 0) >=
           lax.broadcasted_iota(I32, (tm, tm), 1)).astype(BF16)
    cs = sum(jnp.dot(tri, piece, preferred_element_type=F32) for piece in _split3(logf))
    cum = cs + carry_sc[...]
    carry_sc[...] = cum[tm - 1:tm, :]
    hi, mid, low = _split3(cum * LOG2E)
    chi_ref[...] = hi
    cmid_ref[...] = mid
    clo_ref[...] = low
    pieces = jnp.concatenate([hi, mid, low], axis=1)

    step = 4 * LANES
    for c0 in range(0, B_KAUG, step):
        y = (jnp.dot(xn, wk_ref[:, c0:c0 + step], preferred_element_type=F32) +
             jnp.dot(pieces, place_ref[:, c0:c0 + step], preferred_element_type=F32) +
             ones_ref[:, c0:c0 + step])
        kaug_ref[:, c0:c0 + step] = y.astype(BF16)
    for c0 in range(0, B_QK, step):
        y = jnp.dot(xn, wv_ref[:, c0:c0 + step], preferred_element_type=F32)
        for jj in range(step // LANES):
            head = (c0 + jj * LANES) // HEAD_DIM
            for sb in range(tm // tk):
                yt = y[sb * tk:(sb + 1) * tk, jj * LANES:(jj + 1) * LANES].T.astype(BF16)
                for e in range(2):
                    _store_value_head(vt_ref, sb, head + e, yt[e * HEAD_DIM:(e + 1) * HEAD_DIM, :])


def _kvproj(h3, g, wk, wv, wf, bf, place, ones, tm, tk):
    b, s, d = h3.shape
    ns = s // tm
    blk = lambda n: pl.BlockSpec((None, tm, n), lambda bi, i: (bi, i, 0))
    return pl.pallas_call(
        _kvproj_kernel,
        grid=(b, ns),
        in_specs=[blk(d), _resident((1, d)), _resident(wk.shape), _resident(wv.shape),
                  _resident(wf.shape), _resident((1, LANES)), _resident(place.shape),
                  _resident(ones.shape)],
        out_specs=[blk(B_KAUG),
                   pl.BlockSpec((tm // tk, B_VT, tk), lambda bi, i: (bi * ns + i, 0, 0)),
                   blk(LANES), blk(LANES), blk(LANES)],
        out_shape=[jax.ShapeDtypeStruct((b, s, B_KAUG), BF16),
                   jax.ShapeDtypeStruct((b * s // tk, B_VT, tk), BF16),
                   jax.ShapeDtypeStruct((b, s, LANES), BF16),
                   jax.ShapeDtypeStruct((b, s, LANES), BF16),
                   jax.ShapeDtypeStruct((b, s, LANES), BF16)],
        scratch_shapes=[pltpu.VMEM((1, LANES), F32)],
        compiler_params=_params("parallel", "arbitrary"),
        name="kv_proj_cumsum",
    )(h3, g, wk, wv, wf, bf, place, ones)


def _qproj_kernel(h_ref, g_ref, w_ref, rows_ref, qt_ref):
    tm = h_ref.shape[0]
    xn = _rms(h_ref[...], g_ref[...]).astype(BF16)
    pad = jnp.zeros((LANES - HEAD_DIM - BF16_ROWS, tm), BF16)
    step = 4 * LANES
    for c0 in range(0, B_QK, step):
        y = jnp.dot(xn, w_ref[:, c0:c0 + step], preferred_element_type=F32) * QK_SCALE
        for jj in range(step // LANES):
            yt = y[:, jj * LANES:(jj + 1) * LANES].T.astype(BF16)
            for e in range(2):
                h = (c0 + jj * LANES) // HEAD_DIM + e
                r = h * LANES
                qt_ref[r:r + HEAD_DIM, :] = yt[e * HEAD_DIM:(e + 1) * HEAD_DIM, :]
                qt_ref[r + HEAD_DIM:r + HEAD_DIM + BF16_ROWS, :] = (
                    rows_ref[h * BF16_ROWS:(h + 1) * BF16_ROWS, :])
                qt_ref[r + HEAD_DIM + BF16_ROWS:r + LANES, :] = pad


def _qproj(h, g, w, rows, tm):
    t, d = h.shape
    return pl.pallas_call(
        _qproj_kernel,
        grid=(t // tm,),
        in_specs=[pl.BlockSpec((tm, d), lambda i: (i, 0)), _resident((1, d)), _resident(w.shape),
                  pl.BlockSpec((B_HEADS * BF16_ROWS, tm), lambda i: (0, i))],
        out_specs=pl.BlockSpec((B_KAUG, tm), lambda i: (0, i)),
        out_shape=jax.ShapeDtypeStruct((B_KAUG, t), BF16),
        compiler_params=_params("parallel"),
        name="q_proj",
    )(h, g, w, rows)


FOX_HEADS_PER_STEP = 8


def _fox_kernel(qt_ref, k_ref, vt_ref, o_ref, sa_sc, sb_sc, mxa_sc, mxb_sc, m_sc, acc_sc, *, tq):
    tk = tq
    hps = FOX_HEADS_PER_STEP
    jj = pl.program_id(2)
    m_sc[...] = jnp.full(m_sc.shape, MASKED, F32)
    acc_sc[...] = jnp.zeros(acc_sc.shape, F32)
    kiota = lax.broadcasted_iota(I32, (tk, tq), 0)
    qiota = lax.broadcasted_iota(I32, (1, tq), 1)
    buf_a = (sa_sc, mxa_sc)
    buf_b = (sb_sc, mxb_sc)

    def scores(c, sub, h, buf, qlimit):
        r0 = pl.multiple_of(c * tk, tk)
        s = jnp.dot(k_ref[pl.ds(r0, tk), h * LANES:(h + 1) * LANES],
                    qt_ref[h * LANES:(h + 1) * LANES, sub * tq:(sub + 1) * tq],
                    preferred_element_type=F32)
        if qlimit is not None:
            s = jnp.where(kiota <= qlimit, s, MASKED)
        buf[0][sub * hps + h] = s
        buf[1][sub * hps + h] = _colmax8(s)

    def update(c, sub, h, buf):
        _softmax_pv(buf[0], buf[1], vt_ref[c, h * V_ROWS:(h + 1) * V_ROWS, :],
                    sub * hps + h, m_sc, acc_sc)

    def stage(c, cur, nxt, limits, update_subs=(0, 1)):
        for h in range(hps):
            for sub, qlimit in limits:
                scores(c + 1, sub, h, nxt, qlimit)
            for sub in update_subs:
                update(c, sub, h, cur)

    first_limit = qiota + jnp.where(jj == 0, 0, tk)
    for h in range(hps):
        scores(0, 0, h, buf_a, first_limit)
        scores(0, 1, h, buf_a, None)

    def main_body(i, carry):
        c = 2 * i
        stage(c, buf_a, buf_b, ((0, None), (1, None)))
        limit0 = qiota + jnp.where(i == jj - 1, 0, tk)
        stage(c + 1, buf_b, buf_a, ((0, limit0), (1, None)))
        return carry

    lax.fori_loop(0, jj, main_body, 0)
    last = 2 * jj
    stage(last, buf_a, buf_b, ((1, qiota),))
    stage(last + 1, buf_b, None, (), update_subs=(1,))
    for sub in range(2):
        outs = [_normalised(acc_sc, sub * hps + h) for h in range(hps)]
        o_ref[sub * tq:(sub + 1) * tq, :] = jnp.concatenate(outs, axis=0).T.astype(o_ref.dtype)


def _fox(qt, kaug, vt, b, s, tq):
    t = b * s
    nq = s // tq
    assert nq % 2 == 0, "fox_attention walks query blocks in pairs"
    hps = FOX_HEADS_PER_STEP
    nstate = 2 * hps
    return pl.pallas_call(
        functools.partial(_fox_kernel, tq=tq),
        grid=(b, B_HEADS // hps, nq // 2),
        in_specs=[
            pl.BlockSpec((hps * LANES, 2 * tq), lambda bi, p, j: (p, bi * (nq // 2) + j)),
            pl.BlockSpec((None, s, hps * LANES), lambda bi, p, j: (bi, 0, p)),
            pl.BlockSpec((nq, hps * V_ROWS, tq), lambda bi, p, j: (bi, p, 0)),
        ],
        out_specs=pl.BlockSpec((2 * tq, hps * HEAD_DIM), lambda bi, p, j: (bi * (nq // 2) + j, p)),
        out_shape=jax.ShapeDtypeStruct((t, B_QK), BF16),
        scratch_shapes=[
            pltpu.VMEM((nstate, tq, tq), F32),
            pltpu.VMEM((nstate, tq, tq), F32),
            pltpu.VMEM((nstate, SUBLANES, tq), F32),
            pltpu.VMEM((nstate, SUBLANES, tq), F32),
            pltpu.VMEM((nstate, 1, tq), F32),
            pltpu.VMEM((nstate, V_ROWS, tq), F32),
        ],
        compiler_params=_params("parallel", "parallel", "arbitrary"),
        name="fox_attention",
    )(qt, kaug, vt)


def _mlp_kernel(h_ref, o_ref, wo_ref, g_ref, wup_ref, wdn_ref, gf_ref, out_ref, *, tf, final):
    h1 = h_ref[...] + jnp.dot(o_ref[...], wo_ref[...], preferred_element_type=F32)
    xn = _rms(h1, g_ref[...]).astype(BF16)
    acc = h1
    for c0 in range(0, wup_ref.shape[1], tf):
        u = jnp.dot(xn, wup_ref[:, c0:c0 + tf], preferred_element_type=F32)
        a = jnp.square(jnp.maximum(u, 0.0)).astype(BF16)
        acc = acc + jnp.dot(a, wdn_ref[c0:c0 + tf, :], preferred_element_type=F32)
    if final:
        acc = _rms(acc, gf_ref[...])
    out_ref[...] = acc


def _mlp(h, o, wo, g, wup, wdn, gf, final, tm, tf):
    t, d = h.shape
    row = pl.BlockSpec((tm, d), lambda i: (i, 0))
    return pl.pallas_call(
        functools.partial(_mlp_kernel, tf=tf, final=final),
        grid=(t // tm,),
        in_specs=[row, row, _resident(wo.shape), _resident((1, d)), _resident(wup.shape),
                  _resident(wdn.shape), _resident((1, d))],
        out_specs=row,
        out_shape=jax.ShapeDtypeStruct((t, d), F32),
        compiler_params=_params("parallel"),
        name="oproj_mlp",
    )(h, o, wo, g, wup, wdn, gf)


def _dup_heads(w, heads):
    d = w.shape[0]
    w = w.reshape(d, heads, HEAD_DIM)
    return jnp.concatenate([w, w], axis=-1).reshape(d, heads * LANES)


def _prep_w_in_a(w):
    kv = A_KV_HEADS * HEAD_DIM
    o_k, o_v, o_qi = A_Q, A_Q + kv, A_Q + 2 * kv
    o_ki, o_wi = o_qi + A_QI, o_qi + A_QI + HEAD_DIM
    wi = w[:, o_wi:o_wi + IDX_HEADS]
    return jnp.concatenate([
        w[:, :A_Q],
        _dup_heads(w[:, o_k:o_v], A_KV_HEADS),
        w[:, o_v:o_qi],
        w[:, o_qi:o_ki],
        _dup_heads(w[:, o_ki:o_wi], 1),
        jnp.pad(wi, ((0, 0), (0, LANES - IDX_HEADS))),
    ], axis=1).astype(BF16)


def _rope_tables(seq):
    inv = 1.0 / (ROPE_THETA ** (jnp.arange(0, HEAD_DIM, 2, dtype=F32) / HEAD_DIM))
    ang = jnp.arange(seq, dtype=F32)[:, None] * inv[None, :]
    c, s = jnp.cos(ang), jnp.sin(ang)
    return jnp.concatenate([c, c, c, c], axis=1), jnp.concatenate([-s, s, -s, s], axis=1)


def _bias_placement():
    place = np.zeros((BIAS_PIECES * LANES, B_KAUG), np.float32)
    ones = np.zeros((1, B_KAUG), np.float32)
    for h in range(B_HEADS):
        for p in range(BIAS_PIECES):
            place[p * LANES + h, h * LANES + HEAD_DIM + p] = -1.0
            ones[0, h * LANES + HEAD_DIM + BIAS_PIECES + p] = 1.0
    return jnp.asarray(place, BF16), jnp.asarray(ones, F32)


def _query_bias_rows(chi, cmid, clo):
    b, s, _ = chi.shape
    t = b * s
    pcs = [x[:, :, :B_HEADS].reshape(t, B_HEADS).T for x in (chi, cmid, clo)]
    one = jnp.ones((B_HEADS, t), BF16)
    zero = jnp.zeros((B_HEADS, t), BF16)
    rows = [one] * BIAS_PIECES + pcs + [zero] * (BF16_ROWS - 2 * BIAS_PIECES)
    return jnp.stack(rows, axis=1).reshape(B_HEADS * BF16_ROWS, t)


def kernel(x, g_attn_a, w_in_a, w_o_a, g_kv, w_kv_b, b_f, g_attn_b, w_q_b, w_o_b,
           g_mlp, w_up, w_down, g_final):
    b, s, d = x.shape
    t = b * s
    depth = g_mlp.shape[0]
    n_a = g_attn_a.shape[0]
    tm = min(512, s)
    tq = min(ATTN_BLOCK, s)
    tf = 512

    cos, sin = _rope_tables(s)
    gf = g_final.reshape(1, d)
    h = x.reshape(t, d)
    kaug = vt_b = bias_rows = None

    for layer in range(depth):
        if layer < n_a:
            qt, k, vt, qit, ki, wit = _aproj(h, g_attn_a[layer].reshape(1, d),
                                             _prep_w_in_a(w_in_a[layer]), cos, sin, s, tm, tq)
            o = _dsa(qt, qit, wit, ki, k, vt, b, s, tq)
            w_o = w_o_a[layer]
        else:
            jb = layer - n_a
            if jb == 0:
                wk = jnp.pad(w_kv_b[:, :B_QK].reshape(d, B_HEADS, HEAD_DIM),
                             ((0, 0), (0, 0), (0, LANES - HEAD_DIM))).reshape(d, B_KAUG)
                wf = jnp.pad(w_kv_b[:, 2 * B_QK:], ((0, 0), (0, LANES - B_HEADS)))
                bias = jnp.pad(b_f, (0, LANES - B_HEADS)).reshape(1, LANES)
                place, ones = _bias_placement()
                kaug, vt_b, chi, cmid, clo = _kvproj(
                    h.reshape(b, s, d), g_kv.reshape(1, d), wk.astype(BF16),
                    w_kv_b[:, B_QK:2 * B_QK].astype(BF16), wf.astype(BF16), bias,
                    place, ones, tm, tq)
                bias_rows = _query_bias_rows(chi, cmid, clo)
            qt = _qproj(h, g_attn_b[jb].reshape(1, d), w_q_b[jb].astype(BF16), bias_rows, tm)
            o = _fox(qt, kaug, vt_b, b, s, tq)
            w_o = w_o_b[jb]
        h = _mlp(h, o, w_o.astype(BF16), g_mlp[layer].reshape(1, d),
                 w_up[layer].astype(BF16), w_down[layer].astype(BF16), gf,
                 layer == depth - 1, tm, tf)
    return h.reshape(b, s, d)
```

```python
import functools

import numpy as np
import jax
import jax.numpy as jnp
from jax import lax
from jax.experimental import pallas as pl
from jax.experimental.pallas import tpu as pltpu

LANES = 128
SUBLANES = 8
BF16_ROWS = 16
HEAD_DIM = 64
ROPE_HALF = HEAD_DIM // 2
CHUNK = 64
A_HEADS = 16
A_KV_HEADS = 4
A_GROUP = A_HEADS // A_KV_HEADS
IDX_HEADS = 8
TOPK_MAX = 256
B_HEADS = 16
ROPE_THETA = 10000.0
EPS = 1e-6
INT_MIN = -(2 ** 31)
MASKED = -1e30
LOG2E = 1.4426950408889634
QK_SCALE = HEAD_DIM ** -0.5 * LOG2E
VMEM_LIMIT_BYTES = 56 * 1024 * 1024
ATTN_BLOCK = 256

F32 = jnp.float32
BF16 = jnp.bfloat16
I32 = jnp.int32


def _params(*sem):
    return pltpu.CompilerParams(dimension_semantics=sem,
                                vmem_limit_bytes=VMEM_LIMIT_BYTES)


def _resident(shape):
    nd = len(shape)
    return pl.BlockSpec(shape, lambda *_: (0,) * nd, pipeline_mode=pl.Buffered(1))


def _rms(x, g):
    return x * lax.rsqrt(jnp.mean(x * x, axis=-1, keepdims=True) + EPS) * g


def _colsum8(x):
    n, w = x.shape
    return x.reshape(n // SUBLANES, SUBLANES, w).sum(axis=0)


def _colmax8(x):
    n, w = x.shape
    return x.reshape(n // SUBLANES, SUBLANES, w).max(axis=0)


V_ROWS = HEAD_DIM + BF16_ROWS


def _store_value_head(vt_ref, sb, head, yt):
    r = head * V_ROWS
    tk = yt.shape[1]
    vt_ref[sb, r:r + HEAD_DIM, :] = yt
    first_row = lax.broadcasted_iota(I32, (BF16_ROWS, tk), 0) == 0
    vt_ref[sb, r + HEAD_DIM:r + V_ROWS, :] = jnp.where(first_row, 1.0, 0.0).astype(BF16)


def _softmax_pv(s_buf, mx_buf, vt, h, m_sc, acc_sc):
    m_old = m_sc[h]
    m_new = jnp.maximum(m_old, jnp.max(mx_buf[h], axis=0, keepdims=True))
    alpha = jnp.exp2(m_old - m_new)
    p = jnp.exp2(s_buf[h] - m_new)
    acc_sc[h] = alpha * acc_sc[h] + jnp.dot(vt, p.astype(BF16), preferred_element_type=F32)
    m_sc[h] = m_new


def _normalised(acc_sc, h):
    acc = acc_sc[h]
    return acc[0:HEAD_DIM, :] / acc[HEAD_DIM:HEAD_DIM + 1, :]


A_Q = A_HEADS * HEAD_DIM
A_KD = A_KV_HEADS * LANES
A_V = A_KV_HEADS * HEAD_DIM
A_VT = A_KV_HEADS * V_ROWS
A_QI = IDX_HEADS * HEAD_DIM
A_OFF_K = A_Q
A_OFF_V = A_OFF_K + A_KD
A_OFF_QI = A_OFF_V + A_V
A_OFF_KI = A_OFF_QI + A_QI
A_OFF_WI = A_OFF_KI + LANES
A_COLS = A_OFF_WI + LANES


def _aproj_kernel(h_ref, g_ref, w_ref, cos_ref, sin_ref,
                  qt_ref, k_ref, vt_ref, qit_ref, ki_ref, wit_ref, *, wi_scale):
    tm = h_ref.shape[0]
    tk = vt_ref.shape[2]
    xn = _rms(h_ref[...], g_ref[...]).astype(BF16)
    cos = cos_ref[...]
    sin = sin_ref[...]
    lane = lax.broadcasted_iota(I32, (tm, LANES), 1)
    first_half = (lane & ROPE_HALF) == 0

    def rope(y):
        ys = jnp.where(first_half, pltpu.roll(y, LANES - ROPE_HALF, 1),
                       pltpu.roll(y, ROPE_HALF, 1))
        return y * cos + ys * sin

    def chunks(c0, n):
        step = 4 * LANES
        for s0 in range(0, n, step):
            sn = min(step, n - s0)
            y = jnp.dot(xn, w_ref[:, c0 + s0:c0 + s0 + sn], preferred_element_type=F32)
            for j in range(sn // LANES):
                yield (s0 // LANES + j), y[:, j * LANES:(j + 1) * LANES]

    for j, y in chunks(0, A_Q):
        qt_ref[j * LANES:(j + 1) * LANES, :] = (rope(y) * QK_SCALE).T.astype(BF16)
    for j, y in chunks(A_OFF_K, A_KD):
        k_ref[:, j * LANES:(j + 1) * LANES] = rope(y).astype(BF16)
    for j, y in chunks(A_OFF_V, A_V):
        for sb in range(tm // tk):
            yt = y[sb * tk:(sb + 1) * tk].T.astype(BF16)
            for e in range(2):
                _store_value_head(vt_ref, sb, 2 * j + e, yt[e * HEAD_DIM:(e + 1) * HEAD_DIM, :])
    for j, y in chunks(A_OFF_QI, A_QI):
        qit_ref[j * LANES:(j + 1) * LANES, :] = rope(y).T.astype(BF16)
    for j, y in chunks(A_OFF_KI, LANES):
        ki_ref[...] = rope(y).astype(BF16)
    for j, y in chunks(A_OFF_WI, LANES):
        wit_ref[...] = (y * wi_scale).T[:IDX_HEADS, :]


def _aproj(h, g, w, cos, sin, seq, tm, tk):
    t, d = h.shape
    npos = seq // tm
    row = lambda n: pl.BlockSpec((tm, n), lambda i: (i, 0))
    col = lambda n: pl.BlockSpec((n, tm), lambda i: (0, i))
    tab = pl.BlockSpec((tm, LANES), lambda i: (i % npos, 0))
    wi_scale = IDX_HEADS ** -0.5 * HEAD_DIM ** -0.5
    return pl.pallas_call(
        functools.partial(_aproj_kernel, wi_scale=wi_scale),
        grid=(t // tm,),
        in_specs=[row(d), _resident((1, d)), _resident((d, A_COLS)), tab, tab],
        out_specs=[col(A_Q), row(A_KD),
                   pl.BlockSpec((tm // tk, A_VT, tk), lambda i: (i, 0, 0)),
                   col(A_QI), row(LANES), col(IDX_HEADS)],
        out_shape=[jax.ShapeDtypeStruct((A_Q, t), BF16),
                   jax.ShapeDtypeStruct((t, A_KD), BF16),
                   jax.ShapeDtypeStruct((t // tk, A_VT, tk), BF16),
                   jax.ShapeDtypeStruct((A_QI, t), BF16),
                   jax.ShapeDtypeStruct((t, LANES), BF16),
                   jax.ShapeDtypeStruct((IDX_HEADS, t), F32)],
        compiler_params=_params("parallel"),
        name="a_proj",
    )(h, g, w, cos, sin)


def _dsa_kernel(qt_ref, qit_ref, wit_ref, ki_ref, k_ref, vt_ref, o_ref,
                key_sc, qiw_sc, qw_sc, sa_sc, sb_sc, mxa_sc, mxb_sc, m_sc, acc_sc,
                *, tq, n_sel, idx_bits):
    tk = tq
    jj = pl.program_id(1)
    kiota = lax.broadcasted_iota(I32, (tk, tq), 0)
    zeros_half = jnp.zeros((HEAD_DIM, tq), BF16)

    for sub in range(2):
        cols = slice(sub * tq, (sub + 1) * tq)
        for h in range(IDX_HEADS):
            qiw_sc[sub * IDX_HEADS + h, 0:HEAD_DIM, :] = qit_ref[h * HEAD_DIM:(h + 1) * HEAD_DIM, cols]
            qiw_sc[sub * IDX_HEADS + h, HEAD_DIM:LANES, :] = zeros_half
        for h in range(A_HEADS):
            qw_sc[sub * A_HEADS + h, 0:HEAD_DIM, :] = qt_ref[h * HEAD_DIM:(h + 1) * HEAD_DIM, cols]
            qw_sc[sub * A_HEADS + h, HEAD_DIM:LANES, :] = zeros_half

    def select(sub):
        j = 2 * jj + sub
        nkb = j + 1
        keys = key_sc.at[sub]
        qpos = j * tq + lax.broadcasted_iota(I32, (1, tq), 1)
        limit = (qpos // CHUNK + 1) * CHUNK

        def score_body(c, carry):
            kc = ki_ref[pl.ds(pl.multiple_of(c * tk, tk), tk), :]
            sc = jnp.zeros((tk, tq), F32)
            for h in range(IDX_HEADS):
                lg = jnp.dot(kc, qiw_sc[sub * IDX_HEADS + h], preferred_element_type=F32)
                sc = sc + jnp.maximum(lg, 0.0) * wit_ref[h:h + 1, sub * tq:(sub + 1) * tq]
            sc = jnp.where(sc == 0.0, 0.0, sc)
            bits = pltpu.bitcast(sc, I32)
            key = bits ^ ((bits >> 31) & 0x7FFFFFFF)
            keys[c] = jnp.where(c * tk + kiota < limit, key, INT_MIN)
            return carry

        lax.fori_loop(0, nkb, score_body, 0)

        @pl.when(nkb % 2 == 1)
        def _():
            keys[nkb] = jnp.full((tk, tq), INT_MIN, I32)

        def count(pred):
            def body(cc, acc):
                for u in range(2):
                    c = 2 * cc + u
                    acc = acc + _colsum8(jnp.where(pred(c, keys[c]), 1.0, 0.0))
                return acc
            acc = lax.fori_loop(0, (nkb + 1) // 2, body, jnp.zeros((SUBLANES, tq), F32))
            return jnp.sum(acc, axis=0, keepdims=True)

        def try_threshold(cand, thr, cnt_thr):
            cnt = count(lambda c, key: key >= cand)
            ok = cnt >= n_sel
            return jnp.where(ok, cand, thr), jnp.where(ok, cnt, cnt_thr)

        thr0 = jnp.full((1, tq), INT_MIN, I32)
        thr, cnt_thr = try_threshold(jnp.zeros((1, tq), I32), thr0, jnp.zeros((1, tq), F32))

        def bit_body(i, carry):
            thr, cnt_thr = carry
            cand = thr | jnp.left_shift(jnp.int32(1), 30 - i)
            return try_threshold(cand, thr, cnt_thr)

        thr, cnt_thr = lax.fori_loop(0, 31, bit_body, (thr, cnt_thr))
        thr = jnp.maximum(thr, INT_MIN + 1)

        @pl.when(jnp.max(cnt_thr) > n_sel)
        def _():
            need = n_sel - count(lambda c, key: key > thr)
            pos = jnp.zeros((1, tq), I32)
            for b in range(idx_bits - 1, -1, -1):
                cand = pos | (1 << b)
                cnt = count(lambda c, key: jnp.where(key == thr, c * tk + kiota, 2 ** 30) < cand)
                pos = jnp.where(cnt < need, cand, pos)

            def demote(c, carry):
                key = keys[c]
                tie_pos = jnp.where(key == thr, c * tk + kiota, -1)
                keys[c] = jnp.where(tie_pos > pos, INT_MIN, key)
                return carry

            lax.fori_loop(0, nkb, demote, 0)

        return thr

    thrs = [select(0), select(1)]

    m_sc[...] = jnp.full(m_sc.shape, MASKED, F32)
    acc_sc[...] = jnp.zeros(acc_sc.shape, F32)
    buf_a = (sa_sc, mxa_sc)
    buf_b = (sb_sc, mxb_sc)

    def select_bias(c, sub):
        return jnp.where(key_sc[sub, c] >= thrs[sub], 0.0, MASKED)

    def scores(c, sub, h, buf, bias):
        n = h // A_GROUP
        r0 = pl.multiple_of(c * tk, tk)
        s = jnp.dot(k_ref[pl.ds(r0, tk), n * LANES:(n + 1) * LANES], qw_sc[sub * A_HEADS + h],
                    preferred_element_type=F32) + bias
        buf[0][sub * A_HEADS + h] = s
        buf[1][sub * A_HEADS + h] = _colmax8(s)

    def update(c, sub, h, buf):
        n = h // A_GROUP
        _softmax_pv(buf[0], buf[1], vt_ref[c, n * V_ROWS:(n + 1) * V_ROWS, :],
                    sub * A_HEADS + h, m_sc, acc_sc)

    def stage(c, cur, nxt, score_subs, update_subs=(0, 1)):
        biases = {sub: select_bias(c + 1, sub) for sub in score_subs}
        for h in range(A_HEADS):
            for sub in score_subs:
                scores(c + 1, sub, h, nxt, biases[sub])
            for sub in update_subs:
                update(c, sub, h, cur)

    for sub in range(2):
        bias0 = select_bias(0, sub)
        for h in range(A_HEADS):
            scores(0, sub, h, buf_a, bias0)

    def attn_body(i, carry):
        stage(2 * i, buf_a, buf_b, (0, 1))
        stage(2 * i + 1, buf_b, buf_a, (0, 1))
        return carry

    lax.fori_loop(0, jj, attn_body, 0)
    last = 2 * jj
    stage(last, buf_a, buf_b, (1,))
    stage(last + 1, buf_b, None, (), update_subs=(1,))

    for sub in range(2):
        for i in range(A_HEADS // 2):
            pair = [_normalised(acc_sc, sub * A_HEADS + h) for h in (2 * i, 2 * i + 1)]
            o_ref[sub * tq:(sub + 1) * tq, i * LANES:(i + 1) * LANES] = (
                jnp.concatenate(pair, axis=0).T.astype(o_ref.dtype))


def _dsa(qt, qit, wit, ki, k, vt, b, s, tq):
    t = b * s
    nq = s // tq
    assert nq % 2 == 0, "dsa_attention walks query blocks in pairs"
    n_sel = min(TOPK_MAX, s // 4)
    idx_bits = max(1, (s - 1).bit_length())
    qcol = lambda n: pl.BlockSpec((n, 2 * tq), lambda bi, j: (0, bi * (nq // 2) + j))
    full = lambda n: pl.BlockSpec((s, n), lambda bi, j: (bi, 0), pipeline_mode=pl.Buffered(1))
    return pl.pallas_call(
        functools.partial(_dsa_kernel, tq=tq, n_sel=float(n_sel), idx_bits=idx_bits),
        grid=(b, nq // 2),
        in_specs=[qcol(A_Q), qcol(A_QI), qcol(IDX_HEADS), full(LANES), full(A_KD),
                  pl.BlockSpec((nq, A_VT, tq), lambda bi, j: (bi, 0, 0),
                               pipeline_mode=pl.Buffered(1))],
        out_specs=pl.BlockSpec((2 * tq, A_Q), lambda bi, j: (bi * (nq // 2) + j, 0)),
        out_shape=jax.ShapeDtypeStruct((t, A_Q), BF16),
        scratch_shapes=[
            pltpu.VMEM((2, nq, tq, tq), I32),
            pltpu.VMEM((2 * IDX_HEADS, LANES, tq), BF16),
            pltpu.VMEM((2 * A_HEADS, LANES, tq), BF16),
            pltpu.VMEM((2 * A_HEADS, tq, tq), F32),
            pltpu.VMEM((2 * A_HEADS, tq, tq), F32),
            pltpu.VMEM((2 * A_HEADS, SUBLANES, tq), F32),
            pltpu.VMEM((2 * A_HEADS, SUBLANES, tq), F32),
            pltpu.VMEM((2 * A_HEADS, 1, tq), F32),
            pltpu.VMEM((2 * A_HEADS, V_ROWS, tq), F32),
        ],
        compiler_params=_params("parallel", "arbitrary"),
        name="dsa_attention",
    )(qt, qit, wit, ki, k, vt)


B_QK = B_HEADS * HEAD_DIM
B_KAUG = B_HEADS * LANES
B_VT = B_HEADS * V_ROWS
BIAS_PIECES = 3


def _split3(x):
    hi = x.astype(BF16)
    r1 = x - hi.astype(F32)
    mid = r1.astype(BF16)
    low = (r1 - mid.astype(F32)).astype(BF16)
    return hi, mid, low


def _kvproj_kernel(h_ref, g_ref, wk_ref, wv_ref, wf_ref, bf_ref, place_ref, ones_ref,
                   kaug_ref, vt_ref, chi_ref, cmid_ref, clo_ref, carry_sc):
    tm = h_ref.shape[0]
    tk = vt_ref.shape[2]

    @pl.when(pl.program_id(1) == 0)
    def _():
        carry_sc[...] = jnp.zeros(carry_sc.shape, F32)

    xn = _rms(h_ref[...], g_ref[...]).astype(BF16)
    z = jnp.dot(xn, wf_ref[...], preferred_element_type=F32) + bf_ref[...]
    logf = jnp.minimum(z, 0.0) - jnp.log1p(jnp.exp(-jnp.abs(z)))
    tri = (lax.broadcasted_iota(I32, (tm, tm), 0) >=
           lax.broadcasted_iota(I32, (tm, tm), 1)).astype(BF16)
    cs = sum(jnp.dot(tri, piece, preferred_element_type=F32) for piece in _split3(logf))
    cum = cs + carry_sc[...]
    carry_sc[...] = cum[tm - 1:tm, :]
    hi, mid, low = _split3(cum * LOG2E)
    chi_ref[...] = hi
    cmid_ref[...] = mid
    clo_ref[...] = low
    pieces = jnp.concatenate([hi, mid, low], axis=1)

    step = 4 * LANES
    for c0 in range(0, B_KAUG, step):
        y = (jnp.dot(xn, wk_ref[:, c0:c0 + step], preferred_element_type=F32) +
             jnp.dot(pieces, place_ref[:, c0:c0 + step], preferred_element_type=F32) +
             ones_ref[:, c0:c0 + step])
        kaug_ref[:, c0:c0 + step] = y.astype(BF16)
    for c0 in range(0, B_QK, step):
        y = jnp.dot(xn, wv_ref[:, c0:c0 + step], preferred_element_type=F32)
        for jj in range(step // LANES):
            head = (c0 + jj * LANES) // HEAD_DIM
            for sb in range(tm // tk):
                yt = y[sb * tk:(sb + 1) * tk, jj * LANES:(jj + 1) * LANES].T.astype(BF16)
                for e in range(2):
                    _store_value_head(vt_ref, sb, head + e, yt[e * HEAD_DIM:(e + 1) * HEAD_DIM, :])


def _kvproj(h3, g, wk, wv, wf, bf, place, ones, tm, tk):
    b, s, d = h3.shape
    ns = s // tm
    blk = lambda n: pl.BlockSpec((None, tm, n), lambda bi, i: (bi, i, 0))
    return pl.pallas_call(
        _kvproj_kernel,
        grid=(b, ns),
        in_specs=[blk(d), _resident((1, d)), _resident(wk.shape), _resident(wv.shape),
                  _resident(wf.shape), _resident((1, LANES)), _resident(place.shape),
                  _resident(ones.shape)],
        out_specs=[blk(B_KAUG),
                   pl.BlockSpec((tm // tk, B_VT, tk), lambda bi, i: (bi * ns + i, 0, 0)),
                   blk(LANES), blk(LANES), blk(LANES)],
        out_shape=[jax.ShapeDtypeStruct((b, s, B_KAUG), BF16),
                   jax.ShapeDtypeStruct((b * s // tk, B_VT, tk), BF16),
                   jax.ShapeDtypeStruct((b, s, LANES), BF16),
                   jax.ShapeDtypeStruct((b, s, LANES), BF16),
                   jax.ShapeDtypeStruct((b, s, LANES), BF16)],
        scratch_shapes=[pltpu.VMEM((1, LANES), F32)],
        compiler_params=_params("parallel", "arbitrary"),
        name="kv_proj_cumsum",
    )(h3, g, wk, wv, wf, bf, place, ones)


def _qproj_kernel(h_ref, g_ref, w_ref, rows_ref, qt_ref):
    tm = h_ref.shape[0]
    xn = _rms(h_ref[...], g_ref[...]).astype(BF16)
    pad = jnp.zeros((LANES - HEAD_DIM - BF16_ROWS, tm), BF16)
    step = 4 * LANES
    for c0 in range(0, B_QK, step):
        y = jnp.dot(xn, w_ref[:, c0:c0 + step], preferred_element_type=F32) * QK_SCALE
        for jj in range(step // LANES):
            yt = y[:, jj * LANES:(jj + 1) * LANES].T.astype(BF16)
            for e in range(2):
                h = (c0 + jj * LANES) // HEAD_DIM + e
                r = h * LANES
                qt_ref[r:r + HEAD_DIM, :] = yt[e * HEAD_DIM:(e + 1) * HEAD_DIM, :]
                qt_ref[r + HEAD_DIM:r + HEAD_DIM + BF16_ROWS, :] = (
                    rows_ref[h * BF16_ROWS:(h + 1) * BF16_ROWS, :])
                qt_ref[r + HEAD_DIM + BF16_ROWS:r + LANES, :] = pad


def _qproj(h, g, w, rows, tm):
    t, d = h.shape
    return pl.pallas_call(
        _qproj_kernel,
        grid=(t // tm,),
        in_specs=[pl.BlockSpec((tm, d), lambda i: (i, 0)), _resident((1, d)), _resident(w.shape),
                  pl.BlockSpec((B_HEADS * BF16_ROWS, tm), lambda i: (0, i))],
        out_specs=pl.BlockSpec((B_KAUG, tm), lambda i: (0, i)),
        out_shape=jax.ShapeDtypeStruct((B_KAUG, t), BF16),
        compiler_params=_params("parallel"),
        name="q_proj",
    )(h, g, w, rows)


FOX_HEADS_PER_STEP = 8


def _fox_kernel(qt_ref, k_ref, vt_ref, o_ref, sa_sc, sb_sc, mxa_sc, mxb_sc, m_sc, acc_sc, *, tq):
    tk = tq
    hps = FOX_HEADS_PER_STEP
    jj = pl.program_id(2)
    m_sc[...] = jnp.full(m_sc.shape, MASKED, F32)
    acc_sc[...] = jnp.zeros(acc_sc.shape, F32)
    kiota = lax.broadcasted_iota(I32, (tk, tq), 0)
    qiota = lax.broadcasted_iota(I32, (1, tq), 1)
    buf_a = (sa_sc, mxa_sc)
    buf_b = (sb_sc, mxb_sc)

    def scores(c, sub, h, buf, qlimit):
        r0 = pl.multiple_of(c * tk, tk)
        s = jnp.dot(k_ref[pl.ds(r0, tk), h * LANES:(h + 1) * LANES],
                    qt_ref[h * LANES:(h + 1) * LANES, sub * tq:(sub + 1) * tq],
                    preferred_element_type=F32)
        if qlimit is not None:
            s = jnp.where(kiota <= qlimit, s, MASKED)
        buf[0][sub * hps + h] = s
        buf[1][sub * hps + h] = _colmax8(s)

    def update(c, sub, h, buf):
        _softmax_pv(buf[0], buf[1], vt_ref[c, h * V_ROWS:(h + 1) * V_ROWS, :],
                    sub * hps + h, m_sc, acc_sc)

    def stage(c, cur, nxt, limits, update_subs=(0, 1)):
        for h in range(hps):
            for sub, qlimit in limits:
                scores(c + 1, sub, h, nxt, qlimit)
            for sub in update_subs:
                update(c, sub, h, cur)

    first_limit = qiota + jnp.where(jj == 0, 0, tk)
    for h in range(hps):
        scores(0, 0, h, buf_a, first_limit)
        scores(0, 1, h, buf_a, None)

    def main_body(i, carry):
        c = 2 * i
        stage(c, buf_a, buf_b, ((0, None), (1, None)))
        limit0 = qiota + jnp.where(i == jj - 1, 0, tk)
        stage(c + 1, buf_b, buf_a, ((0, limit0), (1, None)))
        return carry

    lax.fori_loop(0, jj, main_body, 0)
    last = 2 * jj
    stage(last, buf_a, buf_b, ((1, qiota),))
    stage(last + 1, buf_b, None, (), update_subs=(1,))
    for sub in range(2):
        outs = [_normalised(acc_sc, sub * hps + h) for h in range(hps)]
        o_ref[sub * tq:(sub + 1) * tq, :] = jnp.concatenate(outs, axis=0).T.astype(o_ref.dtype)


def _fox(qt, kaug, vt, b, s, tq):
    t = b * s
    nq = s // tq
    assert nq % 2 == 0, "fox_attention walks query blocks in pairs"
    hps = FOX_HEADS_PER_STEP
    nstate = 2 * hps
    return pl.pallas_call(
        functools.partial(_fox_kernel, tq=tq),
        grid=(b, B_HEADS // hps, nq // 2),
        in_specs=[
            pl.BlockSpec((hps * LANES, 2 * tq), lambda bi, p, j: (p, bi * (nq // 2) + j)),
            pl.BlockSpec((None, s, hps * LANES), lambda bi, p, j: (bi, 0, p)),
            pl.BlockSpec((nq, hps * V_ROWS, tq), lambda bi, p, j: (bi, p, 0)),
        ],
        out_specs=pl.BlockSpec((2 * tq, hps * HEAD_DIM), lambda bi, p, j: (bi * (nq // 2) + j, p)),
        out_shape=jax.ShapeDtypeStruct((t, B_QK), BF16),
        scratch_shapes=[
            pltpu.VMEM((nstate, tq, tq), F32),
            pltpu.VMEM((nstate, tq, tq), F32),
            pltpu.VMEM((nstate, SUBLANES, tq), F32),
            pltpu.VMEM((nstate, SUBLANES, tq), F32),
            pltpu.VMEM((nstate, 1, tq), F32),
            pltpu.VMEM((nstate, V_ROWS, tq), F32),
        ],
        compiler_params=_params("parallel", "parallel", "arbitrary"),
        name="fox_attention",
    )(qt, kaug, vt)


def _mlp_kernel(h_ref, o_ref, wo_ref, g_ref, wup_ref, wdn_ref, gf_ref, out_ref, *, tf, final):
    h1 = h_ref[...] + jnp.dot(o_ref[...], wo_ref[...], preferred_element_type=F32)
    xn = _rms(h1, g_ref[...]).astype(BF16)
    acc = h1
    for c0 in range(0, wup_ref.shape[1], tf):
        u = jnp.dot(xn, wup_ref[:, c0:c0 + tf], preferred_element_type=F32)
        a = jnp.square(jnp.maximum(u, 0.0)).astype(BF16)
        acc = acc + jnp.dot(a, wdn_ref[c0:c0 + tf, :], preferred_element_type=F32)
    if final:
        acc = _rms(acc, gf_ref[...])
    out_ref[...] = acc


def _mlp(h, o, wo, g, wup, wdn, gf, final, tm, tf):
    t, d = h.shape
    row = pl.BlockSpec((tm, d), lambda i: (i, 0))
    return pl.pallas_call(
        functools.partial(_mlp_kernel, tf=tf, final=final),
        grid=(t // tm,),
        in_specs=[row, row, _resident(wo.shape), _resident((1, d)), _resident(wup.shape),
                  _resident(wdn.shape), _resident((1, d))],
        out_specs=row,
        out_shape=jax.ShapeDtypeStruct((t, d), F32),
        compiler_params=_params("parallel"),
        name="oproj_mlp",
    )(h, o, wo, g, wup, wdn, gf)


def _dup_heads(w, heads):
    d = w.shape[0]
    w = w.reshape(d, heads, HEAD_DIM)
    return jnp.concatenate([w, w], axis=-1).reshape(d, heads * LANES)


def _prep_w_in_a(w):
    kv = A_KV_HEADS * HEAD_DIM
    o_k, o_v, o_qi = A_Q, A_Q + kv, A_Q + 2 * kv
    o_ki, o_wi = o_qi + A_QI, o_qi + A_QI + HEAD_DIM
    wi = w[:, o_wi:o_wi + IDX_HEADS]
    return jnp.concatenate([
        w[:, :A_Q],
        _dup_heads(w[:, o_k:o_v], A_KV_HEADS),
        w[:, o_v:o_qi],
        w[:, o_qi:o_ki],
        _dup_heads(w[:, o_ki:o_wi], 1),
        jnp.pad(wi, ((0, 0), (0, LANES - IDX_HEADS))),
    ], axis=1).astype(BF16)


def _rope_tables(seq):
    inv = 1.0 / (ROPE_THETA ** (jnp.arange(0, HEAD_DIM, 2, dtype=F32) / HEAD_DIM))
    ang = jnp.arange(seq, dtype=F32)[:, None] * inv[None, :]
    c, s = jnp.cos(ang), jnp.sin(ang)
    return jnp.concatenate([c, c, c, c], axis=1), jnp.concatenate([-s, s, -s, s], axis=1)


def _bias_placement():
    place = np.zeros((BIAS_PIECES * LANES, B_KAUG), np.float32)
    ones = np.zeros((1, B_KAUG), np.float32)
    for h in range(B_HEADS):
        for p in range(BIAS_PIECES):
            place[p * LANES + h, h * LANES + HEAD_DIM + p] = -1.0
            ones[0, h * LANES + HEAD_DIM + BIAS_PIECES + p] = 1.0
    return jnp.asarray(place, BF16), jnp.asarray(ones, F32)


def _query_bias_rows(chi, cmid, clo):
    b, s, _ = chi.shape
    t = b * s
    pcs = [x[:, :, :B_HEADS].reshape(t, B_HEADS).T for x in (chi, cmid, clo)]
    one = jnp.ones((B_HEADS, t), BF16)
    zero = jnp.zeros((B_HEADS, t), BF16)
    rows = [one] * BIAS_PIECES + pcs + [zero] * (BF16_ROWS - 2 * BIAS_PIECES)
    return jnp.stack(rows, axis=1).reshape(B_HEADS * BF16_ROWS, t)


def kernel(x, g_attn_a, w_in_a, w_o_a, g_kv, w_kv_b, b_f, g_attn_b, w_q_b, w_o_b,
           g_mlp, w_up, w_down, g_final):
    b, s, d = x.shape
    t = b * s
    depth = g_mlp.shape[0]
    n_a = g_attn_a.shape[0]
    tm = min(512, s)
    tq = min(ATTN_BLOCK, s)
    tf = 512

    cos, sin = _rope_tables(s)
    gf = g_final.reshape(1, d)
    h = x.reshape(t, d)
    kaug = vt_b = bias_rows = None

    for layer in range(depth):
        if layer < n_a:
            qt, k, vt, qit, ki, wit = _aproj(h, g_attn_a[layer].reshape(1, d),
                                             _prep_w_in_a(w_in_a[layer]), cos, sin, s, tm, tq)
            o = _dsa(qt, qit, wit, ki, k, vt, b, s, tq)
            w_o = w_o_a[layer]
        else:
            jb = layer - n_a
            if jb == 0:
                wk = jnp.pad(w_kv_b[:, :B_QK].reshape(d, B_HEADS, HEAD_DIM),
                             ((0, 0), (0, 0), (0, LANES - HEAD_DIM))).reshape(d, B_KAUG)
                wf = jnp.pad(w_kv_b[:, 2 * B_QK:], ((0, 0), (0, LANES - B_HEADS)))
                bias = jnp.pad(b_f, (0, LANES - B_HEADS)).reshape(1, LANES)
                place, ones = _bias_placement()
                kaug, vt_b, chi, cmid, clo = _kvproj(
                    h.reshape(b, s, d), g_kv.reshape(1, d), wk.astype(BF16),
                    w_kv_b[:, B_QK:2 * B_QK].astype(BF16), wf.astype(BF16), bias,
                    place, ones, tm, tq)
                bias_rows = _query_bias_rows(chi, cmid, clo)
            qt = _qproj(h, g_attn_b[jb].reshape(1, d), w_q_b[jb].astype(BF16), bias_rows, tm)
            o = _fox(qt, kaug, vt_b, b, s, tq)
            w_o = w_o_b[jb]
        h = _mlp(h, o, w_o.astype(BF16), g_mlp[layer].reshape(1, d),
                 w_up[layer].astype(BF16), w_down[layer].astype(BF16), gf,
                 layer == depth - 1, tm, tf)
    return h.reshape(b, s, d)
```

```python
import functools

import numpy as np
import jax
import jax.numpy as jnp
from jax import lax
from jax.experimental import pallas as pl
from jax.experimental.pallas import tpu as pltpu

LANES = 128
SUBLANES = 8
BF16_ROWS = 16
HEAD_DIM = 64
ROPE_HALF = HEAD_DIM // 2
CHUNK = 64
A_HEADS = 16
A_KV_HEADS = 4
A_GROUP = A_HEADS // A_KV_HEADS
IDX_HEADS = 8
TOPK_MAX = 256
B_HEADS = 16
ROPE_THETA = 10000.0
EPS = 1e-6
INT_MIN = -(2 ** 31)
MASKED = -1e30
LOG2E = 1.4426950408889634
QK_SCALE = HEAD_DIM ** -0.5 * LOG2E
VMEM_LIMIT_BYTES = 56 * 1024 * 1024
ATTN_BLOCK = 256

F32 = jnp.float32
BF16 = jnp.bfloat16
I32 = jnp.int32


def _params(*sem):
    return pltpu.CompilerParams(dimension_semantics=sem,
                                vmem_limit_bytes=VMEM_LIMIT_BYTES)


def _resident(shape):
    nd = len(shape)
    return pl.BlockSpec(shape, lambda *_: (0,) * nd, pipeline_mode=pl.Buffered(1))


def _rms(x, g):
    return x * lax.rsqrt(jnp.mean(x * x, axis=-1, keepdims=True) + EPS) * g


def _colsum8(x):
    n, w = x.shape
    return x.reshape(n // SUBLANES, SUBLANES, w).sum(axis=0)


def _colmax8(x):
    n, w = x.shape
    return x.reshape(n // SUBLANES, SUBLANES, w).max(axis=0)


V_ROWS = HEAD_DIM + BF16_ROWS


def _store_value_head(vt_ref, sb, head, yt):
    r = head * V_ROWS
    tk = yt.shape[1]
    vt_ref[sb, r:r + HEAD_DIM, :] = yt
    first_row = lax.broadcasted_iota(I32, (BF16_ROWS, tk), 0) == 0
    vt_ref[sb, r + HEAD_DIM:r + V_ROWS, :] = jnp.where(first_row, 1.0, 0.0).astype(BF16)


def _softmax_pv(s_buf, mx_buf, vt, h, m_sc, acc_sc):
    m_old = m_sc[h]
    m_new = jnp.maximum(m_old, jnp.max(mx_buf[h], axis=0, keepdims=True))
    alpha = jnp.exp2(m_old - m_new)
    p = jnp.exp2(s_buf[h] - m_new)
    acc_sc[h] = alpha * acc_sc[h] + jnp.dot(vt, p.astype(BF16), preferred_element_type=F32)
    m_sc[h] = m_new


def _normalised(acc_sc, h):
    acc = acc_sc[h]
    return acc[0:HEAD_DIM, :] / acc[HEAD_DIM:HEAD_DIM + 1, :]


A_Q = A_HEADS * HEAD_DIM
A_KD = A_KV_HEADS * LANES
A_V = A_KV_HEADS * HEAD_DIM
A_VT = A_KV_HEADS * V_ROWS
A_QI = IDX_HEADS * HEAD_DIM
A_OFF_K = A_Q
A_OFF_V = A_OFF_K + A_KD
A_OFF_QI = A_OFF_V + A_V
A_OFF_KI = A_OFF_QI + A_QI
A_OFF_WI = A_OFF_KI + LANES
A_COLS = A_OFF_WI + LANES


def _aproj_kernel(h_ref, g_ref, w_ref, cos_ref, sin_ref,
                  qt_ref, k_ref, vt_ref, qit_ref, ki_ref, wit_ref, *, wi_scale):
    tm = h_ref.shape[0]
    tk = vt_ref.shape[2]
    xn = _rms(h_ref[...], g_ref[...]).astype(BF16)
    cos = cos_ref[...]
    sin = sin_ref[...]
    lane = lax.broadcasted_iota(I32, (tm, LANES), 1)
    first_half = (lane & ROPE_HALF) == 0

    def rope(y):
        ys = jnp.where(first_half, pltpu.roll(y, LANES - ROPE_HALF, 1),
                       pltpu.roll(y, ROPE_HALF, 1))
        return y * cos + ys * sin

    def chunks(c0, n):
        step = 4 * LANES
        for s0 in range(0, n, step):
            sn = min(step, n - s0)
            y = jnp.dot(xn, w_ref[:, c0 + s0:c0 + s0 + sn], preferred_element_type=F32)
            for j in range(sn // LANES):
                yield (s0 // LANES + j), y[:, j * LANES:(j + 1) * LANES]

    for j, y in chunks(0, A_Q):
        qt_ref[j * LANES:(j + 1) * LANES, :] = (rope(y) * QK_SCALE).T.astype(BF16)
    for j, y in chunks(A_OFF_K, A_KD):
        k_ref[:, j * LANES:(j + 1) * LANES] = rope(y).astype(BF16)
    for j, y in chunks(A_OFF_V, A_V):
        for sb in range(tm // tk):
            yt = y[sb * tk:(sb + 1) * tk].T.astype(BF16)
            for e in range(2):
                _store_value_head(vt_ref, sb, 2 * j + e, yt[e * HEAD_DIM:(e + 1) * HEAD_DIM, :])
    for j, y in chunks(A_OFF_QI, A_QI):
        qit_ref[j * LANES:(j + 1) * LANES, :] = rope(y).T.astype(BF16)
    for j, y in chunks(A_OFF_KI, LANES):
        ki_ref[...] = rope(y).astype(BF16)
    for j, y in chunks(A_OFF_WI, LANES):
        wit_ref[...] = (y * wi_scale).T[:IDX_HEADS, :]


def _aproj(h, g, w, cos, sin, seq, tm, tk):
    t, d = h.shape
    npos = seq // tm
    row = lambda n: pl.BlockSpec((tm, n), lambda i: (i, 0))
    col = lambda n: pl.BlockSpec((n, tm), lambda i: (0, i))
    tab = pl.BlockSpec((tm, LANES), lambda i: (i % npos, 0))
    wi_scale = IDX_HEADS ** -0.5 * HEAD_DIM ** -0.5
    return pl.pallas_call(
        functools.partial(_aproj_kernel, wi_scale=wi_scale),
        grid=(t // tm,),
        in_specs=[row(d), _resident((1, d)), _resident((d, A_COLS)), tab, tab],
        out_specs=[col(A_Q), row(A_KD),
                   pl.BlockSpec((tm // tk, A_VT, tk), lambda i: (i, 0, 0)),
                   col(A_QI), row(LANES), col(IDX_HEADS)],
        out_shape=[jax.ShapeDtypeStruct((A_Q, t), BF16),
                   jax.ShapeDtypeStruct((t, A_KD), BF16),
                   jax.ShapeDtypeStruct((t // tk, A_VT, tk), BF16),
                   jax.ShapeDtypeStruct((A_QI, t), BF16),
                   jax.ShapeDtypeStruct((t, LANES), BF16),
                   jax.ShapeDtypeStruct((IDX_HEADS, t), F32)],
        compiler_params=_params("parallel"),
        name="a_proj",
    )(h, g, w, cos, sin)


KEY_BITS = 32
BIT_TRANSPOSE_STEPS = (16, 8, 4, 2, 1)
BIT_TRANSPOSE_MASKS = (0x0000FFFF, 0x00FF00FF, 0x0F0F0F0F, 0x33333333, 0x55555555)


def _dsa_kernel(qt_ref, qit_ref, wit_ref, ki_ref, k_ref, vt_ref, o_ref,
                key_sc, planes_sc, qiw_sc, qw_sc, sa_sc, sb_sc, mxa_sc, mxb_sc, m_sc, acc_sc,
                *, tq, n_sel, idx_bits):
    tk = tq
    jj = pl.program_id(1)
    kiota = lax.broadcasted_iota(I32, (tk, tq), 0)
    zeros_half = jnp.zeros((HEAD_DIM, tq), BF16)

    for sub in range(2):
        cols = slice(sub * tq, (sub + 1) * tq)
        for h in range(IDX_HEADS):
            qiw_sc[sub * IDX_HEADS + h, 0:HEAD_DIM, :] = qit_ref[h * HEAD_DIM:(h + 1) * HEAD_DIM, cols]
            qiw_sc[sub * IDX_HEADS + h, HEAD_DIM:LANES, :] = zeros_half
        for h in range(A_HEADS):
            qw_sc[sub * A_HEADS + h, 0:HEAD_DIM, :] = qt_ref[h * HEAD_DIM:(h + 1) * HEAD_DIM, cols]
            qw_sc[sub * A_HEADS + h, HEAD_DIM:LANES, :] = zeros_half

    def select(sub):
        j = 2 * jj + sub
        nkb = j + 1
        keys = key_sc.at[sub]
        qpos = j * tq + lax.broadcasted_iota(I32, (1, tq), 1)
        limit = (qpos // CHUNK + 1) * CHUNK

        def score_body(c, carry):
            kc = ki_ref[pl.ds(pl.multiple_of(c * tk, tk), tk), :]
            sc = jnp.zeros((tk, tq), F32)
            for h in range(IDX_HEADS):
                lg = jnp.dot(kc, qiw_sc[sub * IDX_HEADS + h], preferred_element_type=F32)
                sc = sc + jnp.maximum(lg, 0.0) * wit_ref[h:h + 1, sub * tq:(sub + 1) * tq]
            sc = jnp.where(sc == 0.0, 0.0, sc)
            bits = pltpu.bitcast(sc, I32)
            key = bits ^ ((bits >> 31) & 0x7FFFFFFF)
            key = jnp.where(c * tk + kiota < limit, key, INT_MIN)
            keys[c] = key
            x = key ^ INT_MIN
            tiles = [x[v * SUBLANES:(v + 1) * SUBLANES, :] for v in range(KEY_BITS)]
            for jbit, mask in zip(BIT_TRANSPOSE_STEPS, BIT_TRANSPOSE_MASKS):
                for v in range(KEY_BITS):
                    if v & jbit == 0:
                        lo, hi = tiles[v], tiles[v + jbit]
                        t = (lax.shift_right_logical(lo, jbit) ^ hi) & mask
                        tiles[v] = lo ^ jnp.left_shift(t, jbit)
                        tiles[v + jbit] = hi ^ t
            for b in range(KEY_BITS):
                planes_sc[b, c] = tiles[b]
            return carry

        lax.fori_loop(0, nkb, score_body, 0)

        @pl.when(nkb % 2 == 1)
        def _():
            keys[nkb] = jnp.full((tk, tq), INT_MIN, I32)

        def count(pred):
            def body(cc, acc):
                for u in range(2):
                    c = 2 * cc + u
                    acc = acc + _colsum8(jnp.where(pred(c, keys[c]), 1.0, 0.0))
                return acc
            acc = lax.fori_loop(0, (nkb + 1) // 2, body, jnp.zeros((SUBLANES, tq), F32))
            return jnp.sum(acc, axis=0, keepdims=True)

        nwords = planes_sc.shape[1] * SUBLANES
        word_block = lax.broadcasted_iota(I32, (nwords, 1), 0) // SUBLANES
        cand = jnp.broadcast_to(jnp.where(word_block < nkb, -1, 0), (nwords, tq))

        def ones_count(words):
            cnt = lax.population_count(words).reshape(nwords // SUBLANES, SUBLANES, tq).sum(axis=0)
            return jnp.sum(cnt.astype(F32), axis=0, keepdims=True)

        thr_u = jnp.zeros((1, tq), I32)
        above = jnp.zeros((1, tq), F32)
        for b in range(KEY_BITS - 1, -1, -1):
            plane = planes_sc[b].reshape(nwords, tq)
            ones = cand & plane
            cnt1 = ones_count(ones)
            take = above + cnt1 >= n_sel
            thr_u = jnp.where(take, thr_u | (INT_MIN if b == KEY_BITS - 1 else 1 << b), thr_u)
            above = jnp.where(take, above, above + cnt1)
            cand = jnp.where(take, ones, cand ^ ones)
        cnt_thr = jnp.where(thr_u == 0, 0.0, above + ones_count(cand))
        thr = jnp.maximum(thr_u ^ INT_MIN, INT_MIN + 1)

        @pl.when(jnp.max(cnt_thr) > n_sel)
        def _():
            need = n_sel - count(lambda c, key: key > thr)
            pos = jnp.zeros((1, tq), I32)
            for b in range(idx_bits - 1, -1, -1):
                cand = pos | (1 << b)
                cnt = count(lambda c, key: jnp.where(key == thr, c * tk + kiota, 2 ** 30) < cand)
                pos = jnp.where(cnt < need, cand, pos)

            def demote(c, carry):
                key = keys[c]
                tie_pos = jnp.where(key == thr, c * tk + kiota, -1)
                keys[c] = jnp.where(tie_pos > pos, INT_MIN, key)
                return carry

            lax.fori_loop(0, nkb, demote, 0)

        return thr

    thrs = [select(0), select(1)]

    m_sc[...] = jnp.full(m_sc.shape, MASKED, F32)
    acc_sc[...] = jnp.zeros(acc_sc.shape, F32)
    buf_a = (sa_sc, mxa_sc)
    buf_b = (sb_sc, mxb_sc)

    def select_bias(c, sub):
        return jnp.where(key_sc[sub, c] >= thrs[sub], 0.0, MASKED)

    def scores(c, sub, h, buf, bias):
        n = h // A_GROUP
        r0 = pl.multiple_of(c * tk, tk)
        s = jnp.dot(k_ref[pl.ds(r0, tk), n * LANES:(n + 1) * LANES], qw_sc[sub * A_HEADS + h],
                    preferred_element_type=F32) + bias
        buf[0][sub * A_HEADS + h] = s
        buf[1][sub * A_HEADS + h] = _colmax8(s)

    def update(c, sub, h, buf):
        n = h // A_GROUP
        _softmax_pv(buf[0], buf[1], vt_ref[c, n * V_ROWS:(n + 1) * V_ROWS, :],
                    sub * A_HEADS + h, m_sc, acc_sc)

    def stage(c, cur, nxt, score_subs, update_subs=(0, 1)):
        biases = {sub: select_bias(c + 1, sub) for sub in score_subs}
        for h in range(A_HEADS):
            for sub in score_subs:
                scores(c + 1, sub, h, nxt, biases[sub])
            for sub in update_subs:
                update(c, sub, h, cur)

    for sub in range(2):
        bias0 = select_bias(0, sub)
        for h in range(A_HEADS):
            scores(0, sub, h, buf_a, bias0)

    def attn_body(i, carry):
        stage(2 * i, buf_a, buf_b, (0, 1))
        stage(2 * i + 1, buf_b, buf_a, (0, 1))
        return carry

    lax.fori_loop(0, jj, attn_body, 0)
    last = 2 * jj
    stage(last, buf_a, buf_b, (1,))
    stage(last + 1, buf_b, None, (), update_subs=(1,))

    for sub in range(2):
        for i in range(A_HEADS // 2):
            pair = [_normalised(acc_sc, sub * A_HEADS + h) for h in (2 * i, 2 * i + 1)]
            o_ref[sub * tq:(sub + 1) * tq, i * LANES:(i + 1) * LANES] = (
                jnp.concatenate(pair, axis=0).T.astype(o_ref.dtype))


def _dsa(qt, qit, wit, ki, k, vt, b, s, tq):
    t = b * s
    nq = s // tq
    assert nq % 2 == 0, "dsa_attention walks query blocks in pairs"
    n_sel = min(TOPK_MAX, s // 4)
    idx_bits = max(1, (s - 1).bit_length())
    assert tq == KEY_BITS * SUBLANES, "the bit-sliced search transposes 32 sublane tiles per key block"
    qcol = lambda n: pl.BlockSpec((n, 2 * tq), lambda bi, j: (0, bi * (nq // 2) + j))
    full = lambda n: pl.BlockSpec((s, n), lambda bi, j: (bi, 0), pipeline_mode=pl.Buffered(1))
    return pl.pallas_call(
        functools.partial(_dsa_kernel, tq=tq, n_sel=float(n_sel), idx_bits=idx_bits),
        grid=(b, nq // 2),
        in_specs=[qcol(A_Q), qcol(A_QI), qcol(IDX_HEADS), full(LANES), full(A_KD),
                  pl.BlockSpec((nq, A_VT, tq), lambda bi, j: (bi, 0, 0),
                               pipeline_mode=pl.Buffered(1))],
        out_specs=pl.BlockSpec((2 * tq, A_Q), lambda bi, j: (bi * (nq // 2) + j, 0)),
        out_shape=jax.ShapeDtypeStruct((t, A_Q), BF16),
        scratch_shapes=[
            pltpu.VMEM((2, nq, tq, tq), I32),
            pltpu.VMEM((KEY_BITS, nq, SUBLANES, tq), I32),
            pltpu.VMEM((2 * IDX_HEADS, LANES, tq), BF16),
            pltpu.VMEM((2 * A_HEADS, LANES, tq), BF16),
            pltpu.VMEM((2 * A_HEADS, tq, tq), F32),
            pltpu.VMEM((2 * A_HEADS, tq, tq), F32),
            pltpu.VMEM((2 * A_HEADS, SUBLANES, tq), F32),
            pltpu.VMEM((2 * A_HEADS, SUBLANES, tq), F32),
            pltpu.VMEM((2 * A_HEADS, 1, tq), F32),
            pltpu.VMEM((2 * A_HEADS, V_ROWS, tq), F32),
        ],
        compiler_params=_params("parallel", "arbitrary"),
        name="dsa_attention",
    )(qt, qit, wit, ki, k, vt)


B_QK = B_HEADS * HEAD_DIM
B_KAUG = B_HEADS * LANES
B_VT = B_HEADS * V_ROWS
BIAS_PIECES = 3


def _split3(x):
    hi = x.astype(BF16)
    r1 = x - hi.astype(F32)
    mid = r1.astype(BF16)
    low = (r1 - mid.astype(F32)).astype(BF16)
    return hi, mid, low


def _kvproj_kernel(h_ref, g_ref, wk_ref, wv_ref, wf_ref, bf_ref, place_ref, ones_ref,
                   kaug_ref, vt_ref, chi_ref, cmid_ref, clo_ref, carry_sc):
    tm = h_ref.shape[0]
    tk = vt_ref.shape[2]

    @pl.when(pl.program_id(1) == 0)
    def _():
        carry_sc[...] = jnp.zeros(carry_sc.shape, F32)

    xn = _rms(h_ref[...], g_ref[...]).astype(BF16)
    z = jnp.dot(xn, wf_ref[...], preferred_element_type=F32) + bf_ref[...]
    logf = jnp.minimum(z, 0.0) - jnp.log1p(jnp.exp(-jnp.abs(z)))
    tri = (lax.broadcasted_iota(I32, (tm, tm), 0) >=
           lax.broadcasted_iota(I32, (tm, tm), 1)).astype(BF16)
    cs = sum(jnp.dot(tri, piece, preferred_element_type=F32) for piece in _split3(logf))
    cum = cs + carry_sc[...]
    carry_sc[...] = cum[tm - 1:tm, :]
    hi, mid, low = _split3(cum * LOG2E)
    chi_ref[...] = hi
    cmid_ref[...] = mid
    clo_ref[...] = low
    pieces = jnp.concatenate([hi, mid, low], axis=1)

    step = 4 * LANES
    for c0 in range(0, B_KAUG, step):
        y = (jnp.dot(xn, wk_ref[:, c0:c0 + step], preferred_element_type=F32) +
             jnp.dot(pieces, place_ref[:, c0:c0 + step], preferred_element_type=F32) +
             ones_ref[:, c0:c0 + step])
        kaug_ref[:, c0:c0 + step] = y.astype(BF16)
    for c0 in range(0, B_QK, step):
        y = jnp.dot(xn, wv_ref[:, c0:c0 + step], preferred_element_type=F32)
        for jj in range(step // LANES):
            head = (c0 + jj * LANES) // HEAD_DIM
            for sb in range(tm // tk):
                yt = y[sb * tk:(sb + 1) * tk, jj * LANES:(jj + 1) * LANES].T.astype(BF16)
                for e in range(2):
                    _store_value_head(vt_ref, sb, head + e, yt[e * HEAD_DIM:(e + 1) * HEAD_DIM, :])


def _kvproj(h3, g, wk, wv, wf, bf, place, ones, tm, tk):
    b, s, d = h3.shape
    ns = s // tm
    blk = lambda n: pl.BlockSpec((None, tm, n), lambda bi, i: (bi, i, 0))
    return pl.pallas_call(
        _kvproj_kernel,
        grid=(b, ns),
        in_specs=[blk(d), _resident((1, d)), _resident(wk.shape), _resident(wv.shape),
                  _resident(wf.shape), _resident((1, LANES)), _resident(place.shape),
                  _resident(ones.shape)],
        out_specs=[blk(B_KAUG),
                   pl.BlockSpec((tm // tk, B_VT, tk), lambda bi, i: (bi * ns + i, 0, 0)),
                   blk(LANES), blk(LANES), blk(LANES)],
        out_shape=[jax.ShapeDtypeStruct((b, s, B_KAUG), BF16),
                   jax.ShapeDtypeStruct((b * s // tk, B_VT, tk), BF16),
                   jax.ShapeDtypeStruct((b, s, LANES), BF16),
                   jax.ShapeDtypeStruct((b, s, LANES), BF16),
                   jax.ShapeDtypeStruct((b, s, LANES), BF16)],
        scratch_shapes=[pltpu.VMEM((1, LANES), F32)],
        compiler_params=_params("parallel", "arbitrary"),
        name="kv_proj_cumsum",
    )(h3, g, wk, wv, wf, bf, place, ones)


def _qproj_kernel(h_ref, g_ref, w_ref, rows_ref, qt_ref):
    tm = h_ref.shape[0]
    xn = _rms(h_ref[...], g_ref[...]).astype(BF16)
    pad = jnp.zeros((LANES - HEAD_DIM - BF16_ROWS, tm), BF16)
    step = 4 * LANES
    for c0 in range(0, B_QK, step):
        y = jnp.dot(xn, w_ref[:, c0:c0 + step], preferred_element_type=F32) * QK_SCALE
        for jj in range(step // LANES):
            yt = y[:, jj * LANES:(jj + 1) * LANES].T.astype(BF16)
            for e in range(2):
                h = (c0 + jj * LANES) // HEAD_DIM + e
                r = h * LANES
                qt_ref[r:r + HEAD_DIM, :] = yt[e * HEAD_DIM:(e + 1) * HEAD_DIM, :]
                qt_ref[r + HEAD_DIM:r + HEAD_DIM + BF16_ROWS, :] = (
                    rows_ref[h * BF16_ROWS:(h + 1) * BF16_ROWS, :])
                qt_ref[r + HEAD_DIM + BF16_ROWS:r + LANES, :] = pad


def _qproj(h, g, w, rows, tm):
    t, d = h.shape
    return pl.pallas_call(
        _qproj_kernel,
        grid=(t // tm,),
        in_specs=[pl.BlockSpec((tm, d), lambda i: (i, 0)), _resident((1, d)), _resident(w.shape),
                  pl.BlockSpec((B_HEADS * BF16_ROWS, tm), lambda i: (0, i))],
        out_specs=pl.BlockSpec((B_KAUG, tm), lambda i: (0, i)),
        out_shape=jax.ShapeDtypeStruct((B_KAUG, t), BF16),
        compiler_params=_params("parallel"),
        name="q_proj",
    )(h, g, w, rows)


FOX_HEADS_PER_STEP = 8


def _fox_kernel(qt_ref, k_ref, vt_ref, o_ref, sa_sc, sb_sc, mxa_sc, mxb_sc, m_sc, acc_sc, *, tq):
    tk = tq
    hps = FOX_HEADS_PER_STEP
    jj = pl.program_id(2)
    m_sc[...] = jnp.full(m_sc.shape, MASKED, F32)
    acc_sc[...] = jnp.zeros(acc_sc.shape, F32)
    kiota = lax.broadcasted_iota(I32, (tk, tq), 0)
    qiota = lax.broadcasted_iota(I32, (1, tq), 1)
    buf_a = (sa_sc, mxa_sc)
    buf_b = (sb_sc, mxb_sc)

    def scores(c, sub, h, buf, qlimit):
        r0 = pl.multiple_of(c * tk, tk)
        s = jnp.dot(k_ref[pl.ds(r0, tk), h * LANES:(h + 1) * LANES],
                    qt_ref[h * LANES:(h + 1) * LANES, sub * tq:(sub + 1) * tq],
                    preferred_element_type=F32)
        if qlimit is not None:
            s = jnp.where(kiota <= qlimit, s, MASKED)
        buf[0][sub * hps + h] = s
        buf[1][sub * hps + h] = _colmax8(s)

    def update(c, sub, h, buf):
        _softmax_pv(buf[0], buf[1], vt_ref[c, h * V_ROWS:(h + 1) * V_ROWS, :],
                    sub * hps + h, m_sc, acc_sc)

    def stage(c, cur, nxt, limits, update_subs=(0, 1)):
        for h in range(hps):
            for sub, qlimit in limits:
                scores(c + 1, sub, h, nxt, qlimit)
            for sub in update_subs:
                update(c, sub, h, cur)

    first_limit = qiota + jnp.where(jj == 0, 0, tk)
    for h in range(hps):
        scores(0, 0, h, buf_a, first_limit)
        scores(0, 1, h, buf_a, None)

    def main_body(i, carry):
        c = 2 * i
        stage(c, buf_a, buf_b, ((0, None), (1, None)))
        limit0 = qiota + jnp.where(i == jj - 1, 0, tk)
        stage(c + 1, buf_b, buf_a, ((0, limit0), (1, None)))
        return carry

    lax.fori_loop(0, jj, main_body, 0)
    last = 2 * jj
    stage(last, buf_a, buf_b, ((1, qiota),))
    stage(last + 1, buf_b, None, (), update_subs=(1,))
    for sub in range(2):
        outs = [_normalised(acc_sc, sub * hps + h) for h in range(hps)]
        o_ref[sub * tq:(sub + 1) * tq, :] = jnp.concatenate(outs, axis=0).T.astype(o_ref.dtype)


def _fox(qt, kaug, vt, b, s, tq):
    t = b * s
    nq = s // tq
    assert nq % 2 == 0, "fox_attention walks query blocks in pairs"
    hps = FOX_HEADS_PER_STEP
    nstate = 2 * hps
    return pl.pallas_call(
        functools.partial(_fox_kernel, tq=tq),
        grid=(b, B_HEADS // hps, nq // 2),
        in_specs=[
            pl.BlockSpec((hps * LANES, 2 * tq), lambda bi, p, j: (p, bi * (nq // 2) + j)),
            pl.BlockSpec((None, s, hps * LANES), lambda bi, p, j: (bi, 0, p)),
            pl.BlockSpec((nq, hps * V_ROWS, tq), lambda bi, p, j: (bi, p, 0)),
        ],
        out_specs=pl.BlockSpec((2 * tq, hps * HEAD_DIM), lambda bi, p, j: (bi * (nq // 2) + j, p)),
        out_shape=jax.ShapeDtypeStruct((t, B_QK), BF16),
        scratch_shapes=[
            pltpu.VMEM((nstate, tq, tq), F32),
            pltpu.VMEM((nstate, tq, tq), F32),
            pltpu.VMEM((nstate, SUBLANES, tq), F32),
            pltpu.VMEM((nstate, SUBLANES, tq), F32),
            pltpu.VMEM((nstate, 1, tq), F32),
            pltpu.VMEM((nstate, V_ROWS, tq), F32),
        ],
        compiler_params=_params("parallel", "parallel", "arbitrary"),
        name="fox_attention",
    )(qt, kaug, vt)


def _mlp_kernel(h_ref, o_ref, wo_ref, g_ref, wup_ref, wdn_ref, gf_ref, out_ref, *, tf, final):
    h1 = h_ref[...] + jnp.dot(o_ref[...], wo_ref[...], preferred_element_type=F32)
    xn = _rms(h1, g_ref[...]).astype(BF16)
    acc = h1
    for c0 in range(0, wup_ref.shape[1], tf):
        u = jnp.dot(xn, wup_ref[:, c0:c0 + tf], preferred_element_type=F32)
        a = jnp.square(jnp.maximum(u, 0.0)).astype(BF16)
        acc = acc + jnp.dot(a, wdn_ref[c0:c0 + tf, :], preferred_element_type=F32)
    if final:
        acc = _rms(acc, gf_ref[...])
    out_ref[...] = acc


def _mlp(h, o, wo, g, wup, wdn, gf, final, tm, tf):
    t, d = h.shape
    row = pl.BlockSpec((tm, d), lambda i: (i, 0))
    return pl.pallas_call(
        functools.partial(_mlp_kernel, tf=tf, final=final),
        grid=(t // tm,),
        in_specs=[row, row, _resident(wo.shape), _resident((1, d)), _resident(wup.shape),
                  _resident(wdn.shape), _resident((1, d))],
        out_specs=row,
        out_shape=jax.ShapeDtypeStruct((t, d), F32),
        compiler_params=_params("parallel"),
        name="oproj_mlp",
    )(h, o, wo, g, wup, wdn, gf)


def _dup_heads(w, heads):
    d = w.shape[0]
    w = w.reshape(d, heads, HEAD_DIM)
    return jnp.concatenate([w, w], axis=-1).reshape(d, heads * LANES)


def _prep_w_in_a(w):
    kv = A_KV_HEADS * HEAD_DIM
    o_k, o_v, o_qi = A_Q, A_Q + kv, A_Q + 2 * kv
    o_ki, o_wi = o_qi + A_QI, o_qi + A_QI + HEAD_DIM
    wi = w[:, o_wi:o_wi + IDX_HEADS]
    return jnp.concatenate([
        w[:, :A_Q],
        _dup_heads(w[:, o_k:o_v], A_KV_HEADS),
        w[:, o_v:o_qi],
        w[:, o_qi:o_ki],
        _dup_heads(w[:, o_ki:o_wi], 1),
        jnp.pad(wi, ((0, 0), (0, LANES - IDX_HEADS))),
    ], axis=1).astype(BF16)


def _rope_tables(seq):
    inv = 1.0 / (ROPE_THETA ** (jnp.arange(0, HEAD_DIM, 2, dtype=F32) / HEAD_DIM))
    ang = jnp.arange(seq, dtype=F32)[:, None] * inv[None, :]
    c, s = jnp.cos(ang), jnp.sin(ang)
    return jnp.concatenate([c, c, c, c], axis=1), jnp.concatenate([-s, s, -s, s], axis=1)


def _bias_placement():
    place = np.zeros((BIAS_PIECES * LANES, B_KAUG), np.float32)
    ones = np.zeros((1, B_KAUG), np.float32)
    for h in range(B_HEADS):
        for p in range(BIAS_PIECES):
            place[p * LANES + h, h * LANES + HEAD_DIM + p] = -1.0
            ones[0, h * LANES + HEAD_DIM + BIAS_PIECES + p] = 1.0
    return jnp.asarray(place, BF16), jnp.asarray(ones, F32)


def _query_bias_rows(chi, cmid, clo):
    b, s, _ = chi.shape
    t = b * s
    pcs = [x[:, :, :B_HEADS].reshape(t, B_HEADS).T for x in (chi, cmid, clo)]
    one = jnp.ones((B_HEADS, t), BF16)
    zero = jnp.zeros((B_HEADS, t), BF16)
    rows = [one] * BIAS_PIECES + pcs + [zero] * (BF16_ROWS - 2 * BIAS_PIECES)
    return jnp.stack(rows, axis=1).reshape(B_HEADS * BF16_ROWS, t)


def kernel(x, g_attn_a, w_in_a, w_o_a, g_kv, w_kv_b, b_f, g_attn_b, w_q_b, w_o_b,
           g_mlp, w_up, w_down, g_final):
    b, s, d = x.shape
    t = b * s
    depth = g_mlp.shape[0]
    n_a = g_attn_a.shape[0]
    tm = min(512, s)
    tq = min(ATTN_BLOCK, s)
    tf = 512

    cos, sin = _rope_tables(s)
    gf = g_final.reshape(1, d)
    h = x.reshape(t, d)
    kaug = vt_b = bias_rows = None

    for layer in range(depth):
        if layer < n_a:
            qt, k, vt, qit, ki, wit = _aproj(h, g_attn_a[layer].reshape(1, d),
                                             _prep_w_in_a(w_in_a[layer]), cos, sin, s, tm, tq)
            o = _dsa(qt, qit, wit, ki, k, vt, b, s, tq)
            w_o = w_o_a[layer]
        else:
            jb = layer - n_a
            if jb == 0:
                wk = jnp.pad(w_kv_b[:, :B_QK].reshape(d, B_HEADS, HEAD_DIM),
                             ((0, 0), (0, 0), (0, LANES - HEAD_DIM))).reshape(d, B_KAUG)
                wf = jnp.pad(w_kv_b[:, 2 * B_QK:], ((0, 0), (0, LANES - B_HEADS)))
                bias = jnp.pad(b_f, (0, LANES - B_HEADS)).reshape(1, LANES)
                place, ones = _bias_placement()
                kaug, vt_b, chi, cmid, clo = _kvproj(
                    h.reshape(b, s, d), g_kv.reshape(1, d), wk.astype(BF16),
                    w_kv_b[:, B_QK:2 * B_QK].astype(BF16), wf.astype(BF16), bias,
                    place, ones, tm, tq)
                bias_rows = _query_bias_rows(chi, cmid, clo)
            qt = _qproj(h, g_attn_b[jb].reshape(1, d), w_q_b[jb].astype(BF16), bias_rows, tm)
            o = _fox(qt, kaug, vt_b, b, s, tq)
            w_o = w_o_b[jb]
        h = _mlp(h, o, w_o.astype(BF16), g_mlp[layer].reshape(1, d),
                 w_up[layer].astype(BF16), w_down[layer].astype(BF16), gf,
                 layer == depth - 1, tm, tf)
    return h.reshape(b, s, d)
```

```python
import functools

import numpy as np
import jax
import jax.numpy as jnp
from jax import lax
from jax.experimental import pallas as pl
from jax.experimental.pallas import tpu as pltpu

LANES = 128
SUBLANES = 8
BF16_ROWS = 16
HEAD_DIM = 64
ROPE_HALF = HEAD_DIM // 2
CHUNK = 64
A_HEADS = 16
A_KV_HEADS = 4
A_GROUP = A_HEADS // A_KV_HEADS
IDX_HEADS = 8
TOPK_MAX = 256
B_HEADS = 16
ROPE_THETA = 10000.0
EPS = 1e-6
INT_MIN = -(2 ** 31)
MASKED = -1e30
LOG2E = 1.4426950408889634
QK_SCALE = HEAD_DIM ** -0.5 * LOG2E
VMEM_LIMIT_BYTES = 56 * 1024 * 1024
ATTN_BLOCK = 256

F32 = jnp.float32
BF16 = jnp.bfloat16
I32 = jnp.int32


def _params(*sem):
    return pltpu.CompilerParams(dimension_semantics=sem,
                                vmem_limit_bytes=VMEM_LIMIT_BYTES)


def _resident(shape):
    nd = len(shape)
    return pl.BlockSpec(shape, lambda *_: (0,) * nd, pipeline_mode=pl.Buffered(1))


def _rms(x, g):
    return x * lax.rsqrt(jnp.mean(x * x, axis=-1, keepdims=True) + EPS) * g


def _colsum8(x):
    n, w = x.shape
    return x.reshape(n // SUBLANES, SUBLANES, w).sum(axis=0)


def _colmax8(x):
    n, w = x.shape
    return x.reshape(n // SUBLANES, SUBLANES, w).max(axis=0)


V_ROWS = HEAD_DIM + BF16_ROWS


def _store_value_head(vt_ref, sb, head, yt):
    r = head * V_ROWS
    tk = yt.shape[1]
    vt_ref[sb, r:r + HEAD_DIM, :] = yt
    first_row = lax.broadcasted_iota(I32, (BF16_ROWS, tk), 0) == 0
    vt_ref[sb, r + HEAD_DIM:r + V_ROWS, :] = jnp.where(first_row, 1.0, 0.0).astype(BF16)


def _softmax_pv(s_buf, mx_buf, vt, h, m_sc, acc_sc):
    m_old = m_sc[h]
    m_new = jnp.maximum(m_old, jnp.max(mx_buf[h], axis=0, keepdims=True))
    alpha = jnp.exp2(m_old - m_new)
    p = jnp.exp2(s_buf[h] - m_new)
    acc_sc[h] = alpha * acc_sc[h] + jnp.dot(vt, p.astype(BF16), preferred_element_type=F32)
    m_sc[h] = m_new


def _normalised(acc_sc, h):
    acc = acc_sc[h]
    return acc[0:HEAD_DIM, :] / acc[HEAD_DIM:HEAD_DIM + 1, :]


A_Q = A_HEADS * HEAD_DIM
A_KD = A_KV_HEADS * LANES
A_V = A_KV_HEADS * HEAD_DIM
A_VT = A_KV_HEADS * V_ROWS
A_QI = IDX_HEADS * HEAD_DIM
A_OFF_K = A_Q
A_OFF_V = A_OFF_K + A_KD
A_OFF_QI = A_OFF_V + A_V
A_OFF_KI = A_OFF_QI + A_QI
A_OFF_WI = A_OFF_KI + LANES
A_COLS = A_OFF_WI + LANES


def _aproj_kernel(h_ref, g_ref, w_ref, cos_ref, sin_ref,
                  qt_ref, k_ref, vt_ref, qit_ref, ki_ref, wit_ref, *, wi_scale):
    tm = h_ref.shape[0]
    tk = vt_ref.shape[2]
    xn = _rms(h_ref[...], g_ref[...]).astype(BF16)
    cos = cos_ref[...]
    sin = sin_ref[...]
    lane = lax.broadcasted_iota(I32, (tm, LANES), 1)
    first_half = (lane & ROPE_HALF) == 0

    def rope(y):
        ys = jnp.where(first_half, pltpu.roll(y, LANES - ROPE_HALF, 1),
                       pltpu.roll(y, ROPE_HALF, 1))
        return y * cos + ys * sin

    def chunks(c0, n):
        step = 4 * LANES
        for s0 in range(0, n, step):
            sn = min(step, n - s0)
            y = jnp.dot(xn, w_ref[:, c0 + s0:c0 + s0 + sn], preferred_element_type=F32)
            for j in range(sn // LANES):
                yield (s0 // LANES + j), y[:, j * LANES:(j + 1) * LANES]

    for j, y in chunks(0, A_Q):
        qt_ref[j * LANES:(j + 1) * LANES, :] = (rope(y) * QK_SCALE).T.astype(BF16)
    for j, y in chunks(A_OFF_K, A_KD):
        k_ref[:, j * LANES:(j + 1) * LANES] = rope(y).astype(BF16)
    for j, y in chunks(A_OFF_V, A_V):
        for sb in range(tm // tk):
            yt = y[sb * tk:(sb + 1) * tk].T.astype(BF16)
            for e in range(2):
                _store_value_head(vt_ref, sb, 2 * j + e, yt[e * HEAD_DIM:(e + 1) * HEAD_DIM, :])
    for j, y in chunks(A_OFF_QI, A_QI):
        qit_ref[j * LANES:(j + 1) * LANES, :] = rope(y).T.astype(BF16)
    for j, y in chunks(A_OFF_KI, LANES):
        ki_ref[...] = rope(y).astype(BF16)
    for j, y in chunks(A_OFF_WI, LANES):
        wit_ref[...] = (y * wi_scale).T[:IDX_HEADS, :]


def _aproj(h, g, w, cos, sin, seq, tm, tk):
    t, d = h.shape
    npos = seq // tm
    row = lambda n: pl.BlockSpec((tm, n), lambda i: (i, 0))
    col = lambda n: pl.BlockSpec((n, tm), lambda i: (0, i))
    tab = pl.BlockSpec((tm, LANES), lambda i: (i % npos, 0))
    wi_scale = IDX_HEADS ** -0.5 * HEAD_DIM ** -0.5
    return pl.pallas_call(
        functools.partial(_aproj_kernel, wi_scale=wi_scale),
        grid=(t // tm,),
        in_specs=[row(d), _resident((1, d)), _resident((d, A_COLS)), tab, tab],
        out_specs=[col(A_Q), row(A_KD),
                   pl.BlockSpec((tm // tk, A_VT, tk), lambda i: (i, 0, 0)),
                   col(A_QI), row(LANES), col(IDX_HEADS)],
        out_shape=[jax.ShapeDtypeStruct((A_Q, t), BF16),
                   jax.ShapeDtypeStruct((t, A_KD), BF16),
                   jax.ShapeDtypeStruct((t // tk, A_VT, tk), BF16),
                   jax.ShapeDtypeStruct((A_QI, t), BF16),
                   jax.ShapeDtypeStruct((t, LANES), BF16),
                   jax.ShapeDtypeStruct((IDX_HEADS, t), F32)],
        compiler_params=_params("parallel"),
        name="a_proj",
    )(h, g, w, cos, sin)


KEY_BITS = 32
BIT_TRANSPOSE_STEPS = (16, 8, 4, 2, 1)
BIT_TRANSPOSE_MASKS = (0x0000FFFF, 0x00FF00FF, 0x0F0F0F0F, 0x33333333, 0x55555555)


def _dsa_kernel(qt_ref, qit_ref, wit_ref, ki_ref, k_ref, vt_ref, o_ref,
                key_sc, planes_sc, qiw_sc, qw_sc, sa_sc, sb_sc, mxa_sc, mxb_sc, m_sc, acc_sc,
                *, tq, n_sel, idx_bits):
    tk = tq
    jj = pl.program_id(1)
    kiota = lax.broadcasted_iota(I32, (tk, tq), 0)
    zeros_half = jnp.zeros((HEAD_DIM, tq), BF16)

    for sub in range(2):
        cols = slice(sub * tq, (sub + 1) * tq)
        for h in range(IDX_HEADS):
            qiw_sc[sub * IDX_HEADS + h, 0:HEAD_DIM, :] = qit_ref[h * HEAD_DIM:(h + 1) * HEAD_DIM, cols]
            qiw_sc[sub * IDX_HEADS + h, HEAD_DIM:LANES, :] = zeros_half
        for h in range(A_HEADS):
            qw_sc[sub * A_HEADS + h, 0:HEAD_DIM, :] = qt_ref[h * HEAD_DIM:(h + 1) * HEAD_DIM, cols]
            qw_sc[sub * A_HEADS + h, HEAD_DIM:LANES, :] = zeros_half

    def select(sub):
        j = 2 * jj + sub
        nkb = j + 1
        keys = key_sc.at[sub]
        qpos = j * tq + lax.broadcasted_iota(I32, (1, tq), 1)
        limit = (qpos // CHUNK + 1) * CHUNK

        def head_logits(c):
            kc = ki_ref[pl.ds(pl.multiple_of(c * tk, tk), tk), :]
            return [jnp.dot(kc, qiw_sc[sub * IDX_HEADS + h], preferred_element_type=F32)
                    for h in range(IDX_HEADS)]

        def store_keys(c, logits):
            sc = jnp.zeros((tk, tq), F32)
            for h in range(IDX_HEADS):
                sc = sc + jnp.maximum(logits[h], 0.0) * wit_ref[h:h + 1, sub * tq:(sub + 1) * tq]
            sc = jnp.where(sc == 0.0, 0.0, sc)
            bits = pltpu.bitcast(sc, I32)
            key = bits ^ ((bits >> 31) & 0x7FFFFFFF)
            keys[c] = jnp.where(c * tk + kiota < limit, key, INT_MIN)

        def bit_slice(c):
            x = keys[c] ^ INT_MIN
            tiles = [x[v * SUBLANES:(v + 1) * SUBLANES, :] for v in range(KEY_BITS)]
            for jbit, mask in zip(BIT_TRANSPOSE_STEPS, BIT_TRANSPOSE_MASKS):
                for v in range(KEY_BITS):
                    if v & jbit == 0:
                        lo, hi = tiles[v], tiles[v + jbit]
                        t = (lax.shift_right_logical(lo, jbit) ^ hi) & mask
                        tiles[v] = lo ^ jnp.left_shift(t, jbit)
                        tiles[v + jbit] = hi ^ t
            for b in range(KEY_BITS):
                planes_sc[b, c] = tiles[b]

        store_keys(0, head_logits(0))

        def score_body(c, carry):
            logits = head_logits(c)
            bit_slice(c - 1)
            store_keys(c, logits)
            return carry

        lax.fori_loop(1, nkb, score_body, 0)
        bit_slice(nkb - 1)

        def clear_body(c, carry):
            planes_sc[:, c] = jnp.zeros((KEY_BITS, SUBLANES, tq), I32)
            return carry

        lax.fori_loop(nkb, planes_sc.shape[1], clear_body, 0)

        @pl.when(nkb % 2 == 1)
        def _():
            keys[nkb] = jnp.full((tk, tq), INT_MIN, I32)

        def count(pred):
            def body(cc, acc):
                for u in range(2):
                    c = 2 * cc + u
                    acc = acc + _colsum8(jnp.where(pred(c, keys[c]), 1.0, 0.0))
                return acc
            acc = lax.fori_loop(0, (nkb + 1) // 2, body, jnp.zeros((SUBLANES, tq), F32))
            return jnp.sum(acc, axis=0, keepdims=True)

        nwords = planes_sc.shape[1] * SUBLANES
        word_block = lax.broadcasted_iota(I32, (nwords, 1), 0) // SUBLANES
        cand = jnp.broadcast_to(jnp.where(word_block < nkb, -1, 0), (nwords, tq))

        def ones_count(words):
            cnt = lax.population_count(words).reshape(nwords // SUBLANES, SUBLANES, tq).sum(axis=0)
            return jnp.sum(cnt.astype(F32), axis=0, keepdims=True)

        thr_u = jnp.zeros((1, tq), I32)
        above = jnp.zeros((1, tq), F32)
        for b in range(KEY_BITS - 1, -1, -1):
            plane = planes_sc[b].reshape(nwords, tq)
            ones = cand & plane
            cnt1 = ones_count(ones)
            take = above + cnt1 >= n_sel
            thr_u = jnp.where(take, thr_u | (INT_MIN if b == KEY_BITS - 1 else 1 << b), thr_u)
            above = jnp.where(take, above, above + cnt1)
            cand = jnp.where(take, ones, cand ^ ones)
        cnt_thr = jnp.where(thr_u == 0, 0.0, above + ones_count(cand))
        thr = jnp.maximum(thr_u ^ INT_MIN, INT_MIN + 1)

        @pl.when(jnp.max(cnt_thr) > n_sel)
        def _():
            need = n_sel - count(lambda c, key: key > thr)
            pos = jnp.zeros((1, tq), I32)
            for b in range(idx_bits - 1, -1, -1):
                cand = pos | (1 << b)
                cnt = count(lambda c, key: jnp.where(key == thr, c * tk + kiota, 2 ** 30) < cand)
                pos = jnp.where(cnt < need, cand, pos)

            def demote(c, carry):
                key = keys[c]
                tie_pos = jnp.where(key == thr, c * tk + kiota, -1)
                keys[c] = jnp.where(tie_pos > pos, INT_MIN, key)
                return carry

            lax.fori_loop(0, nkb, demote, 0)

        return thr

    thrs = [select(0), select(1)]

    m_sc[...] = jnp.full(m_sc.shape, MASKED, F32)
    acc_sc[...] = jnp.zeros(acc_sc.shape, F32)
    buf_a = (sa_sc, mxa_sc)
    buf_b = (sb_sc, mxb_sc)

    def select_bias(c, sub):
        return jnp.where(key_sc[sub, c] >= thrs[sub], 0.0, MASKED)

    def scores(c, sub, h, buf, bias):
        n = h // A_GROUP
        r0 = pl.multiple_of(c * tk, tk)
        s = jnp.dot(k_ref[pl.ds(r0, tk), n * LANES:(n + 1) * LANES], qw_sc[sub * A_HEADS + h],
                    preferred_element_type=F32) + bias
        buf[0][sub * A_HEADS + h] = s
        buf[1][sub * A_HEADS + h] = _colmax8(s)

    def update(c, sub, h, buf):
        n = h // A_GROUP
        _softmax_pv(buf[0], buf[1], vt_ref[c, n * V_ROWS:(n + 1) * V_ROWS, :],
                    sub * A_HEADS + h, m_sc, acc_sc)

    def stage(c, cur, nxt, score_subs, update_subs=(0, 1)):
        biases = {sub: select_bias(c + 1, sub) for sub in score_subs}
        for h in range(A_HEADS):
            for sub in score_subs:
                scores(c + 1, sub, h, nxt, biases[sub])
            for sub in update_subs:
                update(c, sub, h, cur)

    for sub in range(2):
        bias0 = select_bias(0, sub)
        for h in range(A_HEADS):
            scores(0, sub, h, buf_a, bias0)

    def attn_body(i, carry):
        stage(2 * i, buf_a, buf_b, (0, 1))
        stage(2 * i + 1, buf_b, buf_a, (0, 1))
        return carry

    lax.fori_loop(0, jj, attn_body, 0)
    last = 2 * jj
    stage(last, buf_a, buf_b, (1,))
    stage(last + 1, buf_b, None, (), update_subs=(1,))

    for sub in range(2):
        for i in range(A_HEADS // 2):
            pair = [_normalised(acc_sc, sub * A_HEADS + h) for h in (2 * i, 2 * i + 1)]
            o_ref[sub * tq:(sub + 1) * tq, i * LANES:(i + 1) * LANES] = (
                jnp.concatenate(pair, axis=0).T.astype(o_ref.dtype))


def _dsa(qt, qit, wit, ki, k, vt, b, s, tq):
    t = b * s
    nq = s // tq
    assert nq % 2 == 0, "dsa_attention walks query blocks in pairs"
    n_sel = min(TOPK_MAX, s // 4)
    idx_bits = max(1, (s - 1).bit_length())
    assert tq == KEY_BITS * SUBLANES, "the bit-sliced search transposes 32 sublane tiles per key block"
    qcol = lambda n: pl.BlockSpec((n, 2 * tq), lambda bi, j: (0, bi * (nq // 2) + j))
    full = lambda n: pl.BlockSpec((s, n), lambda bi, j: (bi, 0), pipeline_mode=pl.Buffered(1))
    return pl.pallas_call(
        functools.partial(_dsa_kernel, tq=tq, n_sel=float(n_sel), idx_bits=idx_bits),
        grid=(b, nq // 2),
        in_specs=[qcol(A_Q), qcol(A_QI), qcol(IDX_HEADS), full(LANES), full(A_KD),
                  pl.BlockSpec((nq, A_VT, tq), lambda bi, j: (bi, 0, 0),
                               pipeline_mode=pl.Buffered(1))],
        out_specs=pl.BlockSpec((2 * tq, A_Q), lambda bi, j: (bi * (nq // 2) + j, 0)),
        out_shape=jax.ShapeDtypeStruct((t, A_Q), BF16),
        scratch_shapes=[
            pltpu.VMEM((2, nq, tq, tq), I32),
            pltpu.VMEM((KEY_BITS, nq, SUBLANES, tq), I32),
            pltpu.VMEM((2 * IDX_HEADS, LANES, tq), BF16),
            pltpu.VMEM((2 * A_HEADS, LANES, tq), BF16),
            pltpu.VMEM((2 * A_HEADS, tq, tq), F32),
            pltpu.VMEM((2 * A_HEADS, tq, tq), F32),
            pltpu.VMEM((2 * A_HEADS, SUBLANES, tq), F32),
            pltpu.VMEM((2 * A_HEADS, SUBLANES, tq), F32),
            pltpu.VMEM((2 * A_HEADS, 1, tq), F32),
            pltpu.VMEM((2 * A_HEADS, V_ROWS, tq), F32),
        ],
        compiler_params=_params("parallel", "arbitrary"),
        name="dsa_attention",
    )(qt, qit, wit, ki, k, vt)


B_QK = B_HEADS * HEAD_DIM
B_KAUG = B_HEADS * LANES
B_VT = B_HEADS * V_ROWS
BIAS_PIECES = 3


def _split3(x):
    hi = x.astype(BF16)
    r1 = x - hi.astype(F32)
    mid = r1.astype(BF16)
    low = (r1 - mid.astype(F32)).astype(BF16)
    return hi, mid, low


def _kvproj_kernel(h_ref, g_ref, wk_ref, wv_ref, wf_ref, bf_ref, place_ref, ones_ref,
                   kaug_ref, vt_ref, chi_ref, cmid_ref, clo_ref, carry_sc):
    tm = h_ref.shape[0]
    tk = vt_ref.shape[2]

    @pl.when(pl.program_id(1) == 0)
    def _():
        carry_sc[...] = jnp.zeros(carry_sc.shape, F32)

    xn = _rms(h_ref[...], g_ref[...]).astype(BF16)
    z = jnp.dot(xn, wf_ref[...], preferred_element_type=F32) + bf_ref[...]
    logf = jnp.minimum(z, 0.0) - jnp.log1p(jnp.exp(-jnp.abs(z)))
    tri = (lax.broadcasted_iota(I32, (tm, tm), 0) >=
           lax.broadcasted_iota(I32, (tm, tm), 1)).astype(BF16)
    cs = sum(jnp.dot(tri, piece, preferred_element_type=F32) for piece in _split3(logf))
    cum = cs + carry_sc[...]
    carry_sc[...] = cum[tm - 1:tm, :]
    hi, mid, low = _split3(cum * LOG2E)
    chi_ref[...] = hi
    cmid_ref[...] = mid
    clo_ref[...] = low
    pieces = jnp.concatenate([hi, mid, low], axis=1)

    step = 4 * LANES
    for c0 in range(0, B_KAUG, step):
        y = (jnp.dot(xn, wk_ref[:, c0:c0 + step], preferred_element_type=F32) +
             jnp.dot(pieces, place_ref[:, c0:c0 + step], preferred_element_type=F32) +
             ones_ref[:, c0:c0 + step])
        kaug_ref[:, c0:c0 + step] = y.astype(BF16)
    for c0 in range(0, B_QK, step):
        y = jnp.dot(xn, wv_ref[:, c0:c0 + step], preferred_element_type=F32)
        for jj in range(step // LANES):
            head = (c0 + jj * LANES) // HEAD_DIM
            for sb in range(tm // tk):
                yt = y[sb * tk:(sb + 1) * tk, jj * LANES:(jj + 1) * LANES].T.astype(BF16)
                for e in range(2):
                    _store_value_head(vt_ref, sb, head + e, yt[e * HEAD_DIM:(e + 1) * HEAD_DIM, :])


def _kvproj(h3, g, wk, wv, wf, bf, place, ones, tm, tk):
    b, s, d = h3.shape
    ns = s // tm
    blk = lambda n: pl.BlockSpec((None, tm, n), lambda bi, i: (bi, i, 0))
    return pl.pallas_call(
        _kvproj_kernel,
        grid=(b, ns),
        in_specs=[blk(d), _resident((1, d)), _resident(wk.shape), _resident(wv.shape),
                  _resident(wf.shape), _resident((1, LANES)), _resident(place.shape),
                  _resident(ones.shape)],
        out_specs=[blk(B_KAUG),
                   pl.BlockSpec((tm // tk, B_VT, tk), lambda bi, i: (bi * ns + i, 0, 0)),
                   blk(LANES), blk(LANES), blk(LANES)],
        out_shape=[jax.ShapeDtypeStruct((b, s, B_KAUG), BF16),
                   jax.ShapeDtypeStruct((b * s // tk, B_VT, tk), BF16),
                   jax.ShapeDtypeStruct((b, s, LANES), BF16),
                   jax.ShapeDtypeStruct((b, s, LANES), BF16),
                   jax.ShapeDtypeStruct((b, s, LANES), BF16)],
        scratch_shapes=[pltpu.VMEM((1, LANES), F32)],
        compiler_params=_params("parallel", "arbitrary"),
        name="kv_proj_cumsum",
    )(h3, g, wk, wv, wf, bf, place, ones)


def _qproj_kernel(h_ref, g_ref, w_ref, rows_ref, qt_ref):
    tm = h_ref.shape[0]
    xn = _rms(h_ref[...], g_ref[...]).astype(BF16)
    pad = jnp.zeros((LANES - HEAD_DIM - BF16_ROWS, tm), BF16)
    step = 4 * LANES
    for c0 in range(0, B_QK, step):
        y = jnp.dot(xn, w_ref[:, c0:c0 + step], preferred_element_type=F32) * QK_SCALE
        for jj in range(step // LANES):
            yt = y[:, jj * LANES:(jj + 1) * LANES].T.astype(BF16)
            for e in range(2):
                h = (c0 + jj * LANES) // HEAD_DIM + e
                r = h * LANES
                qt_ref[r:r + HEAD_DIM, :] = yt[e * HEAD_DIM:(e + 1) * HEAD_DIM, :]
                qt_ref[r + HEAD_DIM:r + HEAD_DIM + BF16_ROWS, :] = (
                    rows_ref[h * BF16_ROWS:(h + 1) * BF16_ROWS, :])
                qt_ref[r + HEAD_DIM + BF16_ROWS:r + LANES, :] = pad


def _qproj(h, g, w, rows, tm):
    t, d = h.shape
    return pl.pallas_call(
        _qproj_kernel,
        grid=(t // tm,),
        in_specs=[pl.BlockSpec((tm, d), lambda i: (i, 0)), _resident((1, d)), _resident(w.shape),
                  pl.BlockSpec((B_HEADS * BF16_ROWS, tm), lambda i: (0, i))],
        out_specs=pl.BlockSpec((B_KAUG, tm), lambda i: (0, i)),
        out_shape=jax.ShapeDtypeStruct((B_KAUG, t), BF16),
        compiler_params=_params("parallel"),
        name="q_proj",
    )(h, g, w, rows)


FOX_HEADS_PER_STEP = 8


def _fox_kernel(qt_ref, k_ref, vt_ref, o_ref, sa_sc, sb_sc, mxa_sc, mxb_sc, m_sc, acc_sc, *, tq):
    tk = tq
    hps = FOX_HEADS_PER_STEP
    jj = pl.program_id(2)
    m_sc[...] = jnp.full(m_sc.shape, MASKED, F32)
    acc_sc[...] = jnp.zeros(acc_sc.shape, F32)
    kiota = lax.broadcasted_iota(I32, (tk, tq), 0)
    qiota = lax.broadcasted_iota(I32, (1, tq), 1)
    buf_a = (sa_sc, mxa_sc)
    buf_b = (sb_sc, mxb_sc)

    def scores(c, sub, h, buf, qlimit):
        r0 = pl.multiple_of(c * tk, tk)
        s = jnp.dot(k_ref[pl.ds(r0, tk), h * LANES:(h + 1) * LANES],
                    qt_ref[h * LANES:(h + 1) * LANES, sub * tq:(sub + 1) * tq],
                    preferred_element_type=F32)
        if qlimit is not None:
            s = jnp.where(kiota <= qlimit, s, MASKED)
        buf[0][sub * hps + h] = s
        buf[1][sub * hps + h] = _colmax8(s)

    def update(c, sub, h, buf):
        _softmax_pv(buf[0], buf[1], vt_ref[c, h * V_ROWS:(h + 1) * V_ROWS, :],
                    sub * hps + h, m_sc, acc_sc)

    def stage(c, cur, nxt, limits, update_subs=(0, 1)):
        for h in range(hps):
            for sub, qlimit in limits:
                scores(c + 1, sub, h, nxt, qlimit)
            for sub in update_subs:
                update(c, sub, h, cur)

    first_limit = qiota + jnp.where(jj == 0, 0, tk)
    for h in range(hps):
        scores(0, 0, h, buf_a, first_limit)
        scores(0, 1, h, buf_a, None)

    def main_body(i, carry):
        c = 2 * i
        stage(c, buf_a, buf_b, ((0, None), (1, None)))
        limit0 = qiota + jnp.where(i == jj - 1, 0, tk)
        stage(c + 1, buf_b, buf_a, ((0, limit0), (1, None)))
        return carry

    lax.fori_loop(0, jj, main_body, 0)
    last = 2 * jj
    stage(last, buf_a, buf_b, ((1, qiota),))
    stage(last + 1, buf_b, None, (), update_subs=(1,))
    for sub in range(2):
        outs = [_normalised(acc_sc, sub * hps + h) for h in range(hps)]
        o_ref[sub * tq:(sub + 1) * tq, :] = jnp.concatenate(outs, axis=0).T.astype(o_ref.dtype)


def _fox(qt, kaug, vt, b, s, tq):
    t = b * s
    nq = s // tq
    assert nq % 2 == 0, "fox_attention walks query blocks in pairs"
    hps = FOX_HEADS_PER_STEP
    nstate = 2 * hps
    return pl.pallas_call(
        functools.partial(_fox_kernel, tq=tq),
        grid=(b, B_HEADS // hps, nq // 2),
        in_specs=[
            pl.BlockSpec((hps * LANES, 2 * tq), lambda bi, p, j: (p, bi * (nq // 2) + j)),
            pl.BlockSpec((None, s, hps * LANES), lambda bi, p, j: (bi, 0, p)),
            pl.BlockSpec((nq, hps * V_ROWS, tq), lambda bi, p, j: (bi, p, 0)),
        ],
        out_specs=pl.BlockSpec((2 * tq, hps * HEAD_DIM), lambda bi, p, j: (bi * (nq // 2) + j, p)),
        out_shape=jax.ShapeDtypeStruct((t, B_QK), BF16),
        scratch_shapes=[
            pltpu.VMEM((nstate, tq, tq), F32),
            pltpu.VMEM((nstate, tq, tq), F32),
            pltpu.VMEM((nstate, SUBLANES, tq), F32),
            pltpu.VMEM((nstate, SUBLANES, tq), F32),
            pltpu.VMEM((nstate, 1, tq), F32),
            pltpu.VMEM((nstate, V_ROWS, tq), F32),
        ],
        compiler_params=_params("parallel", "parallel", "arbitrary"),
        name="fox_attention",
    )(qt, kaug, vt)


def _mlp_kernel(h_ref, o_ref, wo_ref, g_ref, wup_ref, wdn_ref, gf_ref, out_ref, *, tf, final):
    h1 = h_ref[...] + jnp.dot(o_ref[...], wo_ref[...], preferred_element_type=F32)
    xn = _rms(h1, g_ref[...]).astype(BF16)
    acc = h1
    for c0 in range(0, wup_ref.shape[1], tf):
        u = jnp.dot(xn, wup_ref[:, c0:c0 + tf], preferred_element_type=F32)
        a = jnp.square(jnp.maximum(u, 0.0)).astype(BF16)
        acc = acc + jnp.dot(a, wdn_ref[c0:c0 + tf, :], preferred_element_type=F32)
    if final:
        acc = _rms(acc, gf_ref[...])
    out_ref[...] = acc


def _mlp(h, o, wo, g, wup, wdn, gf, final, tm, tf):
    t, d = h.shape
    row = pl.BlockSpec((tm, d), lambda i: (i, 0))
    return pl.pallas_call(
        functools.partial(_mlp_kernel, tf=tf, final=final),
        grid=(t // tm,),
        in_specs=[row, row, _resident(wo.shape), _resident((1, d)), _resident(wup.shape),
                  _resident(wdn.shape), _resident((1, d))],
        out_specs=row,
        out_shape=jax.ShapeDtypeStruct((t, d), F32),
        compiler_params=_params("parallel"),
        name="oproj_mlp",
    )(h, o, wo, g, wup, wdn, gf)


def _dup_heads(w, heads):
    d = w.shape[0]
    w = w.reshape(d, heads, HEAD_DIM)
    return jnp.concatenate([w, w], axis=-1).reshape(d, heads * LANES)


def _prep_w_in_a(w):
    kv = A_KV_HEADS * HEAD_DIM
    o_k, o_v, o_qi = A_Q, A_Q + kv, A_Q + 2 * kv
    o_ki, o_wi = o_qi + A_QI, o_qi + A_QI + HEAD_DIM
    wi = w[:, o_wi:o_wi + IDX_HEADS]
    return jnp.concatenate([
        w[:, :A_Q],
        _dup_heads(w[:, o_k:o_v], A_KV_HEADS),
        w[:, o_v:o_qi],
        w[:, o_qi:o_ki],
        _dup_heads(w[:, o_ki:o_wi], 1),
        jnp.pad(wi, ((0, 0), (0, LANES - IDX_HEADS))),
    ], axis=1).astype(BF16)


def _rope_tables(seq):
    inv = 1.0 / (ROPE_THETA ** (jnp.arange(0, HEAD_DIM, 2, dtype=F32) / HEAD_DIM))
    ang = jnp.arange(seq, dtype=F32)[:, None] * inv[None, :]
    c, s = jnp.cos(ang), jnp.sin(ang)
    return jnp.concatenate([c, c, c, c], axis=1), jnp.concatenate([-s, s, -s, s], axis=1)


def _bias_placement():
    place = np.zeros((BIAS_PIECES * LANES, B_KAUG), np.float32)
    ones = np.zeros((1, B_KAUG), np.float32)
    for h in range(B_HEADS):
        for p in range(BIAS_PIECES):
            place[p * LANES + h, h * LANES + HEAD_DIM + p] = -1.0
            ones[0, h * LANES + HEAD_DIM + BIAS_PIECES + p] = 1.0
    return jnp.asarray(place, BF16), jnp.asarray(ones, F32)


def _query_bias_rows(chi, cmid, clo):
    b, s, _ = chi.shape
    t = b * s
    pcs = [x[:, :, :B_HEADS].reshape(t, B_HEADS).T for x in (chi, cmid, clo)]
    one = jnp.ones((B_HEADS, t), BF16)
    zero = jnp.zeros((B_HEADS, t), BF16)
    rows = [one] * BIAS_PIECES + pcs + [zero] * (BF16_ROWS - 2 * BIAS_PIECES)
    return jnp.stack(rows, axis=1).reshape(B_HEADS * BF16_ROWS, t)


def kernel(x, g_attn_a, w_in_a, w_o_a, g_kv, w_kv_b, b_f, g_attn_b, w_q_b, w_o_b,
           g_mlp, w_up, w_down, g_final):
    b, s, d = x.shape
    t = b * s
    depth = g_mlp.shape[0]
    n_a = g_attn_a.shape[0]
    tm = min(512, s)
    tq = min(ATTN_BLOCK, s)
    tf = 512

    cos, sin = _rope_tables(s)
    gf = g_final.reshape(1, d)
    h = x.reshape(t, d)
    kaug = vt_b = bias_rows = None

    for layer in range(depth):
        if layer < n_a:
            qt, k, vt, qit, ki, wit = _aproj(h, g_attn_a[layer].reshape(1, d),
                                             _prep_w_in_a(w_in_a[layer]), cos, sin, s, tm, tq)
            o = _dsa(qt, qit, wit, ki, k, vt, b, s, tq)
            w_o = w_o_a[layer]
        else:
            jb = layer - n_a
            if jb == 0:
                wk = jnp.pad(w_kv_b[:, :B_QK].reshape(d, B_HEADS, HEAD_DIM),
                             ((0, 0), (0, 0), (0, LANES - HEAD_DIM))).reshape(d, B_KAUG)
                wf = jnp.pad(w_kv_b[:, 2 * B_QK:], ((0, 0), (0, LANES - B_HEADS)))
                bias = jnp.pad(b_f, (0, LANES - B_HEADS)).reshape(1, LANES)
                place, ones = _bias_placement()
                kaug, vt_b, chi, cmid, clo = _kvproj(
                    h.reshape(b, s, d), g_kv.reshape(1, d), wk.astype(BF16),
                    w_kv_b[:, B_QK:2 * B_QK].astype(BF16), wf.astype(BF16), bias,
                    place, ones, tm, tq)
                bias_rows = _query_bias_rows(chi, cmid, clo)
            qt = _qproj(h, g_attn_b[jb].reshape(1, d), w_q_b[jb].astype(BF16), bias_rows, tm)
            o = _fox(qt, kaug, vt_b, b, s, tq)
            w_o = w_o_b[jb]
        h = _mlp(h, o, w_o.astype(BF16), g_mlp[layer].reshape(1, d),
                 w_up[layer].astype(BF16), w_down[layer].astype(BF16), gf,
                 layer == depth - 1, tm, tf)
    return h.reshape(b, s, d)
```

```python
import functools

import numpy as np
import jax
import jax.numpy as jnp
from jax import lax
from jax.experimental import pallas as pl
from jax.experimental.pallas import tpu as pltpu

LANES = 128
SUBLANES = 8
BF16_ROWS = 16
HEAD_DIM = 64
ROPE_HALF = HEAD_DIM // 2
CHUNK = 64
A_HEADS = 16
A_KV_HEADS = 4
A_GROUP = A_HEADS // A_KV_HEADS
IDX_HEADS = 8
TOPK_MAX = 256
B_HEADS = 16
ROPE_THETA = 10000.0
EPS = 1e-6
INT_MIN = -(2 ** 31)
MASKED = -1e30
LOG2E = 1.4426950408889634
QK_SCALE = HEAD_DIM ** -0.5 * LOG2E
VMEM_LIMIT_BYTES = 56 * 1024 * 1024
ATTN_BLOCK = 256

F32 = jnp.float32
BF16 = jnp.bfloat16
I32 = jnp.int32


def _params(*sem):
    return pltpu.CompilerParams(dimension_semantics=sem,
                                vmem_limit_bytes=VMEM_LIMIT_BYTES)


def _resident(shape):
    nd = len(shape)
    return pl.BlockSpec(shape, lambda *_: (0,) * nd, pipeline_mode=pl.Buffered(1))


def _rms(x, g):
    return x * lax.rsqrt(jnp.mean(x * x, axis=-1, keepdims=True) + EPS) * g


def _colsum8(x):
    n, w = x.shape
    return x.reshape(n // SUBLANES, SUBLANES, w).sum(axis=0)


def _colmax8(x):
    n, w = x.shape
    return x.reshape(n // SUBLANES, SUBLANES, w).max(axis=0)


V_ROWS = HEAD_DIM + BF16_ROWS


def _store_value_head(vt_ref, sb, head, yt):
    r = head * V_ROWS
    tk = yt.shape[1]
    vt_ref[sb, r:r + HEAD_DIM, :] = yt
    first_row = lax.broadcasted_iota(I32, (BF16_ROWS, tk), 0) == 0
    vt_ref[sb, r + HEAD_DIM:r + V_ROWS, :] = jnp.where(first_row, 1.0, 0.0).astype(BF16)


def _softmax_pv(s_buf, mx_buf, vt, h, m_sc, acc_sc):
    m_old = m_sc[h]
    m_new = jnp.maximum(m_old, jnp.max(mx_buf[h], axis=0, keepdims=True))
    alpha = jnp.exp2(m_old - m_new)
    p = jnp.exp2(s_buf[h] - m_new)
    acc_sc[h] = alpha * acc_sc[h] + jnp.dot(vt, p.astype(BF16), preferred_element_type=F32)
    m_sc[h] = m_new


def _normalised(acc_sc, h):
    acc = acc_sc[h]
    return acc[0:HEAD_DIM, :] / acc[HEAD_DIM:HEAD_DIM + 1, :]


A_Q = A_HEADS * HEAD_DIM
A_KD = A_KV_HEADS * LANES
A_V = A_KV_HEADS * HEAD_DIM
A_VT = A_KV_HEADS * V_ROWS
A_QI = IDX_HEADS * HEAD_DIM
A_OFF_K = A_Q
A_OFF_V = A_OFF_K + A_KD
A_OFF_QI = A_OFF_V + A_V
A_OFF_KI = A_OFF_QI + A_QI
A_OFF_WI = A_OFF_KI + LANES
A_COLS = A_OFF_WI + LANES


def _aproj_kernel(h_ref, g_ref, w_ref, cos_ref, sin_ref,
                  qt_ref, k_ref, vt_ref, qit_ref, ki_ref, wit_ref, *, wi_scale):
    tm = h_ref.shape[0]
    tk = vt_ref.shape[2]
    xn = _rms(h_ref[...], g_ref[...]).astype(BF16)
    cos = cos_ref[...]
    sin = sin_ref[...]
    lane = lax.broadcasted_iota(I32, (tm, LANES), 1)
    first_half = (lane & ROPE_HALF) == 0

    def rope(y):
        ys = jnp.where(first_half, pltpu.roll(y, LANES - ROPE_HALF, 1),
                       pltpu.roll(y, ROPE_HALF, 1))
        return y * cos + ys * sin

    def chunks(c0, n):
        step = 4 * LANES
        for s0 in range(0, n, step):
            sn = min(step, n - s0)
            y = jnp.dot(xn, w_ref[:, c0 + s0:c0 + s0 + sn], preferred_element_type=F32)
            for j in range(sn // LANES):
                yield (s0 // LANES + j), y[:, j * LANES:(j + 1) * LANES]

    for j, y in chunks(0, A_Q):
        qt_ref[j * LANES:(j + 1) * LANES, :] = (rope(y) * QK_SCALE).T.astype(BF16)
    for j, y in chunks(A_OFF_K, A_KD):
        k_ref[:, j * LANES:(j + 1) * LANES] = rope(y).astype(BF16)
    for j, y in chunks(A_OFF_V, A_V):
        for sb in range(tm // tk):
            yt = y[sb * tk:(sb + 1) * tk].T.astype(BF16)
            for e in range(2):
                _store_value_head(vt_ref, sb, 2 * j + e, yt[e * HEAD_DIM:(e + 1) * HEAD_DIM, :])
    for j, y in chunks(A_OFF_QI, A_QI):
        qit_ref[j * LANES:(j + 1) * LANES, :] = rope(y).T.astype(BF16)
    for j, y in chunks(A_OFF_KI, LANES):
        ki_ref[...] = rope(y).astype(BF16)
    for j, y in chunks(A_OFF_WI, LANES):
        wit_ref[...] = (y * wi_scale).T[:IDX_HEADS, :]


def _aproj(h, g, w, cos, sin, seq, tm, tk):
    t, d = h.shape
    npos = seq // tm
    row = lambda n: pl.BlockSpec((tm, n), lambda i: (i, 0))
    col = lambda n: pl.BlockSpec((n, tm), lambda i: (0, i))
    tab = pl.BlockSpec((tm, LANES), lambda i: (i % npos, 0))
    wi_scale = IDX_HEADS ** -0.5 * HEAD_DIM ** -0.5
    return pl.pallas_call(
        functools.partial(_aproj_kernel, wi_scale=wi_scale),
        grid=(t // tm,),
        in_specs=[row(d), _resident((1, d)), _resident((d, A_COLS)), tab, tab],
        out_specs=[col(A_Q), row(A_KD),
                   pl.BlockSpec((tm // tk, A_VT, tk), lambda i: (i, 0, 0)),
                   col(A_QI), row(LANES), col(IDX_HEADS)],
        out_shape=[jax.ShapeDtypeStruct((A_Q, t), BF16),
                   jax.ShapeDtypeStruct((t, A_KD), BF16),
                   jax.ShapeDtypeStruct((t // tk, A_VT, tk), BF16),
                   jax.ShapeDtypeStruct((A_QI, t), BF16),
                   jax.ShapeDtypeStruct((t, LANES), BF16),
                   jax.ShapeDtypeStruct((IDX_HEADS, t), F32)],
        compiler_params=_params("parallel"),
        name="a_proj",
    )(h, g, w, cos, sin)


KEY_BITS = 32
BIT_TRANSPOSE_STEPS = (16, 8, 4, 2, 1)
BIT_TRANSPOSE_MASKS = (0x0000FFFF, 0x00FF00FF, 0x0F0F0F0F, 0x33333333, 0x55555555)


def _dsa_kernel(qt_ref, qit_ref, wit_ref, ki_ref, k_ref, vt_ref, o_ref,
                key_sc, planes_sc, qiw_sc, qw_sc, sa_sc, sb_sc, mxa_sc, mxb_sc, m_sc, acc_sc,
                *, tq, n_sel, idx_bits):
    tk = tq
    jj = pl.program_id(1)
    kiota = lax.broadcasted_iota(I32, (tk, tq), 0)
    zeros_half = jnp.zeros((HEAD_DIM, tq), BF16)

    for sub in range(2):
        cols = slice(sub * tq, (sub + 1) * tq)
        for h in range(IDX_HEADS):
            qiw_sc[sub * IDX_HEADS + h, 0:HEAD_DIM, :] = qit_ref[h * HEAD_DIM:(h + 1) * HEAD_DIM, cols]
            qiw_sc[sub * IDX_HEADS + h, HEAD_DIM:LANES, :] = zeros_half
        for h in range(A_HEADS):
            qw_sc[sub * A_HEADS + h, 0:HEAD_DIM, :] = qt_ref[h * HEAD_DIM:(h + 1) * HEAD_DIM, cols]
            qw_sc[sub * A_HEADS + h, HEAD_DIM:LANES, :] = zeros_half

    def select(sub):
        j = 2 * jj + sub
        nkb = j + 1
        keys = key_sc.at[sub]
        qpos = j * tq + lax.broadcasted_iota(I32, (1, tq), 1)
        limit = (qpos // CHUNK + 1) * CHUNK

        def head_logits(c):
            kc = ki_ref[pl.ds(pl.multiple_of(c * tk, tk), tk), :]
            return [jnp.dot(kc, qiw_sc[sub * IDX_HEADS + h], preferred_element_type=F32)
                    for h in range(IDX_HEADS)]

        def store_keys(c, logits):
            sc = jnp.zeros((tk, tq), F32)
            for h in range(IDX_HEADS):
                sc = sc + jnp.maximum(logits[h], 0.0) * wit_ref[h:h + 1, sub * tq:(sub + 1) * tq]
            sc = jnp.where(sc == 0.0, 0.0, sc)
            bits = pltpu.bitcast(sc, I32)
            key = bits ^ ((bits >> 31) & 0x7FFFFFFF)
            keys[c] = jnp.where(c * tk + kiota < limit, key, INT_MIN)

        def bit_slice(c):
            x = keys[c] ^ INT_MIN
            tiles = [x[v * SUBLANES:(v + 1) * SUBLANES, :] for v in range(KEY_BITS)]
            for jbit, mask in zip(BIT_TRANSPOSE_STEPS, BIT_TRANSPOSE_MASKS):
                for v in range(KEY_BITS):
                    if v & jbit == 0:
                        lo, hi = tiles[v], tiles[v + jbit]
                        t = (lax.shift_right_logical(lo, jbit) ^ hi) & mask
                        tiles[v] = lo ^ jnp.left_shift(t, jbit)
                        tiles[v + jbit] = hi ^ t
            for b in range(KEY_BITS):
                planes_sc[b, c] = tiles[b]

        store_keys(0, head_logits(0))

        def score_body(c, carry):
            logits = head_logits(c)
            bit_slice(c - 1)
            store_keys(c, logits)
            return carry

        lax.fori_loop(1, nkb, score_body, 0)
        bit_slice(nkb - 1)

        def clear_body(c, carry):
            planes_sc[:, c] = jnp.zeros((KEY_BITS, SUBLANES, tq), I32)
            return carry

        lax.fori_loop(nkb, planes_sc.shape[1], clear_body, 0)

        @pl.when(nkb % 2 == 1)
        def _():
            keys[nkb] = jnp.full((tk, tq), INT_MIN, I32)

        def count(pred):
            def body(cc, acc):
                for u in range(2):
                    c = 2 * cc + u
                    acc = acc + _colsum8(jnp.where(pred(c, keys[c]), 1.0, 0.0))
                return acc
            acc = lax.fori_loop(0, (nkb + 1) // 2, body, jnp.zeros((SUBLANES, tq), F32))
            return jnp.sum(acc, axis=0, keepdims=True)

        nwords = planes_sc.shape[1] * SUBLANES
        word_block = lax.broadcasted_iota(I32, (nwords, 1), 0) // SUBLANES
        cand = jnp.broadcast_to(jnp.where(word_block < nkb, -1, 0), (nwords, tq))

        def ones_count(words):
            cnt = lax.population_count(words).reshape(nwords // SUBLANES, SUBLANES, tq).sum(axis=0)
            return jnp.sum(cnt.astype(F32), axis=0, keepdims=True)

        thr_u = jnp.zeros((1, tq), I32)
        above = jnp.zeros((1, tq), F32)
        for b in range(KEY_BITS - 1, -1, -1):
            plane = planes_sc[b].reshape(nwords, tq)
            ones = cand & plane
            cnt1 = ones_count(ones)
            take = above + cnt1 >= n_sel
            thr_u = jnp.where(take, thr_u | (INT_MIN if b == KEY_BITS - 1 else 1 << b), thr_u)
            above = jnp.where(take, above, above + cnt1)
            cand = jnp.where(take, ones, cand ^ ones)
        cnt_thr = jnp.where(thr_u == 0, 0.0, above + ones_count(cand))
        thr = jnp.maximum(thr_u ^ INT_MIN, INT_MIN + 1)

        @pl.when(jnp.max(cnt_thr) > n_sel)
        def _():
            need = n_sel - count(lambda c, key: key > thr)
            pos = jnp.zeros((1, tq), I32)
            for b in range(idx_bits - 1, -1, -1):
                cand = pos | (1 << b)
                cnt = count(lambda c, key: jnp.where(key == thr, c * tk + kiota, 2 ** 30) < cand)
                pos = jnp.where(cnt < need, cand, pos)

            def demote(c, carry):
                key = keys[c]
                tie_pos = jnp.where(key == thr, c * tk + kiota, -1)
                keys[c] = jnp.where(tie_pos > pos, INT_MIN, key)
                return carry

            lax.fori_loop(0, nkb, demote, 0)

        return thr

    thrs = [select(0), select(1)]

    m_sc[...] = jnp.full(m_sc.shape, MASKED, F32)
    acc_sc[...] = jnp.zeros(acc_sc.shape, F32)
    buf_a = (sa_sc, mxa_sc)
    buf_b = (sb_sc, mxb_sc)

    def select_bias(c, sub):
        return jnp.where(key_sc[sub, c] >= thrs[sub], 0.0, MASKED)

    def scores(c, sub, h, buf, bias):
        n = h // A_GROUP
        r0 = pl.multiple_of(c * tk, tk)
        s = jnp.dot(k_ref[pl.ds(r0, tk), n * LANES:(n + 1) * LANES], qw_sc[sub * A_HEADS + h],
                    preferred_element_type=F32) + bias
        buf[0][sub * A_HEADS + h] = s
        buf[1][sub * A_HEADS + h] = _colmax8(s)

    def update(c, sub, h, buf):
        n = h // A_GROUP
        _softmax_pv(buf[0], buf[1], vt_ref[c, n * V_ROWS:(n + 1) * V_ROWS, :],
                    sub * A_HEADS + h, m_sc, acc_sc)

    def stage(c, cur, nxt, score_subs, update_subs=(0, 1)):
        biases = {sub: select_bias(c + 1, sub) for sub in score_subs}
        for h in range(A_HEADS):
            for sub in score_subs:
                scores(c + 1, sub, h, nxt, biases[sub])
            for sub in update_subs:
                update(c, sub, h, cur)

    for sub in range(2):
        bias0 = select_bias(0, sub)
        for h in range(A_HEADS):
            scores(0, sub, h, buf_a, bias0)

    def attn_body(i, carry):
        stage(2 * i, buf_a, buf_b, (0, 1))
        stage(2 * i + 1, buf_b, buf_a, (0, 1))
        return carry

    lax.fori_loop(0, jj, attn_body, 0)
    last = 2 * jj
    stage(last, buf_a, buf_b, (1,))
    stage(last + 1, buf_b, None, (), update_subs=(1,))

    for sub in range(2):
        for i in range(A_HEADS // 2):
            pair = [_normalised(acc_sc, sub * A_HEADS + h) for h in (2 * i, 2 * i + 1)]
            o_ref[sub * tq:(sub + 1) * tq, i * LANES:(i + 1) * LANES] = (
                jnp.concatenate(pair, axis=0).T.astype(o_ref.dtype))


def _dsa(qt, qit, wit, ki, k, vt, b, s, tq):
    t = b * s
    nq = s // tq
    assert nq % 2 == 0, "dsa_attention walks query blocks in pairs"
    n_sel = min(TOPK_MAX, s // 4)
    idx_bits = max(1, (s - 1).bit_length())
    assert tq == KEY_BITS * SUBLANES, "the bit-sliced search transposes 32 sublane tiles per key block"
    qcol = lambda n: pl.BlockSpec((n, 2 * tq), lambda bi, j: (0, bi * (nq // 2) + j))
    full = lambda n: pl.BlockSpec((s, n), lambda bi, j: (bi, 0), pipeline_mode=pl.Buffered(1))
    return pl.pallas_call(
        functools.partial(_dsa_kernel, tq=tq, n_sel=float(n_sel), idx_bits=idx_bits),
        grid=(b, nq // 2),
        in_specs=[qcol(A_Q), qcol(A_QI), qcol(IDX_HEADS), full(LANES), full(A_KD),
                  pl.BlockSpec((nq, A_VT, tq), lambda bi, j: (bi, 0, 0),
                               pipeline_mode=pl.Buffered(1))],
        out_specs=pl.BlockSpec((2 * tq, A_Q), lambda bi, j: (bi * (nq // 2) + j, 0)),
        out_shape=jax.ShapeDtypeStruct((t, A_Q), BF16),
        scratch_shapes=[
            pltpu.VMEM((2, nq, tq, tq), I32),
            pltpu.VMEM((KEY_BITS, nq, SUBLANES, tq), I32),
            pltpu.VMEM((2 * IDX_HEADS, LANES, tq), BF16),
            pltpu.VMEM((2 * A_HEADS, LANES, tq), BF16),
            pltpu.VMEM((2 * A_HEADS, tq, tq), F32),
            pltpu.VMEM((2 * A_HEADS, tq, tq), F32),
            pltpu.VMEM((2 * A_HEADS, SUBLANES, tq), F32),
            pltpu.VMEM((2 * A_HEADS, SUBLANES, tq), F32),
            pltpu.VMEM((2 * A_HEADS, 1, tq), F32),
            pltpu.VMEM((2 * A_HEADS, V_ROWS, tq), F32),
        ],
        compiler_params=_params("parallel", "arbitrary"),
        name="dsa_attention",
    )(qt, qit, wit, ki, k, vt)


B_QK = B_HEADS * HEAD_DIM
B_KAUG = B_HEADS * LANES
B_VT = B_HEADS * V_ROWS
BIAS_PIECES = 3


def _split3(x):
    hi = x.astype(BF16)
    r1 = x - hi.astype(F32)
    mid = r1.astype(BF16)
    low = (r1 - mid.astype(F32)).astype(BF16)
    return hi, mid, low


def _kvproj_kernel(h_ref, g_ref, wk_ref, wv_ref, wf_ref, bf_ref, place_ref, ones_ref,
                   qplace_ref, qones_ref, kaug_ref, vt_ref, qrows_ref, carry_sc):
    tm = h_ref.shape[0]
    tk = vt_ref.shape[2]

    @pl.when(pl.program_id(1) == 0)
    def _():
        carry_sc[...] = jnp.zeros(carry_sc.shape, F32)

    xn = _rms(h_ref[...], g_ref[...]).astype(BF16)
    z = jnp.dot(xn, wf_ref[...], preferred_element_type=F32) + bf_ref[...]
    logf = jnp.minimum(z, 0.0) - jnp.log1p(jnp.exp(-jnp.abs(z)))
    tri = (lax.broadcasted_iota(I32, (tm, tm), 0) >=
           lax.broadcasted_iota(I32, (tm, tm), 1)).astype(BF16)
    cs = sum(jnp.dot(tri, piece, preferred_element_type=F32) for piece in _split3(logf))
    cum = cs + carry_sc[...]
    carry_sc[...] = cum[tm - 1:tm, :]
    hi, mid, low = _split3(cum * LOG2E)
    pieces = jnp.concatenate([hi, mid, low], axis=1)
    qrows = lax.dot_general(qplace_ref[...], pieces, (((1,), (1,)), ((), ())),
                            preferred_element_type=F32) + qones_ref[...]
    qrows_ref[...] = qrows.astype(BF16)

    step = 4 * LANES
    for c0 in range(0, B_KAUG, step):
        y = (jnp.dot(xn, wk_ref[:, c0:c0 + step], preferred_element_type=F32) +
             jnp.dot(pieces, place_ref[:, c0:c0 + step], preferred_element_type=F32) +
             ones_ref[:, c0:c0 + step])
        kaug_ref[:, c0:c0 + step] = y.astype(BF16)
    for c0 in range(0, B_QK, step):
        y = jnp.dot(xn, wv_ref[:, c0:c0 + step], preferred_element_type=F32)
        for jj in range(step // LANES):
            head = (c0 + jj * LANES) // HEAD_DIM
            for sb in range(tm // tk):
                yt = y[sb * tk:(sb + 1) * tk, jj * LANES:(jj + 1) * LANES].T.astype(BF16)
                for e in range(2):
                    _store_value_head(vt_ref, sb, head + e, yt[e * HEAD_DIM:(e + 1) * HEAD_DIM, :])


def _kvproj(h3, g, wk, wv, wf, bf, place, ones, qplace, qones, tm, tk):
    b, s, d = h3.shape
    ns = s // tm
    blk = lambda n: pl.BlockSpec((None, tm, n), lambda bi, i: (bi, i, 0))
    return pl.pallas_call(
        _kvproj_kernel,
        grid=(b, ns),
        in_specs=[blk(d), _resident((1, d)), _resident(wk.shape), _resident(wv.shape),
                  _resident(wf.shape), _resident((1, LANES)), _resident(place.shape),
                  _resident(ones.shape), _resident(qplace.shape), _resident(qones.shape)],
        out_specs=[blk(B_KAUG),
                   pl.BlockSpec((tm // tk, B_VT, tk), lambda bi, i: (bi * ns + i, 0, 0)),
                   pl.BlockSpec((B_HEADS * BF16_ROWS, tm), lambda bi, i: (0, bi * ns + i))],
        out_shape=[jax.ShapeDtypeStruct((b, s, B_KAUG), BF16),
                   jax.ShapeDtypeStruct((b * s // tk, B_VT, tk), BF16),
                   jax.ShapeDtypeStruct((B_HEADS * BF16_ROWS, b * s), BF16)],
        scratch_shapes=[pltpu.VMEM((1, LANES), F32)],
        compiler_params=_params("parallel", "arbitrary"),
        name="kv_proj_cumsum",
    )(h3, g, wk, wv, wf, bf, place, ones, qplace, qones)


def _qproj_kernel(h_ref, g_ref, w_ref, rows_ref, qt_ref):
    tm = h_ref.shape[0]
    xn = _rms(h_ref[...], g_ref[...]).astype(BF16)
    pad = jnp.zeros((LANES - HEAD_DIM - BF16_ROWS, tm), BF16)
    step = 4 * LANES
    for c0 in range(0, B_QK, step):
        y = jnp.dot(xn, w_ref[:, c0:c0 + step], preferred_element_type=F32) * QK_SCALE
        for jj in range(step // LANES):
            yt = y[:, jj * LANES:(jj + 1) * LANES].T.astype(BF16)
            for e in range(2):
                h = (c0 + jj * LANES) // HEAD_DIM + e
                r = h * LANES
                qt_ref[r:r + HEAD_DIM, :] = yt[e * HEAD_DIM:(e + 1) * HEAD_DIM, :]
                qt_ref[r + HEAD_DIM:r + HEAD_DIM + BF16_ROWS, :] = (
                    rows_ref[h * BF16_ROWS:(h + 1) * BF16_ROWS, :])
                qt_ref[r + HEAD_DIM + BF16_ROWS:r + LANES, :] = pad


def _qproj(h, g, w, rows, tm):
    t, d = h.shape
    return pl.pallas_call(
        _qproj_kernel,
        grid=(t // tm,),
        in_specs=[pl.BlockSpec((tm, d), lambda i: (i, 0)), _resident((1, d)), _resident(w.shape),
                  pl.BlockSpec((B_HEADS * BF16_ROWS, tm), lambda i: (0, i))],
        out_specs=pl.BlockSpec((B_KAUG, tm), lambda i: (0, i)),
        out_shape=jax.ShapeDtypeStruct((B_KAUG, t), BF16),
        compiler_params=_params("parallel"),
        name="q_proj",
    )(h, g, w, rows)


FOX_HEADS_PER_STEP = 8


def _fox_kernel(qt_ref, k_ref, vt_ref, o_ref, sa_sc, sb_sc, mxa_sc, mxb_sc, m_sc, acc_sc, *, tq):
    tk = tq
    hps = FOX_HEADS_PER_STEP
    jj = pl.program_id(2)
    m_sc[...] = jnp.full(m_sc.shape, MASKED, F32)
    acc_sc[...] = jnp.zeros(acc_sc.shape, F32)
    kiota = lax.broadcasted_iota(I32, (tk, tq), 0)
    qiota = lax.broadcasted_iota(I32, (1, tq), 1)
    buf_a = (sa_sc, mxa_sc)
    buf_b = (sb_sc, mxb_sc)

    def scores(c, sub, h, buf, qlimit):
        r0 = pl.multiple_of(c * tk, tk)
        s = jnp.dot(k_ref[pl.ds(r0, tk), h * LANES:(h + 1) * LANES],
                    qt_ref[h * LANES:(h + 1) * LANES, sub * tq:(sub + 1) * tq],
                    preferred_element_type=F32)
        if qlimit is not None:
            s = jnp.where(kiota <= qlimit, s, MASKED)
        buf[0][sub * hps + h] = s
        buf[1][sub * hps + h] = _colmax8(s)

    def update(c, sub, h, buf):
        _softmax_pv(buf[0], buf[1], vt_ref[c, h * V_ROWS:(h + 1) * V_ROWS, :],
                    sub * hps + h, m_sc, acc_sc)

    def stage(c, cur, nxt, limits, update_subs=(0, 1)):
        for h in range(hps):
            for sub, qlimit in limits:
                scores(c + 1, sub, h, nxt, qlimit)
            for sub in update_subs:
                update(c, sub, h, cur)

    first_limit = qiota + jnp.where(jj == 0, 0, tk)
    for h in range(hps):
        scores(0, 0, h, buf_a, first_limit)
        scores(0, 1, h, buf_a, None)

    def main_body(i, carry):
        c = 2 * i
        stage(c, buf_a, buf_b, ((0, None), (1, None)))
        limit0 = qiota + jnp.where(i == jj - 1, 0, tk)
        stage(c + 1, buf_b, buf_a, ((0, limit0), (1, None)))
        return carry

    lax.fori_loop(0, jj, main_body, 0)
    last = 2 * jj
    stage(last, buf_a, buf_b, ((1, qiota),))
    stage(last + 1, buf_b, None, (), update_subs=(1,))
    for sub in range(2):
        outs = [_normalised(acc_sc, sub * hps + h) for h in range(hps)]
        o_ref[sub * tq:(sub + 1) * tq, :] = jnp.concatenate(outs, axis=0).T.astype(o_ref.dtype)


def _fox(qt, kaug, vt, b, s, tq):
    t = b * s
    nq = s // tq
    assert nq % 2 == 0, "fox_attention walks query blocks in pairs"
    hps = FOX_HEADS_PER_STEP
    nstate = 2 * hps
    return pl.pallas_call(
        functools.partial(_fox_kernel, tq=tq),
        grid=(b, B_HEADS // hps, nq // 2),
        in_specs=[
            pl.BlockSpec((hps * LANES, 2 * tq), lambda bi, p, j: (p, bi * (nq // 2) + j)),
            pl.BlockSpec((None, s, hps * LANES), lambda bi, p, j: (bi, 0, p)),
            pl.BlockSpec((nq, hps * V_ROWS, tq), lambda bi, p, j: (bi, p, 0)),
        ],
        out_specs=pl.BlockSpec((2 * tq, hps * HEAD_DIM), lambda bi, p, j: (bi * (nq // 2) + j, p)),
        out_shape=jax.ShapeDtypeStruct((t, B_QK), BF16),
        scratch_shapes=[
            pltpu.VMEM((nstate, tq, tq), F32),
            pltpu.VMEM((nstate, tq, tq), F32),
            pltpu.VMEM((nstate, SUBLANES, tq), F32),
            pltpu.VMEM((nstate, SUBLANES, tq), F32),
            pltpu.VMEM((nstate, 1, tq), F32),
            pltpu.VMEM((nstate, V_ROWS, tq), F32),
        ],
        compiler_params=_params("parallel", "parallel", "arbitrary"),
        name="fox_attention",
    )(qt, kaug, vt)


def _mlp_kernel(h_ref, o_ref, wo_ref, g_ref, wup_ref, wdn_ref, gf_ref, out_ref, *, tf, final):
    h1 = h_ref[...] + jnp.dot(o_ref[...], wo_ref[...], preferred_element_type=F32)
    xn = _rms(h1, g_ref[...]).astype(BF16)
    acc = h1
    for c0 in range(0, wup_ref.shape[1], tf):
        u = jnp.dot(xn, wup_ref[:, c0:c0 + tf], preferred_element_type=F32)
        a = jnp.square(jnp.maximum(u, 0.0)).astype(BF16)
        acc = acc + jnp.dot(a, wdn_ref[c0:c0 + tf, :], preferred_element_type=F32)
    if final:
        acc = _rms(acc, gf_ref[...])
    out_ref[...] = acc


def _mlp(h, o, wo, g, wup, wdn, gf, final, tm, tf):
    t, d = h.shape
    row = pl.BlockSpec((tm, d), lambda i: (i, 0))
    return pl.pallas_call(
        functools.partial(_mlp_kernel, tf=tf, final=final),
        grid=(t // tm,),
        in_specs=[row, row, _resident(wo.shape), _resident((1, d)), _resident(wup.shape),
                  _resident(wdn.shape), _resident((1, d))],
        out_specs=row,
        out_shape=jax.ShapeDtypeStruct((t, d), F32),
        compiler_params=_params("parallel"),
        name="oproj_mlp",
    )(h, o, wo, g, wup, wdn, gf)


def _dup_heads(w, heads):
    d = w.shape[0]
    w = w.reshape(d, heads, HEAD_DIM)
    return jnp.concatenate([w, w], axis=-1).reshape(d, heads * LANES)


def _prep_w_in_a(w):
    kv = A_KV_HEADS * HEAD_DIM
    o_k, o_v, o_qi = A_Q, A_Q + kv, A_Q + 2 * kv
    o_ki, o_wi = o_qi + A_QI, o_qi + A_QI + HEAD_DIM
    wi = w[:, o_wi:o_wi + IDX_HEADS]
    return jnp.concatenate([
        w[:, :A_Q],
        _dup_heads(w[:, o_k:o_v], A_KV_HEADS),
        w[:, o_v:o_qi],
        w[:, o_qi:o_ki],
        _dup_heads(w[:, o_ki:o_wi], 1),
        jnp.pad(wi, ((0, 0), (0, LANES - IDX_HEADS))),
    ], axis=1).astype(BF16)


def _rope_tables(seq):
    inv = 1.0 / (ROPE_THETA ** (jnp.arange(0, HEAD_DIM, 2, dtype=F32) / HEAD_DIM))
    ang = jnp.arange(seq, dtype=F32)[:, None] * inv[None, :]
    c, s = jnp.cos(ang), jnp.sin(ang)
    return jnp.concatenate([c, c, c, c], axis=1), jnp.concatenate([-s, s, -s, s], axis=1)


def _bias_placement():
    place = np.zeros((BIAS_PIECES * LANES, B_KAUG), np.float32)
    ones = np.zeros((1, B_KAUG), np.float32)
    for h in range(B_HEADS):
        for p in range(BIAS_PIECES):
            place[p * LANES + h, h * LANES + HEAD_DIM + p] = -1.0
            ones[0, h * LANES + HEAD_DIM + BIAS_PIECES + p] = 1.0
    return jnp.asarray(place, BF16), jnp.asarray(ones, F32)


def _query_bias_placement(tm):
    qplace = np.zeros((B_HEADS * BF16_ROWS, BIAS_PIECES * LANES), np.float32)
    qones = np.zeros((B_HEADS * BF16_ROWS, tm), np.float32)
    for h in range(B_HEADS):
        for p in range(BIAS_PIECES):
            qplace[h * BF16_ROWS + BIAS_PIECES + p, p * LANES + h] = 1.0
            qones[h * BF16_ROWS + p, :] = 1.0
    return jnp.asarray(qplace, BF16), jnp.asarray(qones, F32)


def kernel(x, g_attn_a, w_in_a, w_o_a, g_kv, w_kv_b, b_f, g_attn_b, w_q_b, w_o_b,
           g_mlp, w_up, w_down, g_final):
    b, s, d = x.shape
    t = b * s
    depth = g_mlp.shape[0]
    n_a = g_attn_a.shape[0]
    tm = min(512, s)
    tq = min(ATTN_BLOCK, s)
    tf = 512

    cos, sin = _rope_tables(s)
    gf = g_final.reshape(1, d)
    h = x.reshape(t, d)
    kaug = vt_b = bias_rows = None

    for layer in range(depth):
        if layer < n_a:
            qt, k, vt, qit, ki, wit = _aproj(h, g_attn_a[layer].reshape(1, d),
                                             _prep_w_in_a(w_in_a[layer]), cos, sin, s, tm, tq)
            o = _dsa(qt, qit, wit, ki, k, vt, b, s, tq)
            w_o = w_o_a[layer]
        else:
            jb = layer - n_a
            if jb == 0:
                wk = jnp.pad(w_kv_b[:, :B_QK].reshape(d, B_HEADS, HEAD_DIM),
                             ((0, 0), (0, 0), (0, LANES - HEAD_DIM))).reshape(d, B_KAUG)
                wf = jnp.pad(w_kv_b[:, 2 * B_QK:], ((0, 0), (0, LANES - B_HEADS)))
                bias = jnp.pad(b_f, (0, LANES - B_HEADS)).reshape(1, LANES)
                place, ones = _bias_placement()
                qplace, qones = _query_bias_placement(tm)
                kaug, vt_b, bias_rows = _kvproj(
                    h.reshape(b, s, d), g_kv.reshape(1, d), wk.astype(BF16),
                    w_kv_b[:, B_QK:2 * B_QK].astype(BF16), wf.astype(BF16), bias,
                    place, ones, qplace, qones, tm, tq)
            qt = _qproj(h, g_attn_b[jb].reshape(1, d), w_q_b[jb].astype(BF16), bias_rows, tm)
            o = _fox(qt, kaug, vt_b, b, s, tq)
            w_o = w_o_b[jb]
        h = _mlp(h, o, w_o.astype(BF16), g_mlp[layer].reshape(1, d),
                 w_up[layer].astype(BF16), w_down[layer].astype(BF16), gf,
                 layer == depth - 1, tm, tf)
    return h.reshape(b, s, d)
```

```python
import functools

import numpy as np
import jax
import jax.numpy as jnp
from jax import lax
from jax.experimental import pallas as pl
from jax.experimental.pallas import tpu as pltpu

LANES = 128
SUBLANES = 8
BF16_ROWS = 16
HEAD_DIM = 64
ROPE_HALF = HEAD_DIM // 2
CHUNK = 64
A_HEADS = 16
A_KV_HEADS = 4
A_GROUP = A_HEADS // A_KV_HEADS
IDX_HEADS = 8
TOPK_MAX = 256
B_HEADS = 16
ROPE_THETA = 10000.0
EPS = 1e-6
INT_MIN = -(2 ** 31)
MASKED = -1e30
LOG2E = 1.4426950408889634
QK_SCALE = HEAD_DIM ** -0.5 * LOG2E
VMEM_LIMIT_BYTES = 56 * 1024 * 1024
ATTN_BLOCK = 256

F32 = jnp.float32
BF16 = jnp.bfloat16
I32 = jnp.int32


def _params(*sem):
    return pltpu.CompilerParams(dimension_semantics=sem,
                                vmem_limit_bytes=VMEM_LIMIT_BYTES)


def _resident(shape):
    nd = len(shape)
    return pl.BlockSpec(shape, lambda *_: (0,) * nd, pipeline_mode=pl.Buffered(1))


def _rms(x, g):
    return x * lax.rsqrt(jnp.mean(x * x, axis=-1, keepdims=True) + EPS) * g


def _colsum8(x):
    n, w = x.shape
    return x.reshape(n // SUBLANES, SUBLANES, w).sum(axis=0)


def _colmax8(x):
    n, w = x.shape
    return x.reshape(n // SUBLANES, SUBLANES, w).max(axis=0)


V_ROWS = HEAD_DIM + BF16_ROWS


def _store_value_head(vt_ref, sb, head, yt):
    r = head * V_ROWS
    tk = yt.shape[1]
    vt_ref[sb, r:r + HEAD_DIM, :] = yt
    first_row = lax.broadcasted_iota(I32, (BF16_ROWS, tk), 0) == 0
    vt_ref[sb, r + HEAD_DIM:r + V_ROWS, :] = jnp.where(first_row, 1.0, 0.0).astype(BF16)


def _softmax_pv(s_buf, mx_buf, vt, h, m_sc, acc_sc):
    m_old = m_sc[h]
    m_new = jnp.maximum(m_old, jnp.max(mx_buf[h], axis=0, keepdims=True))
    alpha = jnp.exp2(m_old - m_new)
    p = jnp.exp2(s_buf[h] - m_new)
    acc_sc[h] = alpha * acc_sc[h] + jnp.dot(vt, p.astype(BF16), preferred_element_type=F32)
    m_sc[h] = m_new


def _normalised(acc_sc, h):
    acc = acc_sc[h]
    return acc[0:HEAD_DIM, :] / acc[HEAD_DIM:HEAD_DIM + 1, :]


A_Q = A_HEADS * HEAD_DIM
A_KD = A_KV_HEADS * LANES
A_V = A_KV_HEADS * HEAD_DIM
A_VT = A_KV_HEADS * V_ROWS
A_QI = IDX_HEADS * HEAD_DIM
A_OFF_K = A_Q
A_OFF_V = A_OFF_K + A_KD
A_OFF_QI = A_OFF_V + A_V
A_OFF_KI = A_OFF_QI + A_QI
A_OFF_WI = A_OFF_KI + LANES
A_COLS = A_OFF_WI + LANES


def _aproj_kernel(h_ref, g_ref, w_ref, cos_ref, sin_ref,
                  qt_ref, k_ref, vt_ref, qit_ref, ki_ref, wit_ref, *, wi_scale):
    tm = h_ref.shape[0]
    tk = vt_ref.shape[2]
    xn = _rms(h_ref[...], g_ref[...]).astype(BF16)
    cos = cos_ref[...]
    sin = sin_ref[...]
    lane = lax.broadcasted_iota(I32, (tm, LANES), 1)
    first_half = (lane & ROPE_HALF) == 0

    def rope(y):
        ys = jnp.where(first_half, pltpu.roll(y, LANES - ROPE_HALF, 1),
                       pltpu.roll(y, ROPE_HALF, 1))
        return y * cos + ys * sin

    def chunks(c0, n):
        step = 4 * LANES
        for s0 in range(0, n, step):
            sn = min(step, n - s0)
            y = jnp.dot(xn, w_ref[:, c0 + s0:c0 + s0 + sn], preferred_element_type=F32)
            for j in range(sn // LANES):
                yield (s0 // LANES + j), y[:, j * LANES:(j + 1) * LANES]

    for j, y in chunks(0, A_Q):
        qt_ref[j * LANES:(j + 1) * LANES, :] = (rope(y) * QK_SCALE).T.astype(BF16)
    for j, y in chunks(A_OFF_K, A_KD):
        k_ref[:, j * LANES:(j + 1) * LANES] = rope(y).astype(BF16)
    for j, y in chunks(A_OFF_V, A_V):
        for sb in range(tm // tk):
            yt = y[sb * tk:(sb + 1) * tk].T.astype(BF16)
            for e in range(2):
                _store_value_head(vt_ref, sb, 2 * j + e, yt[e * HEAD_DIM:(e + 1) * HEAD_DIM, :])
    for j, y in chunks(A_OFF_QI, A_QI):
        qit_ref[j * LANES:(j + 1) * LANES, :] = rope(y).T.astype(BF16)
    for j, y in chunks(A_OFF_KI, LANES):
        ki_ref[...] = rope(y).astype(BF16)
    for j, y in chunks(A_OFF_WI, LANES):
        wit_ref[...] = (y * wi_scale).T[:IDX_HEADS, :]


def _aproj(h, g, w, cos, sin, seq, tm, tk):
    t, d = h.shape
    npos = seq // tm
    row = lambda n: pl.BlockSpec((tm, n), lambda i: (i, 0))
    col = lambda n: pl.BlockSpec((n, tm), lambda i: (0, i))
    tab = pl.BlockSpec((tm, LANES), lambda i: (i % npos, 0))
    wi_scale = IDX_HEADS ** -0.5 * HEAD_DIM ** -0.5
    return pl.pallas_call(
        functools.partial(_aproj_kernel, wi_scale=wi_scale),
        grid=(t // tm,),
        in_specs=[row(d), _resident((1, d)), _resident((d, A_COLS)), tab, tab],
        out_specs=[col(A_Q), row(A_KD),
                   pl.BlockSpec((tm // tk, A_VT, tk), lambda i: (i, 0, 0)),
                   col(A_QI), row(LANES), col(IDX_HEADS)],
        out_shape=[jax.ShapeDtypeStruct((A_Q, t), BF16),
                   jax.ShapeDtypeStruct((t, A_KD), BF16),
                   jax.ShapeDtypeStruct((t // tk, A_VT, tk), BF16),
                   jax.ShapeDtypeStruct((A_QI, t), BF16),
                   jax.ShapeDtypeStruct((t, LANES), BF16),
                   jax.ShapeDtypeStruct((IDX_HEADS, t), F32)],
        compiler_params=_params("parallel"),
        name="a_proj",
    )(h, g, w, cos, sin)


KEY_BITS = 32
BIT_TRANSPOSE_STEPS = (16, 8, 4, 2, 1)
BIT_TRANSPOSE_MASKS = (0x0000FFFF, 0x00FF00FF, 0x0F0F0F0F, 0x33333333, 0x55555555)


def _dsa_kernel(qt_ref, qit_ref, wit_ref, ki_ref, k_ref, vt_ref, o_ref,
                key_sc, planes_sc, qiw_sc, qw_sc, sa_sc, sb_sc, mxa_sc, mxb_sc, m_sc, acc_sc,
                *, tq, n_sel, idx_bits):
    tk = tq
    jj = pl.program_id(1)
    kiota = lax.broadcasted_iota(I32, (tk, tq), 0)
    zeros_half = jnp.zeros((HEAD_DIM, tq), BF16)

    for sub in range(2):
        cols = slice(sub * tq, (sub + 1) * tq)
        for h in range(IDX_HEADS):
            qiw_sc[sub * IDX_HEADS + h, 0:HEAD_DIM, :] = qit_ref[h * HEAD_DIM:(h + 1) * HEAD_DIM, cols]
            qiw_sc[sub * IDX_HEADS + h, HEAD_DIM:LANES, :] = zeros_half
        for h in range(A_HEADS):
            qw_sc[sub * A_HEADS + h, 0:HEAD_DIM, :] = qt_ref[h * HEAD_DIM:(h + 1) * HEAD_DIM, cols]
            qw_sc[sub * A_HEADS + h, HEAD_DIM:LANES, :] = zeros_half

    def select(sub):
        j = 2 * jj + sub
        nkb = j + 1
        keys = key_sc.at[sub]
        qpos = j * tq + lax.broadcasted_iota(I32, (1, tq), 1)
        limit = (qpos // CHUNK + 1) * CHUNK

        def head_logits(c):
            kc = ki_ref[pl.ds(pl.multiple_of(c * tk, tk), tk), :]
            return [jnp.dot(kc, qiw_sc[sub * IDX_HEADS + h], preferred_element_type=F32)
                    for h in range(IDX_HEADS)]

        def store_keys(c, logits):
            sc = jnp.zeros((tk, tq), F32)
            for h in range(IDX_HEADS):
                sc = sc + jnp.maximum(logits[h], 0.0) * wit_ref[h:h + 1, sub * tq:(sub + 1) * tq]
            sc = jnp.where(sc == 0.0, 0.0, sc)
            bits = pltpu.bitcast(sc, I32)
            key = bits ^ ((bits >> 31) & 0x7FFFFFFF)
            keys[c] = jnp.where(c * tk + kiota < limit, key, INT_MIN)

        def bit_slice(c):
            x = keys[c] ^ INT_MIN
            tiles = [x[v * SUBLANES:(v + 1) * SUBLANES, :] for v in range(KEY_BITS)]
            for jbit, mask in zip(BIT_TRANSPOSE_STEPS, BIT_TRANSPOSE_MASKS):
                for v in range(KEY_BITS):
                    if v & jbit == 0:
                        lo, hi = tiles[v], tiles[v + jbit]
                        t = (lax.shift_right_logical(lo, jbit) ^ hi) & mask
                        tiles[v] = lo ^ jnp.left_shift(t, jbit)
                        tiles[v + jbit] = hi ^ t
            for b in range(KEY_BITS):
                planes_sc[b, c] = tiles[b]

        store_keys(0, head_logits(0))

        def score_body(c, carry):
            logits = head_logits(c)
            bit_slice(c - 1)
            store_keys(c, logits)
            return carry

        lax.fori_loop(1, nkb, score_body, 0)
        bit_slice(nkb - 1)

        def clear_body(c, carry):
            planes_sc[:, c] = jnp.zeros((KEY_BITS, SUBLANES, tq), I32)
            return carry

        lax.fori_loop(nkb, planes_sc.shape[1], clear_body, 0)

        @pl.when(nkb % 2 == 1)
        def _():
            keys[nkb] = jnp.full((tk, tq), INT_MIN, I32)

        def count(pred):
            def body(cc, acc):
                for u in range(2):
                    c = 2 * cc + u
                    acc = acc + _colsum8(jnp.where(pred(c, keys[c]), 1.0, 0.0))
                return acc
            acc = lax.fori_loop(0, (nkb + 1) // 2, body, jnp.zeros((SUBLANES, tq), F32))
            return jnp.sum(acc, axis=0, keepdims=True)

        nwords = planes_sc.shape[1] * SUBLANES
        word_block = lax.broadcasted_iota(I32, (nwords, 1), 0) // SUBLANES
        cand = jnp.broadcast_to(jnp.where(word_block < nkb, -1, 0), (nwords, tq))

        def ones_count(words):
            cnt = lax.population_count(words).reshape(nwords // SUBLANES, SUBLANES, tq).sum(axis=0)
            return jnp.sum(cnt.astype(F32), axis=0, keepdims=True)

        thr_u = jnp.zeros((1, tq), I32)
        above = jnp.zeros((1, tq), F32)
        for b in range(KEY_BITS - 1, -1, -1):
            plane = planes_sc[b].reshape(nwords, tq)
            ones = cand & plane
            cnt1 = ones_count(ones)
            take = above + cnt1 >= n_sel
            thr_u = jnp.where(take, thr_u | (INT_MIN if b == KEY_BITS - 1 else 1 << b), thr_u)
            above = jnp.where(take, above, above + cnt1)
            cand = jnp.where(take, ones, cand ^ ones)
        cnt_thr = jnp.where(thr_u == 0, 0.0, above + ones_count(cand))
        thr = jnp.maximum(thr_u ^ INT_MIN, INT_MIN + 1)

        @pl.when(jnp.max(cnt_thr) > n_sel)
        def _():
            need = n_sel - count(lambda c, key: key > thr)
            pos = jnp.zeros((1, tq), I32)
            for b in range(idx_bits - 1, -1, -1):
                cand = pos | (1 << b)
                cnt = count(lambda c, key: jnp.where(key == thr, c * tk + kiota, 2 ** 30) < cand)
                pos = jnp.where(cnt < need, cand, pos)

            def demote(c, carry):
                key = keys[c]
                tie_pos = jnp.where(key == thr, c * tk + kiota, -1)
                keys[c] = jnp.where(tie_pos > pos, INT_MIN, key)
                return carry

            lax.fori_loop(0, nkb, demote, 0)

        return thr

    thrs = [select(0), select(1)]

    m_sc[...] = jnp.full(m_sc.shape, MASKED, F32)
    acc_sc[...] = jnp.zeros(acc_sc.shape, F32)
    buf_a = (sa_sc, mxa_sc)
    buf_b = (sb_sc, mxb_sc)

    def select_bias(c, sub):
        return jnp.where(key_sc[sub, c] >= thrs[sub], 0.0, MASKED)

    def scores(c, sub, h, buf, bias):
        n = h // A_GROUP
        r0 = pl.multiple_of(c * tk, tk)
        s = jnp.dot(k_ref[pl.ds(r0, tk), n * LANES:(n + 1) * LANES], qw_sc[sub * A_HEADS + h],
                    preferred_element_type=F32) + bias
        buf[0][sub * A_HEADS + h] = s
        buf[1][sub * A_HEADS + h] = _colmax8(s)

    def update(c, sub, h, buf):
        n = h // A_GROUP
        _softmax_pv(buf[0], buf[1], vt_ref[c, n * V_ROWS:(n + 1) * V_ROWS, :],
                    sub * A_HEADS + h, m_sc, acc_sc)

    def stage(c, cur, nxt, score_subs, update_subs=(0, 1)):
        biases = {sub: select_bias(c + 1, sub) for sub in score_subs}
        for h in range(A_HEADS):
            for sub in score_subs:
                scores(c + 1, sub, h, nxt, biases[sub])
            for sub in update_subs:
                update(c, sub, h, cur)

    for sub in range(2):
        bias0 = select_bias(0, sub)
        for h in range(A_HEADS):
            scores(0, sub, h, buf_a, bias0)

    def attn_body(i, carry):
        stage(2 * i, buf_a, buf_b, (0, 1))
        stage(2 * i + 1, buf_b, buf_a, (0, 1))
        return carry

    lax.fori_loop(0, jj, attn_body, 0)
    last = 2 * jj
    stage(last, buf_a, buf_b, (1,))
    stage(last + 1, buf_b, None, (), update_subs=(1,))

    for sub in range(2):
        for i in range(A_HEADS // 2):
            pair = [_normalised(acc_sc, sub * A_HEADS + h) for h in (2 * i, 2 * i + 1)]
            o_ref[sub * tq:(sub + 1) * tq, i * LANES:(i + 1) * LANES] = (
                jnp.concatenate(pair, axis=0).T.astype(o_ref.dtype))


def _dsa(qt, qit, wit, ki, k, vt, b, s, tq):
    t = b * s
    nq = s // tq
    assert nq % 2 == 0, "dsa_attention walks query blocks in pairs"
    n_sel = min(TOPK_MAX, s // 4)
    idx_bits = max(1, (s - 1).bit_length())
    assert tq == KEY_BITS * SUBLANES, "the bit-sliced search transposes 32 sublane tiles per key block"
    qcol = lambda n: pl.BlockSpec((n, 2 * tq), lambda bi, j: (0, bi * (nq // 2) + j))
    full = lambda n: pl.BlockSpec((s, n), lambda bi, j: (bi, 0))
    return pl.pallas_call(
        functools.partial(_dsa_kernel, tq=tq, n_sel=float(n_sel), idx_bits=idx_bits),
        grid=(b, nq // 2),
        in_specs=[qcol(A_Q), qcol(A_QI), qcol(IDX_HEADS), full(LANES), full(A_KD),
                  pl.BlockSpec((nq, A_VT, tq), lambda bi, j: (bi, 0, 0))],
        out_specs=pl.BlockSpec((2 * tq, A_Q), lambda bi, j: (bi * (nq // 2) + j, 0)),
        out_shape=jax.ShapeDtypeStruct((t, A_Q), BF16),
        scratch_shapes=[
            pltpu.VMEM((2, nq, tq, tq), I32),
            pltpu.VMEM((KEY_BITS, nq, SUBLANES, tq), I32),
            pltpu.VMEM((2 * IDX_HEADS, LANES, tq), BF16),
            pltpu.VMEM((2 * A_HEADS, LANES, tq), BF16),
            pltpu.VMEM((2 * A_HEADS, tq, tq), F32),
            pltpu.VMEM((2 * A_HEADS, tq, tq), F32),
            pltpu.VMEM((2 * A_HEADS, SUBLANES, tq), F32),
            pltpu.VMEM((2 * A_HEADS, SUBLANES, tq), F32),
            pltpu.VMEM((2 * A_HEADS, 1, tq), F32),
            pltpu.VMEM((2 * A_HEADS, V_ROWS, tq), F32),
        ],
        compiler_params=_params("parallel", "arbitrary"),
        name="dsa_attention",
    )(qt, qit, wit, ki, k, vt)


B_QK = B_HEADS * HEAD_DIM
B_KAUG = B_HEADS * LANES
B_VT = B_HEADS * V_ROWS
BIAS_PIECES = 3


def _split3(x):
    hi = x.astype(BF16)
    r1 = x - hi.astype(F32)
    mid = r1.astype(BF16)
    low = (r1 - mid.astype(F32)).astype(BF16)
    return hi, mid, low


def _kvproj_kernel(h_ref, g_ref, wk_ref, wv_ref, wf_ref, bf_ref, place_ref, ones_ref,
                   qplace_ref, qones_ref, kaug_ref, vt_ref, qrows_ref, carry_sc):
    tm = h_ref.shape[0]
    tk = vt_ref.shape[2]

    @pl.when(pl.program_id(1) == 0)
    def _():
        carry_sc[...] = jnp.zeros(carry_sc.shape, F32)

    xn = _rms(h_ref[...], g_ref[...]).astype(BF16)
    z = jnp.dot(xn, wf_ref[...], preferred_element_type=F32) + bf_ref[...]
    logf = jnp.minimum(z, 0.0) - jnp.log1p(jnp.exp(-jnp.abs(z)))
    tri = (lax.broadcasted_iota(I32, (tm, tm), 0) >=
           lax.broadcasted_iota(I32, (tm, tm), 1)).astype(BF16)
    cs = sum(jnp.dot(tri, piece, preferred_element_type=F32) for piece in _split3(logf))
    cum = cs + carry_sc[...]
    carry_sc[...] = cum[tm - 1:tm, :]
    hi, mid, low = _split3(cum * LOG2E)
    pieces = jnp.concatenate([hi, mid, low], axis=1)
    qrows = lax.dot_general(qplace_ref[...], pieces, (((1,), (1,)), ((), ())),
                            preferred_element_type=F32) + qones_ref[...]
    qrows_ref[...] = qrows.astype(BF16)

    step = 4 * LANES
    for c0 in range(0, B_KAUG, step):
        y = (jnp.dot(xn, wk_ref[:, c0:c0 + step], preferred_element_type=F32) +
             jnp.dot(pieces, place_ref[:, c0:c0 + step], preferred_element_type=F32) +
             ones_ref[:, c0:c0 + step])
        kaug_ref[:, c0:c0 + step] = y.astype(BF16)
    for c0 in range(0, B_QK, step):
        y = jnp.dot(xn, wv_ref[:, c0:c0 + step], preferred_element_type=F32)
        for jj in range(step // LANES):
            head = (c0 + jj * LANES) // HEAD_DIM
            for sb in range(tm // tk):
                yt = y[sb * tk:(sb + 1) * tk, jj * LANES:(jj + 1) * LANES].T.astype(BF16)
                for e in range(2):
                    _store_value_head(vt_ref, sb, head + e, yt[e * HEAD_DIM:(e + 1) * HEAD_DIM, :])


def _kvproj(h3, g, wk, wv, wf, bf, place, ones, qplace, qones, tm, tk):
    b, s, d = h3.shape
    ns = s // tm
    blk = lambda n: pl.BlockSpec((None, tm, n), lambda bi, i: (bi, i, 0))
    return pl.pallas_call(
        _kvproj_kernel,
        grid=(b, ns),
        in_specs=[blk(d), _resident((1, d)), _resident(wk.shape), _resident(wv.shape),
                  _resident(wf.shape), _resident((1, LANES)), _resident(place.shape),
                  _resident(ones.shape), _resident(qplace.shape), _resident(qones.shape)],
        out_specs=[blk(B_KAUG),
                   pl.BlockSpec((tm // tk, B_VT, tk), lambda bi, i: (bi * ns + i, 0, 0)),
                   pl.BlockSpec((B_HEADS * BF16_ROWS, tm), lambda bi, i: (0, bi * ns + i))],
        out_shape=[jax.ShapeDtypeStruct((b, s, B_KAUG), BF16),
                   jax.ShapeDtypeStruct((b * s // tk, B_VT, tk), BF16),
                   jax.ShapeDtypeStruct((B_HEADS * BF16_ROWS, b * s), BF16)],
        scratch_shapes=[pltpu.VMEM((1, LANES), F32)],
        compiler_params=_params("parallel", "arbitrary"),
        name="kv_proj_cumsum",
    )(h3, g, wk, wv, wf, bf, place, ones, qplace, qones)


def _qproj_kernel(h_ref, g_ref, w_ref, rows_ref, qt_ref):
    tm = h_ref.shape[0]
    xn = _rms(h_ref[...], g_ref[...]).astype(BF16)
    pad = jnp.zeros((LANES - HEAD_DIM - BF16_ROWS, tm), BF16)
    step = 4 * LANES
    for c0 in range(0, B_QK, step):
        y = jnp.dot(xn, w_ref[:, c0:c0 + step], preferred_element_type=F32) * QK_SCALE
        for jj in range(step // LANES):
            yt = y[:, jj * LANES:(jj + 1) * LANES].T.astype(BF16)
            for e in range(2):
                h = (c0 + jj * LANES) // HEAD_DIM + e
                r = h * LANES
                qt_ref[r:r + HEAD_DIM, :] = yt[e * HEAD_DIM:(e + 1) * HEAD_DIM, :]
                qt_ref[r + HEAD_DIM:r + HEAD_DIM + BF16_ROWS, :] = (
                    rows_ref[h * BF16_ROWS:(h + 1) * BF16_ROWS, :])
                qt_ref[r + HEAD_DIM + BF16_ROWS:r + LANES, :] = pad


def _qproj(h, g, w, rows, tm):
    t, d = h.shape
    return pl.pallas_call(
        _qproj_kernel,
        grid=(t // tm,),
        in_specs=[pl.BlockSpec((tm, d), lambda i: (i, 0)), _resident((1, d)), _resident(w.shape),
                  pl.BlockSpec((B_HEADS * BF16_ROWS, tm), lambda i: (0, i))],
        out_specs=pl.BlockSpec((B_KAUG, tm), lambda i: (0, i)),
        out_shape=jax.ShapeDtypeStruct((B_KAUG, t), BF16),
        compiler_params=_params("parallel"),
        name="q_proj",
    )(h, g, w, rows)


FOX_HEADS_PER_STEP = 8


def _fox_kernel(qt_ref, k_ref, vt_ref, o_ref, sa_sc, sb_sc, mxa_sc, mxb_sc, m_sc, acc_sc, *, tq):
    tk = tq
    hps = FOX_HEADS_PER_STEP
    jj = pl.program_id(2)
    m_sc[...] = jnp.full(m_sc.shape, MASKED, F32)
    acc_sc[...] = jnp.zeros(acc_sc.shape, F32)
    kiota = lax.broadcasted_iota(I32, (tk, tq), 0)
    qiota = lax.broadcasted_iota(I32, (1, tq), 1)
    buf_a = (sa_sc, mxa_sc)
    buf_b = (sb_sc, mxb_sc)

    def scores(c, sub, h, buf, qlimit):
        r0 = pl.multiple_of(c * tk, tk)
        s = jnp.dot(k_ref[pl.ds(r0, tk), h * LANES:(h + 1) * LANES],
                    qt_ref[h * LANES:(h + 1) * LANES, sub * tq:(sub + 1) * tq],
                    preferred_element_type=F32)
        if qlimit is not None:
            s = jnp.where(kiota <= qlimit, s, MASKED)
        buf[0][sub * hps + h] = s
        buf[1][sub * hps + h] = _colmax8(s)

    def update(c, sub, h, buf):
        _softmax_pv(buf[0], buf[1], vt_ref[c, h * V_ROWS:(h + 1) * V_ROWS, :],
                    sub * hps + h, m_sc, acc_sc)

    def stage(c, cur, nxt, limits, update_subs=(0, 1)):
        for h in range(hps):
            for sub, qlimit in limits:
                scores(c + 1, sub, h, nxt, qlimit)
            for sub in update_subs:
                update(c, sub, h, cur)

    first_limit = qiota + jnp.where(jj == 0, 0, tk)
    for h in range(hps):
        scores(0, 0, h, buf_a, first_limit)
        scores(0, 1, h, buf_a, None)

    def main_body(i, carry):
        c = 2 * i
        stage(c, buf_a, buf_b, ((0, None), (1, None)))
        limit0 = qiota + jnp.where(i == jj - 1, 0, tk)
        stage(c + 1, buf_b, buf_a, ((0, limit0), (1, None)))
        return carry

    lax.fori_loop(0, jj, main_body, 0)
    last = 2 * jj
    stage(last, buf_a, buf_b, ((1, qiota),))
    stage(last + 1, buf_b, None, (), update_subs=(1,))
    for sub in range(2):
        outs = [_normalised(acc_sc, sub * hps + h) for h in range(hps)]
        o_ref[sub * tq:(sub + 1) * tq, :] = jnp.concatenate(outs, axis=0).T.astype(o_ref.dtype)


def _fox(qt, kaug, vt, b, s, tq):
    t = b * s
    nq = s // tq
    assert nq % 2 == 0, "fox_attention walks query blocks in pairs"
    hps = FOX_HEADS_PER_STEP
    nstate = 2 * hps
    return pl.pallas_call(
        functools.partial(_fox_kernel, tq=tq),
        grid=(b, B_HEADS // hps, nq // 2),
        in_specs=[
            pl.BlockSpec((hps * LANES, 2 * tq), lambda bi, p, j: (p, bi * (nq // 2) + j)),
            pl.BlockSpec((None, s, hps * LANES), lambda bi, p, j: (bi, 0, p)),
            pl.BlockSpec((nq, hps * V_ROWS, tq), lambda bi, p, j: (bi, p, 0)),
        ],
        out_specs=pl.BlockSpec((2 * tq, hps * HEAD_DIM), lambda bi, p, j: (bi * (nq // 2) + j, p)),
        out_shape=jax.ShapeDtypeStruct((t, B_QK), BF16),
        scratch_shapes=[
            pltpu.VMEM((nstate, tq, tq), F32),
            pltpu.VMEM((nstate, tq, tq), F32),
            pltpu.VMEM((nstate, SUBLANES, tq), F32),
            pltpu.VMEM((nstate, SUBLANES, tq), F32),
            pltpu.VMEM((nstate, 1, tq), F32),
            pltpu.VMEM((nstate, V_ROWS, tq), F32),
        ],
        compiler_params=_params("parallel", "parallel", "arbitrary"),
        name="fox_attention",
    )(qt, kaug, vt)


def _mlp_kernel(h_ref, o_ref, wo_ref, g_ref, wup_ref, wdn_ref, gf_ref, out_ref, *, tf, final):
    h1 = h_ref[...] + jnp.dot(o_ref[...], wo_ref[...], preferred_element_type=F32)
    xn = _rms(h1, g_ref[...]).astype(BF16)
    acc = h1
    for c0 in range(0, wup_ref.shape[1], tf):
        u = jnp.dot(xn, wup_ref[:, c0:c0 + tf], preferred_element_type=F32)
        a = jnp.square(jnp.maximum(u, 0.0)).astype(BF16)
        acc = acc + jnp.dot(a, wdn_ref[c0:c0 + tf, :], preferred_element_type=F32)
    if final:
        acc = _rms(acc, gf_ref[...])
    out_ref[...] = acc


def _mlp(h, o, wo, g, wup, wdn, gf, final, tm, tf):
    t, d = h.shape
    row = pl.BlockSpec((tm, d), lambda i: (i, 0))
    return pl.pallas_call(
        functools.partial(_mlp_kernel, tf=tf, final=final),
        grid=(t // tm,),
        in_specs=[row, row, _resident(wo.shape), _resident((1, d)), _resident(wup.shape),
                  _resident(wdn.shape), _resident((1, d))],
        out_specs=row,
        out_shape=jax.ShapeDtypeStruct((t, d), F32),
        compiler_params=_params("parallel"),
        name="oproj_mlp",
    )(h, o, wo, g, wup, wdn, gf)


def _dup_heads(w, heads):
    d = w.shape[0]
    w = w.reshape(d, heads, HEAD_DIM)
    return jnp.concatenate([w, w], axis=-1).reshape(d, heads * LANES)


def _prep_w_in_a(w):
    kv = A_KV_HEADS * HEAD_DIM
    o_k, o_v, o_qi = A_Q, A_Q + kv, A_Q + 2 * kv
    o_ki, o_wi = o_qi + A_QI, o_qi + A_QI + HEAD_DIM
    wi = w[:, o_wi:o_wi + IDX_HEADS]
    return jnp.concatenate([
        w[:, :A_Q],
        _dup_heads(w[:, o_k:o_v], A_KV_HEADS),
        w[:, o_v:o_qi],
        w[:, o_qi:o_ki],
        _dup_heads(w[:, o_ki:o_wi], 1),
        jnp.pad(wi, ((0, 0), (0, LANES - IDX_HEADS))),
    ], axis=1).astype(BF16)


def _rope_tables(seq):
    inv = 1.0 / (ROPE_THETA ** (jnp.arange(0, HEAD_DIM, 2, dtype=F32) / HEAD_DIM))
    ang = jnp.arange(seq, dtype=F32)[:, None] * inv[None, :]
    c, s = jnp.cos(ang), jnp.sin(ang)
    return jnp.concatenate([c, c, c, c], axis=1), jnp.concatenate([-s, s, -s, s], axis=1)


def _bias_placement():
    place = np.zeros((BIAS_PIECES * LANES, B_KAUG), np.float32)
    ones = np.zeros((1, B_KAUG), np.float32)
    for h in range(B_HEADS):
        for p in range(BIAS_PIECES):
            place[p * LANES + h, h * LANES + HEAD_DIM + p] = -1.0
            ones[0, h * LANES + HEAD_DIM + BIAS_PIECES + p] = 1.0
    return jnp.asarray(place, BF16), jnp.asarray(ones, F32)


def _query_bias_placement(tm):
    qplace = np.zeros((B_HEADS * BF16_ROWS, BIAS_PIECES * LANES), np.float32)
    qones = np.zeros((B_HEADS * BF16_ROWS, tm), np.float32)
    for h in range(B_HEADS):
        for p in range(BIAS_PIECES):
            qplace[h * BF16_ROWS + BIAS_PIECES + p, p * LANES + h] = 1.0
            qones[h * BF16_ROWS + p, :] = 1.0
    return jnp.asarray(qplace, BF16), jnp.asarray(qones, F32)


def kernel(x, g_attn_a, w_in_a, w_o_a, g_kv, w_kv_b, b_f, g_attn_b, w_q_b, w_o_b,
           g_mlp, w_up, w_down, g_final):
    b, s, d = x.shape
    t = b * s
    depth = g_mlp.shape[0]
    n_a = g_attn_a.shape[0]
    tm = min(512, s)
    tq = min(ATTN_BLOCK, s)
    tf = 512

    cos, sin = _rope_tables(s)
    gf = g_final.reshape(1, d)
    h = x.reshape(t, d)
    kaug = vt_b = bias_rows = None

    for layer in range(depth):
        if layer < n_a:
            qt, k, vt, qit, ki, wit = _aproj(h, g_attn_a[layer].reshape(1, d),
                                             _prep_w_in_a(w_in_a[layer]), cos, sin, s, tm, tq)
            o = _dsa(qt, qit, wit, ki, k, vt, b, s, tq)
            w_o = w_o_a[layer]
        else:
            jb = layer - n_a
            if jb == 0:
                wk = jnp.pad(w_kv_b[:, :B_QK].reshape(d, B_HEADS, HEAD_DIM),
                             ((0, 0), (0, 0), (0, LANES - HEAD_DIM))).reshape(d, B_KAUG)
                wf = jnp.pad(w_kv_b[:, 2 * B_QK:], ((0, 0), (0, LANES - B_HEADS)))
                bias = jnp.pad(b_f, (0, LANES - B_HEADS)).reshape(1, LANES)
                place, ones = _bias_placement()
                qplace, qones = _query_bias_placement(tm)
                kaug, vt_b, bias_rows = _kvproj(
                    h.reshape(b, s, d), g_kv.reshape(1, d), wk.astype(BF16),
                    w_kv_b[:, B_QK:2 * B_QK].astype(BF16), wf.astype(BF16), bias,
                    place, ones, qplace, qones, tm, tq)
            qt = _qproj(h, g_attn_b[jb].reshape(1, d), w_q_b[jb].astype(BF16), bias_rows, tm)
            o = _fox(qt, kaug, vt_b, b, s, tq)
            w_o = w_o_b[jb]
        h = _mlp(h, o, w_o.astype(BF16), g_mlp[layer].reshape(1, d),
                 w_up[layer].astype(BF16), w_down[layer].astype(BF16), gf,
                 layer == depth - 1, tm, tf)
    return h.reshape(b, s, d)
```

```python
import functools

import numpy as np
import jax
import jax.numpy as jnp
from jax import lax
from jax.experimental import pallas as pl
from jax.experimental.pallas import tpu as pltpu

LANES = 128
SUBLANES = 8
BF16_ROWS = 16
HEAD_DIM = 64
ROPE_HALF = HEAD_DIM // 2
CHUNK = 64
A_HEADS = 16
A_KV_HEADS = 4
A_GROUP = A_HEADS // A_KV_HEADS
IDX_HEADS = 8
TOPK_MAX = 256
B_HEADS = 16
ROPE_THETA = 10000.0
EPS = 1e-6
INT_MIN = -(2 ** 31)
MASKED = -1e30
LOG2E = 1.4426950408889634
QK_SCALE = HEAD_DIM ** -0.5 * LOG2E
VMEM_LIMIT_BYTES = 56 * 1024 * 1024
ATTN_BLOCK = 256

F32 = jnp.float32
BF16 = jnp.bfloat16
I32 = jnp.int32


def _params(*sem):
    return pltpu.CompilerParams(dimension_semantics=sem,
                                vmem_limit_bytes=VMEM_LIMIT_BYTES)


def _resident(shape):
    nd = len(shape)
    return pl.BlockSpec(shape, lambda *_: (0,) * nd, pipeline_mode=pl.Buffered(1))


def _rms(x, g):
    return x * lax.rsqrt(jnp.mean(x * x, axis=-1, keepdims=True) + EPS) * g


def _colsum8(x):
    n, w = x.shape
    return x.reshape(n // SUBLANES, SUBLANES, w).sum(axis=0)


def _colmax8(x):
    n, w = x.shape
    return x.reshape(n // SUBLANES, SUBLANES, w).max(axis=0)


V_ROWS = HEAD_DIM + BF16_ROWS


def _store_value_head(vt_ref, sb, head, yt):
    r = head * V_ROWS
    tk = yt.shape[1]
    vt_ref[sb, r:r + HEAD_DIM, :] = yt
    first_row = lax.broadcasted_iota(I32, (BF16_ROWS, tk), 0) == 0
    vt_ref[sb, r + HEAD_DIM:r + V_ROWS, :] = jnp.where(first_row, 1.0, 0.0).astype(BF16)


def _softmax_pv(s_buf, mx_buf, vt, h, m_sc, acc_sc):
    m_old = m_sc[h]
    m_new = jnp.maximum(m_old, jnp.max(mx_buf[h], axis=0, keepdims=True))
    alpha = jnp.exp2(m_old - m_new)
    p = jnp.exp2(s_buf[h] - m_new)
    acc_sc[h] = alpha * acc_sc[h] + jnp.dot(vt, p.astype(BF16), preferred_element_type=F32)
    m_sc[h] = m_new


def _normalised(acc_sc, h):
    acc = acc_sc[h]
    return acc[0:HEAD_DIM, :] / acc[HEAD_DIM:HEAD_DIM + 1, :]


A_Q = A_HEADS * HEAD_DIM
A_KD = A_KV_HEADS * LANES
A_V = A_KV_HEADS * HEAD_DIM
A_VT = A_KV_HEADS * V_ROWS
A_QI = IDX_HEADS * HEAD_DIM
A_OFF_K = A_Q
A_OFF_V = A_OFF_K + A_KD
A_OFF_QI = A_OFF_V + A_V
A_OFF_KI = A_OFF_QI + A_QI
A_OFF_WI = A_OFF_KI + LANES
A_COLS = A_OFF_WI + LANES


def _aproj_kernel(h_ref, g_ref, w_ref, cos_ref, sin_ref,
                  qt_ref, k_ref, vt_ref, qit_ref, ki_ref, wit_ref, *, wi_scale):
    tm = h_ref.shape[0]
    tk = vt_ref.shape[2]
    xn = _rms(h_ref[...], g_ref[...]).astype(BF16)
    cos = cos_ref[...]
    sin = sin_ref[...]
    lane = lax.broadcasted_iota(I32, (tm, LANES), 1)
    first_half = (lane & ROPE_HALF) == 0

    def rope(y):
        ys = jnp.where(first_half, pltpu.roll(y, LANES - ROPE_HALF, 1),
                       pltpu.roll(y, ROPE_HALF, 1))
        return y * cos + ys * sin

    def chunks(c0, n):
        step = 4 * LANES
        for s0 in range(0, n, step):
            sn = min(step, n - s0)
            y = jnp.dot(xn, w_ref[:, c0 + s0:c0 + s0 + sn], preferred_element_type=F32)
            for j in range(sn // LANES):
                yield (s0 // LANES + j), y[:, j * LANES:(j + 1) * LANES]

    for j, y in chunks(0, A_Q):
        qt_ref[j * LANES:(j + 1) * LANES, :] = (rope(y) * QK_SCALE).T.astype(BF16)
    for j, y in chunks(A_OFF_K, A_KD):
        k_ref[:, j * LANES:(j + 1) * LANES] = rope(y).astype(BF16)
    for j, y in chunks(A_OFF_V, A_V):
        for sb in range(tm // tk):
            yt = y[sb * tk:(sb + 1) * tk].T.astype(BF16)
            for e in range(2):
                _store_value_head(vt_ref, sb, 2 * j + e, yt[e * HEAD_DIM:(e + 1) * HEAD_DIM, :])
    for j, y in chunks(A_OFF_QI, A_QI):
        qit_ref[j * LANES:(j + 1) * LANES, :] = rope(y).T.astype(BF16)
    for j, y in chunks(A_OFF_KI, LANES):
        ki_ref[...] = rope(y).astype(BF16)
    for j, y in chunks(A_OFF_WI, LANES):
        wit_ref[...] = (y * wi_scale).T[:IDX_HEADS, :]


def _aproj(h, g, w, cos, sin, seq, tm, tk):
    t, d = h.shape
    npos = seq // tm
    row = lambda n: pl.BlockSpec((tm, n), lambda i: (i, 0))
    col = lambda n: pl.BlockSpec((n, tm), lambda i: (0, i))
    tab = pl.BlockSpec((tm, LANES), lambda i: (i % npos, 0))
    wi_scale = IDX_HEADS ** -0.5 * HEAD_DIM ** -0.5
    return pl.pallas_call(
        functools.partial(_aproj_kernel, wi_scale=wi_scale),
        grid=(t // tm,),
        in_specs=[row(d), _resident((1, d)), _resident((d, A_COLS)), tab, tab],
        out_specs=[col(A_Q), row(A_KD),
                   pl.BlockSpec((tm // tk, A_VT, tk), lambda i: (i, 0, 0)),
                   col(A_QI), row(LANES), col(IDX_HEADS)],
        out_shape=[jax.ShapeDtypeStruct((A_Q, t), BF16),
                   jax.ShapeDtypeStruct((t, A_KD), BF16),
                   jax.ShapeDtypeStruct((t // tk, A_VT, tk), BF16),
                   jax.ShapeDtypeStruct((A_QI, t), BF16),
                   jax.ShapeDtypeStruct((t, LANES), BF16),
                   jax.ShapeDtypeStruct((IDX_HEADS, t), F32)],
        compiler_params=_params("parallel"),
        name="a_proj",
    )(h, g, w, cos, sin)


KEY_BITS = 32
BIT_TRANSPOSE_STEPS = (16, 8, 4, 2, 1)
BIT_TRANSPOSE_MASKS = (0x0000FFFF, 0x00FF00FF, 0x0F0F0F0F, 0x33333333, 0x55555555)


def _dsa_kernel(qt_ref, qit_ref, wit_ref, ki_ref, k_ref, vt_ref, o_ref,
                key_sc, planes_sc, qiw_sc, qw_sc, sa_sc, sb_sc, mxa_sc, mxb_sc, m_sc, acc_sc,
                *, tq, n_sel, idx_bits):
    tk = tq
    jj = pl.program_id(1)
    kiota = lax.broadcasted_iota(I32, (tk, tq), 0)
    zeros_half = jnp.zeros((HEAD_DIM, tq), BF16)

    for sub in range(2):
        cols = slice(sub * tq, (sub + 1) * tq)
        for h in range(IDX_HEADS):
            qiw_sc[sub * IDX_HEADS + h, 0:HEAD_DIM, :] = qit_ref[h * HEAD_DIM:(h + 1) * HEAD_DIM, cols]
            qiw_sc[sub * IDX_HEADS + h, HEAD_DIM:LANES, :] = zeros_half
        for h in range(A_HEADS):
            qw_sc[sub * A_HEADS + h, 0:HEAD_DIM, :] = qt_ref[h * HEAD_DIM:(h + 1) * HEAD_DIM, cols]
            qw_sc[sub * A_HEADS + h, HEAD_DIM:LANES, :] = zeros_half

    def select(sub):
        j = 2 * jj + sub
        nkb = j + 1
        keys = key_sc.at[sub]
        qpos = j * tq + lax.broadcasted_iota(I32, (1, tq), 1)
        limit = (qpos // CHUNK + 1) * CHUNK

        def head_logits(c):
            kc = ki_ref[pl.ds(pl.multiple_of(c * tk, tk), tk), :]
            return [jnp.dot(kc, qiw_sc[sub * IDX_HEADS + h], preferred_element_type=F32)
                    for h in range(IDX_HEADS)]

        def store_keys(c, logits):
            sc = jnp.zeros((tk, tq), F32)
            for h in range(IDX_HEADS):
                sc = sc + jnp.maximum(logits[h], 0.0) * wit_ref[h:h + 1, sub * tq:(sub + 1) * tq]
            sc = jnp.where(sc == 0.0, 0.0, sc)
            bits = pltpu.bitcast(sc, I32)
            key = bits ^ ((bits >> 31) & 0x7FFFFFFF)
            keys[c] = jnp.where(c * tk + kiota < limit, key, INT_MIN)

        def bit_slice(c):
            x = keys[c] ^ INT_MIN
            tiles = [x[v * SUBLANES:(v + 1) * SUBLANES, :] for v in range(KEY_BITS)]
            for jbit, mask in zip(BIT_TRANSPOSE_STEPS, BIT_TRANSPOSE_MASKS):
                for v in range(KEY_BITS):
                    if v & jbit == 0:
                        lo, hi = tiles[v], tiles[v + jbit]
                        t = (lax.shift_right_logical(lo, jbit) ^ hi) & mask
                        tiles[v] = lo ^ jnp.left_shift(t, jbit)
                        tiles[v + jbit] = hi ^ t
            for b in range(KEY_BITS):
                planes_sc[b, c] = tiles[b]

        store_keys(0, head_logits(0))

        def score_body(c, carry):
            logits = head_logits(c)
            bit_slice(c - 1)
            store_keys(c, logits)
            return carry

        lax.fori_loop(1, nkb, score_body, 0)
        bit_slice(nkb - 1)

        def clear_body(c, carry):
            planes_sc[:, c] = jnp.zeros((KEY_BITS, SUBLANES, tq), I32)
            return carry

        lax.fori_loop(nkb, planes_sc.shape[1], clear_body, 0)

        @pl.when(nkb % 2 == 1)
        def _():
            keys[nkb] = jnp.full((tk, tq), INT_MIN, I32)

        def count(pred):
            def body(cc, acc):
                for u in range(2):
                    c = 2 * cc + u
                    acc = acc + _colsum8(jnp.where(pred(c, keys[c]), 1.0, 0.0))
                return acc
            acc = lax.fori_loop(0, (nkb + 1) // 2, body, jnp.zeros((SUBLANES, tq), F32))
            return jnp.sum(acc, axis=0, keepdims=True)

        nwords = planes_sc.shape[1] * SUBLANES
        word_block = lax.broadcasted_iota(I32, (nwords, 1), 0) // SUBLANES
        cand = jnp.broadcast_to(jnp.where(word_block < nkb, -1, 0), (nwords, tq))

        def ones_count(words):
            cnt = lax.population_count(words).reshape(nwords // SUBLANES, SUBLANES, tq).sum(axis=0)
            return jnp.sum(cnt.astype(F32), axis=0, keepdims=True)

        thr_u = jnp.zeros((1, tq), I32)
        above = jnp.zeros((1, tq), F32)
        for b in range(KEY_BITS - 1, -1, -1):
            plane = planes_sc[b].reshape(nwords, tq)
            ones = cand & plane
            cnt1 = ones_count(ones)
            take = above + cnt1 >= n_sel
            thr_u = jnp.where(take, thr_u | (INT_MIN if b == KEY_BITS - 1 else 1 << b), thr_u)
            above = jnp.where(take, above, above + cnt1)
            cand = jnp.where(take, ones, cand ^ ones)
        cnt_thr = jnp.where(thr_u == 0, 0.0, above + ones_count(cand))
        thr = jnp.maximum(thr_u ^ INT_MIN, INT_MIN + 1)

        @pl.when(jnp.max(cnt_thr) > n_sel)
        def _():
            need = n_sel - count(lambda c, key: key > thr)
            pos = jnp.zeros((1, tq), I32)
            for b in range(idx_bits - 1, -1, -1):
                cand = pos | (1 << b)
                cnt = count(lambda c, key: jnp.where(key == thr, c * tk + kiota, 2 ** 30) < cand)
                pos = jnp.where(cnt < need, cand, pos)

            def demote(c, carry):
                key = keys[c]
                tie_pos = jnp.where(key == thr, c * tk + kiota, -1)
                keys[c] = jnp.where(tie_pos > pos, INT_MIN, key)
                return carry

            lax.fori_loop(0, nkb, demote, 0)

        return thr

    thrs = [select(0), select(1)]

    m_sc[...] = jnp.full(m_sc.shape, MASKED, F32)
    acc_sc[...] = jnp.zeros(acc_sc.shape, F32)
    buf_a = (sa_sc, mxa_sc)
    buf_b = (sb_sc, mxb_sc)

    def select_bias(c, sub):
        return jnp.where(key_sc[sub, c] >= thrs[sub], 0.0, MASKED)

    def scores(c, sub, h, buf, bias):
        n = h // A_GROUP
        r0 = pl.multiple_of(c * tk, tk)
        s = jnp.dot(k_ref[pl.ds(r0, tk), n * LANES:(n + 1) * LANES], qw_sc[sub * A_HEADS + h],
                    preferred_element_type=F32) + bias
        buf[0][sub * A_HEADS + h] = s
        buf[1][sub * A_HEADS + h] = _colmax8(s)

    def update(c, sub, h, buf):
        n = h // A_GROUP
        _softmax_pv(buf[0], buf[1], vt_ref[c, n * V_ROWS:(n + 1) * V_ROWS, :],
                    sub * A_HEADS + h, m_sc, acc_sc)

    def stage(c, cur, nxt, score_subs, update_subs=(0, 1)):
        biases = {sub: select_bias(c + 1, sub) for sub in score_subs}
        for h in range(A_HEADS):
            for sub in score_subs:
                scores(c + 1, sub, h, nxt, biases[sub])
            for sub in update_subs:
                update(c, sub, h, cur)

    for sub in range(2):
        bias0 = select_bias(0, sub)
        for h in range(A_HEADS):
            scores(0, sub, h, buf_a, bias0)

    def attn_body(i, carry):
        stage(2 * i, buf_a, buf_b, (0, 1))
        stage(2 * i + 1, buf_b, buf_a, (0, 1))
        return carry

    lax.fori_loop(0, jj, attn_body, 0)
    last = 2 * jj
    stage(last, buf_a, buf_b, (1,))
    stage(last + 1, buf_b, None, (), update_subs=(1,))

    for sub in range(2):
        for h in range(A_HEADS):
            o_ref[h * HEAD_DIM:(h + 1) * HEAD_DIM, sub * tq:(sub + 1) * tq] = (
                _normalised(acc_sc, sub * A_HEADS + h).astype(o_ref.dtype))


def _dsa(qt, qit, wit, ki, k, vt, b, s, tq):
    t = b * s
    nq = s // tq
    assert nq % 2 == 0, "dsa_attention walks query blocks in pairs"
    n_sel = min(TOPK_MAX, s // 4)
    idx_bits = max(1, (s - 1).bit_length())
    assert tq == KEY_BITS * SUBLANES, "the bit-sliced search transposes 32 sublane tiles per key block"
    qcol = lambda n: pl.BlockSpec((n, 2 * tq), lambda bi, j: (0, bi * (nq // 2) + j))
    full = lambda n: pl.BlockSpec((s, n), lambda bi, j: (bi, 0))
    return pl.pallas_call(
        functools.partial(_dsa_kernel, tq=tq, n_sel=float(n_sel), idx_bits=idx_bits),
        grid=(b, nq // 2),
        in_specs=[qcol(A_Q), qcol(A_QI), qcol(IDX_HEADS), full(LANES), full(A_KD),
                  pl.BlockSpec((nq, A_VT, tq), lambda bi, j: (bi, 0, 0))],
        out_specs=pl.BlockSpec((A_Q, 2 * tq), lambda bi, j: (0, bi * (nq // 2) + j)),
        out_shape=jax.ShapeDtypeStruct((A_Q, t), BF16),
        scratch_shapes=[
            pltpu.VMEM((2, nq, tq, tq), I32),
            pltpu.VMEM((KEY_BITS, nq, SUBLANES, tq), I32),
            pltpu.VMEM((2 * IDX_HEADS, LANES, tq), BF16),
            pltpu.VMEM((2 * A_HEADS, LANES, tq), BF16),
            pltpu.VMEM((2 * A_HEADS, tq, tq), F32),
            pltpu.VMEM((2 * A_HEADS, tq, tq), F32),
            pltpu.VMEM((2 * A_HEADS, SUBLANES, tq), F32),
            pltpu.VMEM((2 * A_HEADS, SUBLANES, tq), F32),
            pltpu.VMEM((2 * A_HEADS, 1, tq), F32),
            pltpu.VMEM((2 * A_HEADS, V_ROWS, tq), F32),
        ],
        compiler_params=_params("parallel", "arbitrary"),
        name="dsa_attention",
    )(qt, qit, wit, ki, k, vt)


B_QK = B_HEADS * HEAD_DIM
B_KAUG = B_HEADS * LANES
B_VT = B_HEADS * V_ROWS
BIAS_PIECES = 3


def _split3(x):
    hi = x.astype(BF16)
    r1 = x - hi.astype(F32)
    mid = r1.astype(BF16)
    low = (r1 - mid.astype(F32)).astype(BF16)
    return hi, mid, low


def _kvproj_kernel(h_ref, g_ref, wk_ref, wv_ref, wf_ref, bf_ref, place_ref, ones_ref,
                   qplace_ref, qones_ref, kaug_ref, vt_ref, qrows_ref, carry_sc):
    tm = h_ref.shape[0]
    tk = vt_ref.shape[2]

    @pl.when(pl.program_id(1) == 0)
    def _():
        carry_sc[...] = jnp.zeros(carry_sc.shape, F32)

    xn = _rms(h_ref[...], g_ref[...]).astype(BF16)
    z = jnp.dot(xn, wf_ref[...], preferred_element_type=F32) + bf_ref[...]
    logf = jnp.minimum(z, 0.0) - jnp.log1p(jnp.exp(-jnp.abs(z)))
    tri = (lax.broadcasted_iota(I32, (tm, tm), 0) >=
           lax.broadcasted_iota(I32, (tm, tm), 1)).astype(BF16)
    cs = sum(jnp.dot(tri, piece, preferred_element_type=F32) for piece in _split3(logf))
    cum = cs + carry_sc[...]
    carry_sc[...] = cum[tm - 1:tm, :]
    hi, mid, low = _split3(cum * LOG2E)
    pieces = jnp.concatenate([hi, mid, low], axis=1)
    qrows = lax.dot_general(qplace_ref[...], pieces, (((1,), (1,)), ((), ())),
                            preferred_element_type=F32) + qones_ref[...]
    qrows_ref[...] = qrows.astype(BF16)

    step = 4 * LANES
    for c0 in range(0, B_KAUG, step):
        y = (jnp.dot(xn, wk_ref[:, c0:c0 + step], preferred_element_type=F32) +
             jnp.dot(pieces, place_ref[:, c0:c0 + step], preferred_element_type=F32) +
             ones_ref[:, c0:c0 + step])
        kaug_ref[:, c0:c0 + step] = y.astype(BF16)
    for c0 in range(0, B_QK, step):
        y = jnp.dot(xn, wv_ref[:, c0:c0 + step], preferred_element_type=F32)
        for jj in range(step // LANES):
            head = (c0 + jj * LANES) // HEAD_DIM
            for sb in range(tm // tk):
                yt = y[sb * tk:(sb + 1) * tk, jj * LANES:(jj + 1) * LANES].T.astype(BF16)
                for e in range(2):
                    _store_value_head(vt_ref, sb, head + e, yt[e * HEAD_DIM:(e + 1) * HEAD_DIM, :])


def _kvproj(h3, g, wk, wv, wf, bf, place, ones, qplace, qones, tm, tk):
    b, s, d = h3.shape
    ns = s // tm
    blk = lambda n: pl.BlockSpec((None, tm, n), lambda bi, i: (bi, i, 0))
    return pl.pallas_call(
        _kvproj_kernel,
        grid=(b, ns),
        in_specs=[blk(d), _resident((1, d)), _resident(wk.shape), _resident(wv.shape),
                  _resident(wf.shape), _resident((1, LANES)), _resident(place.shape),
                  _resident(ones.shape), _resident(qplace.shape), _resident(qones.shape)],
        out_specs=[blk(B_KAUG),
                   pl.BlockSpec((tm // tk, B_VT, tk), lambda bi, i: (bi * ns + i, 0, 0)),
                   pl.BlockSpec((B_HEADS * BF16_ROWS, tm), lambda bi, i: (0, bi * ns + i))],
        out_shape=[jax.ShapeDtypeStruct((b, s, B_KAUG), BF16),
                   jax.ShapeDtypeStruct((b * s // tk, B_VT, tk), BF16),
                   jax.ShapeDtypeStruct((B_HEADS * BF16_ROWS, b * s), BF16)],
        scratch_shapes=[pltpu.VMEM((1, LANES), F32)],
        compiler_params=_params("parallel", "arbitrary"),
        name="kv_proj_cumsum",
    )(h3, g, wk, wv, wf, bf, place, ones, qplace, qones)


def _qproj_kernel(h_ref, g_ref, w_ref, rows_ref, qt_ref):
    tm = h_ref.shape[0]
    xn = _rms(h_ref[...], g_ref[...]).astype(BF16)
    pad = jnp.zeros((LANES - HEAD_DIM - BF16_ROWS, tm), BF16)
    step = 4 * LANES
    for c0 in range(0, B_QK, step):
        y = jnp.dot(xn, w_ref[:, c0:c0 + step], preferred_element_type=F32) * QK_SCALE
        for jj in range(step // LANES):
            yt = y[:, jj * LANES:(jj + 1) * LANES].T.astype(BF16)
            for e in range(2):
                h = (c0 + jj * LANES) // HEAD_DIM + e
                r = h * LANES
                qt_ref[r:r + HEAD_DIM, :] = yt[e * HEAD_DIM:(e + 1) * HEAD_DIM, :]
                qt_ref[r + HEAD_DIM:r + HEAD_DIM + BF16_ROWS, :] = (
                    rows_ref[h * BF16_ROWS:(h + 1) * BF16_ROWS, :])
                qt_ref[r + HEAD_DIM + BF16_ROWS:r + LANES, :] = pad


def _qproj(h, g, w, rows, tm):
    t, d = h.shape
    return pl.pallas_call(
        _qproj_kernel,
        grid=(t // tm,),
        in_specs=[pl.BlockSpec((tm, d), lambda i: (i, 0)), _resident((1, d)), _resident(w.shape),
                  pl.BlockSpec((B_HEADS * BF16_ROWS, tm), lambda i: (0, i))],
        out_specs=pl.BlockSpec((B_KAUG, tm), lambda i: (0, i)),
        out_shape=jax.ShapeDtypeStruct((B_KAUG, t), BF16),
        compiler_params=_params("parallel"),
        name="q_proj",
    )(h, g, w, rows)


FOX_HEADS_PER_STEP = 8


def _fox_kernel(qt_ref, k_ref, vt_ref, o_ref, sa_sc, sb_sc, mxa_sc, mxb_sc, m_sc, acc_sc, *, tq):
    tk = tq
    hps = FOX_HEADS_PER_STEP
    jj = pl.program_id(2)
    m_sc[...] = jnp.full(m_sc.shape, MASKED, F32)
    acc_sc[...] = jnp.zeros(acc_sc.shape, F32)
    kiota = lax.broadcasted_iota(I32, (tk, tq), 0)
    qiota = lax.broadcasted_iota(I32, (1, tq), 1)
    buf_a = (sa_sc, mxa_sc)
    buf_b = (sb_sc, mxb_sc)

    def scores(c, sub, h, buf, qlimit):
        r0 = pl.multiple_of(c * tk, tk)
        s = jnp.dot(k_ref[pl.ds(r0, tk), h * LANES:(h + 1) * LANES],
                    qt_ref[h * LANES:(h + 1) * LANES, sub * tq:(sub + 1) * tq],
                    preferred_element_type=F32)
        if qlimit is not None:
            s = jnp.where(kiota <= qlimit, s, MASKED)
        buf[0][sub * hps + h] = s
        buf[1][sub * hps + h] = _colmax8(s)

    def update(c, sub, h, buf):
        _softmax_pv(buf[0], buf[1], vt_ref[c, h * V_ROWS:(h + 1) * V_ROWS, :],
                    sub * hps + h, m_sc, acc_sc)

    def stage(c, cur, nxt, limits, update_subs=(0, 1)):
        for h in range(hps):
            for sub, qlimit in limits:
                scores(c + 1, sub, h, nxt, qlimit)
            for sub in update_subs:
                update(c, sub, h, cur)

    first_limit = qiota + jnp.where(jj == 0, 0, tk)
    for h in range(hps):
        scores(0, 0, h, buf_a, first_limit)
        scores(0, 1, h, buf_a, None)

    def main_body(i, carry):
        c = 2 * i
        stage(c, buf_a, buf_b, ((0, None), (1, None)))
        limit0 = qiota + jnp.where(i == jj - 1, 0, tk)
        stage(c + 1, buf_b, buf_a, ((0, limit0), (1, None)))
        return carry

    lax.fori_loop(0, jj, main_body, 0)
    last = 2 * jj
    stage(last, buf_a, buf_b, ((1, qiota),))
    stage(last + 1, buf_b, None, (), update_subs=(1,))
    for sub in range(2):
        for h in range(hps):
            o_ref[h * HEAD_DIM:(h + 1) * HEAD_DIM, sub * tq:(sub + 1) * tq] = (
                _normalised(acc_sc, sub * hps + h).astype(o_ref.dtype))


def _fox(qt, kaug, vt, b, s, tq):
    t = b * s
    nq = s // tq
    assert nq % 2 == 0, "fox_attention walks query blocks in pairs"
    hps = FOX_HEADS_PER_STEP
    nstate = 2 * hps
    return pl.pallas_call(
        functools.partial(_fox_kernel, tq=tq),
        grid=(b, B_HEADS // hps, nq // 2),
        in_specs=[
            pl.BlockSpec((hps * LANES, 2 * tq), lambda bi, p, j: (p, bi * (nq // 2) + j)),
            pl.BlockSpec((None, s, hps * LANES), lambda bi, p, j: (bi, 0, p)),
            pl.BlockSpec((nq, hps * V_ROWS, tq), lambda bi, p, j: (bi, p, 0)),
        ],
        out_specs=pl.BlockSpec((hps * HEAD_DIM, 2 * tq), lambda bi, p, j: (p, bi * (nq // 2) + j)),
        out_shape=jax.ShapeDtypeStruct((B_QK, t), BF16),
        scratch_shapes=[
            pltpu.VMEM((nstate, tq, tq), F32),
            pltpu.VMEM((nstate, tq, tq), F32),
            pltpu.VMEM((nstate, SUBLANES, tq), F32),
            pltpu.VMEM((nstate, SUBLANES, tq), F32),
            pltpu.VMEM((nstate, 1, tq), F32),
            pltpu.VMEM((nstate, V_ROWS, tq), F32),
        ],
        compiler_params=_params("parallel", "parallel", "arbitrary"),
        name="fox_attention",
    )(qt, kaug, vt)


def _mlp_kernel(h_ref, o_ref, wo_ref, g_ref, wup_ref, wdn_ref, gf_ref, out_ref, *, tf, final):
    h1 = h_ref[...] + lax.dot_general(o_ref[...], wo_ref[...], (((0,), (0,)), ((), ())),
                                      preferred_element_type=F32)
    xn = _rms(h1, g_ref[...]).astype(BF16)
    acc = h1
    for c0 in range(0, wup_ref.shape[1], tf):
        u = jnp.dot(xn, wup_ref[:, c0:c0 + tf], preferred_element_type=F32)
        a = jnp.square(jnp.maximum(u, 0.0)).astype(BF16)
        acc = acc + jnp.dot(a, wdn_ref[c0:c0 + tf, :], preferred_element_type=F32)
    if final:
        acc = _rms(acc, gf_ref[...])
    out_ref[...] = acc


def _mlp(h, o, wo, g, wup, wdn, gf, final, tm, tf):
    t, d = h.shape
    row = pl.BlockSpec((tm, d), lambda i: (i, 0))
    return pl.pallas_call(
        functools.partial(_mlp_kernel, tf=tf, final=final),
        grid=(t // tm,),
        in_specs=[row, pl.BlockSpec((o.shape[0], tm), lambda i: (0, i)), _resident(wo.shape),
                  _resident((1, d)), _resident(wup.shape), _resident(wdn.shape), _resident((1, d))],
        out_specs=row,
        out_shape=jax.ShapeDtypeStruct((t, d), F32),
        compiler_params=_params("parallel"),
        name="oproj_mlp",
    )(h, o, wo, g, wup, wdn, gf)


def _dup_heads(w, heads):
    d = w.shape[0]
    w = w.reshape(d, heads, HEAD_DIM)
    return jnp.concatenate([w, w], axis=-1).reshape(d, heads * LANES)


def _prep_w_in_a(w):
    kv = A_KV_HEADS * HEAD_DIM
    o_k, o_v, o_qi = A_Q, A_Q + kv, A_Q + 2 * kv
    o_ki, o_wi = o_qi + A_QI, o_qi + A_QI + HEAD_DIM
    wi = w[:, o_wi:o_wi + IDX_HEADS]
    return jnp.concatenate([
        w[:, :A_Q],
        _dup_heads(w[:, o_k:o_v], A_KV_HEADS),
        w[:, o_v:o_qi],
        w[:, o_qi:o_ki],
        _dup_heads(w[:, o_ki:o_wi], 1),
        jnp.pad(wi, ((0, 0), (0, LANES - IDX_HEADS))),
    ], axis=1).astype(BF16)


def _rope_tables(seq):
    inv = 1.0 / (ROPE_THETA ** (jnp.arange(0, HEAD_DIM, 2, dtype=F32) / HEAD_DIM))
    ang = jnp.arange(seq, dtype=F32)[:, None] * inv[None, :]
    c, s = jnp.cos(ang), jnp.sin(ang)
    return jnp.concatenate([c, c, c, c], axis=1), jnp.concatenate([-s, s, -s, s], axis=1)


def _bias_placement():
    place = np.zeros((BIAS_PIECES * LANES, B_KAUG), np.float32)
    ones = np.zeros((1, B_KAUG), np.float32)
    for h in range(B_HEADS):
        for p in range(BIAS_PIECES):
            place[p * LANES + h, h * LANES + HEAD_DIM + p] = -1.0
            ones[0, h * LANES + HEAD_DIM + BIAS_PIECES + p] = 1.0
    return jnp.asarray(place, BF16), jnp.asarray(ones, F32)


def _query_bias_placement(tm):
    qplace = np.zeros((B_HEADS * BF16_ROWS, BIAS_PIECES * LANES), np.float32)
    qones = np.zeros((B_HEADS * BF16_ROWS, tm), np.float32)
    for h in range(B_HEADS):
        for p in range(BIAS_PIECES):
            qplace[h * BF16_ROWS + BIAS_PIECES + p, p * LANES + h] = 1.0
            qones[h * BF16_ROWS + p, :] = 1.0
    return jnp.asarray(qplace, BF16), jnp.asarray(qones, F32)


def kernel(x, g_attn_a, w_in_a, w_o_a, g_kv, w_kv_b, b_f, g_attn_b, w_q_b, w_o_b,
           g_mlp, w_up, w_down, g_final):
    b, s, d = x.shape
    t = b * s
    depth = g_mlp.shape[0]
    n_a = g_attn_a.shape[0]
    tm = min(512, s)
    tq = min(ATTN_BLOCK, s)
    tf = 512

    cos, sin = _rope_tables(s)
    gf = g_final.reshape(1, d)
    h = x.reshape(t, d)
    kaug = vt_b = bias_rows = None

    for layer in range(depth):
        if layer < n_a:
            qt, k, vt, qit, ki, wit = _aproj(h, g_attn_a[layer].reshape(1, d),
                                             _prep_w_in_a(w_in_a[layer]), cos, sin, s, tm, tq)
            o = _dsa(qt, qit, wit, ki, k, vt, b, s, tq)
            w_o = w_o_a[layer]
        else:
            jb = layer - n_a
            if jb == 0:
                wk = jnp.pad(w_kv_b[:, :B_QK].reshape(d, B_HEADS, HEAD_DIM),
                             ((0, 0), (0, 0), (0, LANES - HEAD_DIM))).reshape(d, B_KAUG)
                wf = jnp.pad(w_kv_b[:, 2 * B_QK:], ((0, 0), (0, LANES - B_HEADS)))
                bias = jnp.pad(b_f, (0, LANES - B_HEADS)).reshape(1, LANES)
                place, ones = _bias_placement()
                qplace, qones = _query_bias_placement(tm)
                kaug, vt_b, bias_rows = _kvproj(
                    h.reshape(b, s, d), g_kv.reshape(1, d), wk.astype(BF16),
                    w_kv_b[:, B_QK:2 * B_QK].astype(BF16), wf.astype(BF16), bias,
                    place, ones, qplace, qones, tm, tq)
            qt = _qproj(h, g_attn_b[jb].reshape(1, d), w_q_b[jb].astype(BF16), bias_rows, tm)
            o = _fox(qt, kaug, vt_b, b, s, tq)
            w_o = w_o_b[jb]
        h = _mlp(h, o, w_o.astype(BF16), g_mlp[layer].reshape(1, d),
                 w_up[layer].astype(BF16), w_down[layer].astype(BF16), gf,
                 layer == depth - 1, tm, tf)
    return h.reshape(b, s, d)
```

```python
import functools

import numpy as np
import jax
import jax.numpy as jnp
from jax import lax
from jax.experimental import pallas as pl
from jax.experimental.pallas import tpu as pltpu

LANES = 128
SUBLANES = 8
BF16_ROWS = 16
HEAD_DIM = 64
ROPE_HALF = HEAD_DIM // 2
CHUNK = 64
A_HEADS = 16
A_KV_HEADS = 4
A_GROUP = A_HEADS // A_KV_HEADS
IDX_HEADS = 8
TOPK_MAX = 256
B_HEADS = 16
ROPE_THETA = 10000.0
EPS = 1e-6
INT_MIN = -(2 ** 31)
MASKED = -1e30
LOG2E = 1.4426950408889634
QK_SCALE = HEAD_DIM ** -0.5 * LOG2E
VMEM_LIMIT_BYTES = 56 * 1024 * 1024
ATTN_BLOCK = 256

F32 = jnp.float32
BF16 = jnp.bfloat16
I32 = jnp.int32


def _params(*sem):
    return pltpu.CompilerParams(dimension_semantics=sem,
                                vmem_limit_bytes=VMEM_LIMIT_BYTES)


def _resident(shape):
    nd = len(shape)
    return pl.BlockSpec(shape, lambda *_: (0,) * nd, pipeline_mode=pl.Buffered(1))


def _rms(x, g):
    return x * lax.rsqrt(jnp.mean(x * x, axis=-1, keepdims=True) + EPS) * g


def _colsum8(x):
    n, w = x.shape
    return x.reshape(n // SUBLANES, SUBLANES, w).sum(axis=0)


def _colmax8(x):
    n, w = x.shape
    return x.reshape(n // SUBLANES, SUBLANES, w).max(axis=0)


V_ROWS = HEAD_DIM + BF16_ROWS


def _store_value_head(vt_ref, sb, head, yt):
    r = head * V_ROWS
    tk = yt.shape[1]
    vt_ref[sb, r:r + HEAD_DIM, :] = yt
    first_row = lax.broadcasted_iota(I32, (BF16_ROWS, tk), 0) == 0
    vt_ref[sb, r + HEAD_DIM:r + V_ROWS, :] = jnp.where(first_row, 1.0, 0.0).astype(BF16)


def _softmax_pv(s_buf, mx_buf, vt, h, m_sc, acc_sc):
    m_old = m_sc[h]
    m_new = jnp.maximum(m_old, jnp.max(mx_buf[h], axis=0, keepdims=True))
    alpha = jnp.exp2(m_old - m_new)
    p = jnp.exp2(s_buf[h] - m_new)
    acc_sc[h] = alpha * acc_sc[h] + jnp.dot(vt, p.astype(BF16), preferred_element_type=F32)
    m_sc[h] = m_new


def _normalised(acc_sc, h):
    acc = acc_sc[h]
    return acc[0:HEAD_DIM, :] / acc[HEAD_DIM:HEAD_DIM + 1, :]


A_Q = A_HEADS * HEAD_DIM
A_KD = A_KV_HEADS * LANES
A_V = A_KV_HEADS * HEAD_DIM
A_VT = A_KV_HEADS * V_ROWS
A_QI = IDX_HEADS * HEAD_DIM
A_OFF_K = A_Q
A_OFF_V = A_OFF_K + A_KD
A_OFF_QI = A_OFF_V + A_V
A_OFF_KI = A_OFF_QI + A_QI
A_OFF_WI = A_OFF_KI + LANES
A_COLS = A_OFF_WI + LANES


def _aproj_kernel(h_ref, g_ref, w_ref, cos_ref, sin_ref,
                  qt_ref, k_ref, vt_ref, qit_ref, ki_ref, wit_ref, *, wi_scale):
    tm = h_ref.shape[0]
    tk = vt_ref.shape[2]
    xn = _rms(h_ref[...], g_ref[...]).astype(BF16)
    cos = cos_ref[...]
    sin = sin_ref[...]
    lane = lax.broadcasted_iota(I32, (tm, LANES), 1)
    first_half = (lane & ROPE_HALF) == 0

    def rope(y):
        ys = jnp.where(first_half, pltpu.roll(y, LANES - ROPE_HALF, 1),
                       pltpu.roll(y, ROPE_HALF, 1))
        return y * cos + ys * sin

    def chunks(c0, n):
        step = 4 * LANES
        for s0 in range(0, n, step):
            sn = min(step, n - s0)
            y = jnp.dot(xn, w_ref[:, c0 + s0:c0 + s0 + sn], preferred_element_type=F32)
            for j in range(sn // LANES):
                yield (s0 // LANES + j), y[:, j * LANES:(j + 1) * LANES]

    for j, y in chunks(0, A_Q):
        qt_ref[j * LANES:(j + 1) * LANES, :] = (rope(y) * QK_SCALE).T.astype(BF16)
    for j, y in chunks(A_OFF_K, A_KD):
        k_ref[:, j * LANES:(j + 1) * LANES] = rope(y).astype(BF16)
    for j, y in chunks(A_OFF_V, A_V):
        for sb in range(tm // tk):
            yt = y[sb * tk:(sb + 1) * tk].T.astype(BF16)
            for e in range(2):
                _store_value_head(vt_ref, sb, 2 * j + e, yt[e * HEAD_DIM:(e + 1) * HEAD_DIM, :])
    for j, y in chunks(A_OFF_QI, A_QI):
        qit_ref[j * LANES:(j + 1) * LANES, :] = rope(y).T.astype(BF16)
    for j, y in chunks(A_OFF_KI, LANES):
        ki_ref[...] = rope(y).astype(BF16)
    for j, y in chunks(A_OFF_WI, LANES):
        wit_ref[...] = (y * wi_scale).T[:IDX_HEADS, :]


def _aproj(h, g, w, cos, sin, seq, tm, tk):
    t, d = h.shape
    npos = seq // tm
    row = lambda n: pl.BlockSpec((tm, n), lambda i: (i, 0))
    col = lambda n: pl.BlockSpec((n, tm), lambda i: (0, i))
    tab = pl.BlockSpec((tm, LANES), lambda i: (i % npos, 0))
    wi_scale = IDX_HEADS ** -0.5 * HEAD_DIM ** -0.5
    return pl.pallas_call(
        functools.partial(_aproj_kernel, wi_scale=wi_scale),
        grid=(t // tm,),
        in_specs=[row(d), _resident((1, d)), _resident((d, A_COLS)), tab, tab],
        out_specs=[col(A_Q), row(A_KD),
                   pl.BlockSpec((tm // tk, A_VT, tk), lambda i: (i, 0, 0)),
                   col(A_QI), row(LANES), col(IDX_HEADS)],
        out_shape=[jax.ShapeDtypeStruct((A_Q, t), BF16),
                   jax.ShapeDtypeStruct((t, A_KD), BF16),
                   jax.ShapeDtypeStruct((t // tk, A_VT, tk), BF16),
                   jax.ShapeDtypeStruct((A_QI, t), BF16),
                   jax.ShapeDtypeStruct((t, LANES), BF16),
                   jax.ShapeDtypeStruct((IDX_HEADS, t), F32)],
        compiler_params=_params("parallel"),
        name="a_proj",
    )(h, g, w, cos, sin)


KEY_BITS = 32
BIT_TRANSPOSE_STEPS = (16, 8, 4, 2, 1)
BIT_TRANSPOSE_MASKS = (0x0000FFFF, 0x00FF00FF, 0x0F0F0F0F, 0x33333333, 0x55555555)


def _dsa_kernel(qt_ref, qit_ref, wit_ref, ki_ref, k_ref, vt_ref, o_ref,
                key_sc, planes_sc, qiw_sc, qw_sc, sa_sc, sb_sc, mxa_sc, mxb_sc, m_sc, acc_sc,
                *, tq, n_sel, idx_bits):
    tk = tq
    jj = pl.program_id(1)
    kiota = lax.broadcasted_iota(I32, (tk, tq), 0)
    zeros_half = jnp.zeros((HEAD_DIM, tq), BF16)

    for sub in range(2):
        cols = slice(sub * tq, (sub + 1) * tq)
        for h in range(IDX_HEADS):
            qiw_sc[sub * IDX_HEADS + h, 0:HEAD_DIM, :] = qit_ref[h * HEAD_DIM:(h + 1) * HEAD_DIM, cols]
            qiw_sc[sub * IDX_HEADS + h, HEAD_DIM:LANES, :] = zeros_half
        for h in range(A_HEADS):
            qw_sc[sub * A_HEADS + h, 0:HEAD_DIM, :] = qt_ref[h * HEAD_DIM:(h + 1) * HEAD_DIM, cols]
            qw_sc[sub * A_HEADS + h, HEAD_DIM:LANES, :] = zeros_half

    def select(sub):
        j = 2 * jj + sub
        nkb = j + 1
        keys = key_sc.at[sub]
        qpos = j * tq + lax.broadcasted_iota(I32, (1, tq), 1)
        limit = (qpos // CHUNK + 1) * CHUNK

        def head_logits(c):
            kc = ki_ref[pl.ds(pl.multiple_of(c * tk, tk), tk), :]
            return [jnp.dot(kc, qiw_sc[sub * IDX_HEADS + h], preferred_element_type=F32)
                    for h in range(IDX_HEADS)]

        def store_keys(c, logits):
            sc = jnp.zeros((tk, tq), F32)
            for h in range(IDX_HEADS):
                sc = sc + jnp.maximum(logits[h], 0.0) * wit_ref[h:h + 1, sub * tq:(sub + 1) * tq]
            sc = jnp.where(sc == 0.0, 0.0, sc)
            bits = pltpu.bitcast(sc, I32)
            key = bits ^ ((bits >> 31) & 0x7FFFFFFF)
            keys[c] = jnp.where(c * tk + kiota < limit, key, INT_MIN)

        def bit_slice(c):
            x = keys[c] ^ INT_MIN
            tiles = [x[v * SUBLANES:(v + 1) * SUBLANES, :] for v in range(KEY_BITS)]
            for jbit, mask in zip(BIT_TRANSPOSE_STEPS, BIT_TRANSPOSE_MASKS):
                for v in range(KEY_BITS):
                    if v & jbit == 0:
                        lo, hi = tiles[v], tiles[v + jbit]
                        t = (lax.shift_right_logical(lo, jbit) ^ hi) & mask
                        tiles[v] = lo ^ jnp.left_shift(t, jbit)
                        tiles[v + jbit] = hi ^ t
            for b in range(KEY_BITS):
                planes_sc[b, c] = tiles[b]

        store_keys(0, head_logits(0))

        def score_body(c, carry):
            logits = head_logits(c)
            bit_slice(c - 1)
            store_keys(c, logits)
            return carry

        lax.fori_loop(1, nkb, score_body, 0)
        bit_slice(nkb - 1)

        def clear_body(c, carry):
            planes_sc[:, c] = jnp.zeros((KEY_BITS, SUBLANES, tq), I32)
            return carry

        lax.fori_loop(nkb, planes_sc.shape[1], clear_body, 0)

        @pl.when(nkb % 2 == 1)
        def _():
            keys[nkb] = jnp.full((tk, tq), INT_MIN, I32)

        def count(pred):
            def body(cc, acc):
                for u in range(2):
                    c = 2 * cc + u
                    acc = acc + _colsum8(jnp.where(pred(c, keys[c]), 1.0, 0.0))
                return acc
            acc = lax.fori_loop(0, (nkb + 1) // 2, body, jnp.zeros((SUBLANES, tq), F32))
            return jnp.sum(acc, axis=0, keepdims=True)

        nwords = planes_sc.shape[1] * SUBLANES
        word_block = lax.broadcasted_iota(I32, (nwords, 1), 0) // SUBLANES
        cand = jnp.broadcast_to(jnp.where(word_block < nkb, -1, 0), (nwords, tq))

        def ones_count(words):
            cnt = lax.population_count(words).reshape(nwords // SUBLANES, SUBLANES, tq).sum(axis=0)
            return jnp.sum(cnt.astype(F32), axis=0, keepdims=True)

        thr_u = jnp.zeros((1, tq), I32)
        above = jnp.zeros((1, tq), F32)
        for b in range(KEY_BITS - 1, -1, -1):
            plane = planes_sc[b].reshape(nwords, tq)
            ones = cand & plane
            cnt1 = ones_count(ones)
            take = above + cnt1 >= n_sel
            thr_u = jnp.where(take, thr_u | (INT_MIN if b == KEY_BITS - 1 else 1 << b), thr_u)
            above = jnp.where(take, above, above + cnt1)
            cand = jnp.where(take, ones, cand ^ ones)
        cnt_thr = jnp.where(thr_u == 0, 0.0, above + ones_count(cand))
        thr = jnp.maximum(thr_u ^ INT_MIN, INT_MIN + 1)

        @pl.when(jnp.max(cnt_thr) > n_sel)
        def _():
            need = n_sel - count(lambda c, key: key > thr)
            pos = jnp.zeros((1, tq), I32)
            for b in range(idx_bits - 1, -1, -1):
                cand = pos | (1 << b)
                cnt = count(lambda c, key: jnp.where(key == thr, c * tk + kiota, 2 ** 30) < cand)
                pos = jnp.where(cnt < need, cand, pos)

            def demote(c, carry):
                key = keys[c]
                tie_pos = jnp.where(key == thr, c * tk + kiota, -1)
                keys[c] = jnp.where(tie_pos > pos, INT_MIN, key)
                return carry

            lax.fori_loop(0, nkb, demote, 0)

        return thr

    thrs = [select(0), select(1)]

    m_sc[...] = jnp.full(m_sc.shape, MASKED, F32)
    acc_sc[...] = jnp.zeros(acc_sc.shape, F32)
    buf_a = (sa_sc, mxa_sc)
    buf_b = (sb_sc, mxb_sc)

    def select_bias(c, sub):
        return jnp.where(key_sc[sub, c] >= thrs[sub], 0.0, MASKED)

    def scores(c, sub, h, buf, bias):
        n = h // A_GROUP
        r0 = pl.multiple_of(c * tk, tk)
        s = jnp.dot(k_ref[pl.ds(r0, tk), n * LANES:(n + 1) * LANES], qw_sc[sub * A_HEADS + h],
                    preferred_element_type=F32) + bias
        buf[0][sub * A_HEADS + h] = s
        buf[1][sub * A_HEADS + h] = _colmax8(s)

    def update(c, sub, h, buf):
        n = h // A_GROUP
        _softmax_pv(buf[0], buf[1], vt_ref[c, n * V_ROWS:(n + 1) * V_ROWS, :],
                    sub * A_HEADS + h, m_sc, acc_sc)

    def stage(c, cur, nxt, score_subs, update_subs=(0, 1)):
        biases = {sub: select_bias(c + 1, sub) for sub in score_subs}
        for h in range(A_HEADS):
            for sub in score_subs:
                scores(c + 1, sub, h, nxt, biases[sub])
            for sub in update_subs:
                update(c, sub, h, cur)

    for sub in range(2):
        bias0 = select_bias(0, sub)
        for h in range(A_HEADS):
            scores(0, sub, h, buf_a, bias0)

    def attn_body(i, carry):
        stage(2 * i, buf_a, buf_b, (0, 1))
        stage(2 * i + 1, buf_b, buf_a, (0, 1))
        return carry

    lax.fori_loop(0, jj, attn_body, 0)
    last = 2 * jj
    stage(last, buf_a, buf_b, (1,))
    stage(last + 1, buf_b, None, (), update_subs=(1,))

    for sub in range(2):
        for h in range(A_HEADS):
            o_ref[h * HEAD_DIM:(h + 1) * HEAD_DIM, sub * tq:(sub + 1) * tq] = (
                _normalised(acc_sc, sub * A_HEADS + h).astype(o_ref.dtype))


def _dsa(qt, qit, wit, ki, k, vt, b, s, tq):
    t = b * s
    nq = s // tq
    assert nq % 2 == 0, "dsa_attention walks query blocks in pairs"
    n_sel = min(TOPK_MAX, s // 4)
    idx_bits = max(1, (s - 1).bit_length())
    assert tq == KEY_BITS * SUBLANES, "the bit-sliced search transposes 32 sublane tiles per key block"
    qcol = lambda n: pl.BlockSpec((n, 2 * tq), lambda bi, j: (0, bi * (nq // 2) + j))
    full = lambda n: pl.BlockSpec((s, n), lambda bi, j: (bi, 0))
    return pl.pallas_call(
        functools.partial(_dsa_kernel, tq=tq, n_sel=float(n_sel), idx_bits=idx_bits),
        grid=(b, nq // 2),
        in_specs=[qcol(A_Q), qcol(A_QI), qcol(IDX_HEADS), full(LANES), full(A_KD),
                  pl.BlockSpec((nq, A_VT, tq), lambda bi, j: (bi, 0, 0))],
        out_specs=pl.BlockSpec((A_Q, 2 * tq), lambda bi, j: (0, bi * (nq // 2) + j)),
        out_shape=jax.ShapeDtypeStruct((A_Q, t), BF16),
        scratch_shapes=[
            pltpu.VMEM((2, nq, tq, tq), I32),
            pltpu.VMEM((KEY_BITS, nq, SUBLANES, tq), I32),
            pltpu.VMEM((2 * IDX_HEADS, LANES, tq), BF16),
            pltpu.VMEM((2 * A_HEADS, LANES, tq), BF16),
            pltpu.VMEM((2 * A_HEADS, tq, tq), F32),
            pltpu.VMEM((2 * A_HEADS, tq, tq), F32),
            pltpu.VMEM((2 * A_HEADS, SUBLANES, tq), F32),
            pltpu.VMEM((2 * A_HEADS, SUBLANES, tq), F32),
            pltpu.VMEM((2 * A_HEADS, 1, tq), F32),
            pltpu.VMEM((2 * A_HEADS, V_ROWS, tq), F32),
        ],
        compiler_params=_params("parallel", "arbitrary"),
        name="dsa_attention",
    )(qt, qit, wit, ki, k, vt)


B_QK = B_HEADS * HEAD_DIM
B_KAUG = B_HEADS * LANES
B_VT = B_HEADS * V_ROWS
BIAS_PIECES = 3


def _split3(x):
    hi = x.astype(BF16)
    r1 = x - hi.astype(F32)
    mid = r1.astype(BF16)
    low = (r1 - mid.astype(F32)).astype(BF16)
    return hi, mid, low


def _kvproj_kernel(h_ref, g_ref, wk_ref, wv_ref, wf_ref, bf_ref, place_ref, ones_ref,
                   qplace_ref, qones_ref, kaug_ref, vt_ref, qrows_ref, carry_sc):
    tm = h_ref.shape[0]
    tk = vt_ref.shape[2]

    @pl.when(pl.program_id(1) == 0)
    def _():
        carry_sc[...] = jnp.zeros(carry_sc.shape, F32)

    xn = _rms(h_ref[...], g_ref[...]).astype(BF16)
    z = jnp.dot(xn, wf_ref[...], preferred_element_type=F32) + bf_ref[...]
    logf = jnp.minimum(z, 0.0) - jnp.log1p(jnp.exp(-jnp.abs(z)))
    tri = (lax.broadcasted_iota(I32, (tm, tm), 0) >=
           lax.broadcasted_iota(I32, (tm, tm), 1)).astype(BF16)
    cs = sum(jnp.dot(tri, piece, preferred_element_type=F32) for piece in _split3(logf))
    cum = cs + carry_sc[...]
    carry_sc[...] = cum[tm - 1:tm, :]
    hi, mid, low = _split3(cum * LOG2E)
    pieces = jnp.concatenate([hi, mid, low], axis=1)
    qrows = lax.dot_general(qplace_ref[...], pieces, (((1,), (1,)), ((), ())),
                            preferred_element_type=F32) + qones_ref[...]
    qrows_ref[...] = qrows.astype(BF16)

    step = 4 * LANES
    for c0 in range(0, B_KAUG, step):
        y = (jnp.dot(xn, wk_ref[:, c0:c0 + step], preferred_element_type=F32) +
             jnp.dot(pieces, place_ref[:, c0:c0 + step], preferred_element_type=F32) +
             ones_ref[:, c0:c0 + step])
        for jj in range(step // LANES):
            r = c0 + jj * LANES
            for sb in range(tm // tk):
                kaug_ref[sb, r:r + LANES, :] = (
                    y[sb * tk:(sb + 1) * tk, jj * LANES:(jj + 1) * LANES].T.astype(BF16))
    for c0 in range(0, B_QK, step):
        y = jnp.dot(xn, wv_ref[:, c0:c0 + step], preferred_element_type=F32)
        for jj in range(step // LANES):
            head = (c0 + jj * LANES) // HEAD_DIM
            for sb in range(tm // tk):
                yt = y[sb * tk:(sb + 1) * tk, jj * LANES:(jj + 1) * LANES].T.astype(BF16)
                for e in range(2):
                    _store_value_head(vt_ref, sb, head + e, yt[e * HEAD_DIM:(e + 1) * HEAD_DIM, :])


def _kvproj(h3, g, wk, wv, wf, bf, place, ones, qplace, qones, tm, tk):
    b, s, d = h3.shape
    ns = s // tm
    blk = lambda n: pl.BlockSpec((None, tm, n), lambda bi, i: (bi, i, 0))
    return pl.pallas_call(
        _kvproj_kernel,
        grid=(b, ns),
        in_specs=[blk(d), _resident((1, d)), _resident(wk.shape), _resident(wv.shape),
                  _resident(wf.shape), _resident((1, LANES)), _resident(place.shape),
                  _resident(ones.shape), _resident(qplace.shape), _resident(qones.shape)],
        out_specs=[pl.BlockSpec((tm // tk, B_KAUG, tk), lambda bi, i: (bi * ns + i, 0, 0)),
                   pl.BlockSpec((tm // tk, B_VT, tk), lambda bi, i: (bi * ns + i, 0, 0)),
                   pl.BlockSpec((B_HEADS * BF16_ROWS, tm), lambda bi, i: (0, bi * ns + i))],
        out_shape=[jax.ShapeDtypeStruct((b * s // tk, B_KAUG, tk), BF16),
                   jax.ShapeDtypeStruct((b * s // tk, B_VT, tk), BF16),
                   jax.ShapeDtypeStruct((B_HEADS * BF16_ROWS, b * s), BF16)],
        scratch_shapes=[pltpu.VMEM((1, LANES), F32)],
        compiler_params=_params("parallel", "arbitrary"),
        name="kv_proj_cumsum",
    )(h3, g, wk, wv, wf, bf, place, ones, qplace, qones)


def _qproj_kernel(h_ref, g_ref, w_ref, rows_ref, qt_ref):
    tm = h_ref.shape[0]
    xn = _rms(h_ref[...], g_ref[...]).astype(BF16)
    pad = jnp.zeros((LANES - HEAD_DIM - BF16_ROWS, tm), BF16)
    step = 4 * LANES
    for c0 in range(0, B_QK, step):
        y = jnp.dot(xn, w_ref[:, c0:c0 + step], preferred_element_type=F32) * QK_SCALE
        for jj in range(step // LANES):
            yt = y[:, jj * LANES:(jj + 1) * LANES].T.astype(BF16)
            for e in range(2):
                h = (c0 + jj * LANES) // HEAD_DIM + e
                r = h * LANES
                qt_ref[r:r + HEAD_DIM, :] = yt[e * HEAD_DIM:(e + 1) * HEAD_DIM, :]
                qt_ref[r + HEAD_DIM:r + HEAD_DIM + BF16_ROWS, :] = (
                    rows_ref[h * BF16_ROWS:(h + 1) * BF16_ROWS, :])
                qt_ref[r + HEAD_DIM + BF16_ROWS:r + LANES, :] = pad


def _qproj(h, g, w, rows, tm):
    t, d = h.shape
    return pl.pallas_call(
        _qproj_kernel,
        grid=(t // tm,),
        in_specs=[pl.BlockSpec((tm, d), lambda i: (i, 0)), _resident((1, d)), _resident(w.shape),
                  pl.BlockSpec((B_HEADS * BF16_ROWS, tm), lambda i: (0, i))],
        out_specs=pl.BlockSpec((B_KAUG, tm), lambda i: (0, i)),
        out_shape=jax.ShapeDtypeStruct((B_KAUG, t), BF16),
        compiler_params=_params("parallel"),
        name="q_proj",
    )(h, g, w, rows)


FOX_HEADS_PER_STEP = 8


def _fox_kernel(qt_ref, k_ref, vt_ref, o_ref, qw_sc, sa_sc, sb_sc, mxa_sc, mxb_sc, m_sc, acc_sc,
                *, tq):
    tk = tq
    hps = FOX_HEADS_PER_STEP
    jj = pl.program_id(2)
    m_sc[...] = jnp.full(m_sc.shape, MASKED, F32)
    acc_sc[...] = jnp.zeros(acc_sc.shape, F32)
    kiota = lax.broadcasted_iota(I32, (tk, tq), 0)
    qiota = lax.broadcasted_iota(I32, (1, tq), 1)
    buf_a = (sa_sc, mxa_sc)
    buf_b = (sb_sc, mxb_sc)
    for sub in range(2):
        for h in range(hps):
            qw_sc[sub * hps + h] = qt_ref[h * LANES:(h + 1) * LANES, sub * tq:(sub + 1) * tq]

    def scores(c, sub, h, buf, qlimit):
        s = lax.dot_general(k_ref[c, h * LANES:(h + 1) * LANES, :], qw_sc[sub * hps + h],
                            (((0,), (0,)), ((), ())), preferred_element_type=F32)
        if qlimit is not None:
            s = jnp.where(kiota <= qlimit, s, MASKED)
        buf[0][sub * hps + h] = s
        buf[1][sub * hps + h] = _colmax8(s)

    def update(c, sub, h, buf):
        _softmax_pv(buf[0], buf[1], vt_ref[c, h * V_ROWS:(h + 1) * V_ROWS, :],
                    sub * hps + h, m_sc, acc_sc)

    def stage(c, cur, nxt, limits, update_subs=(0, 1)):
        for h in range(hps):
            for sub, qlimit in limits:
                scores(c + 1, sub, h, nxt, qlimit)
            for sub in update_subs:
                update(c, sub, h, cur)

    first_limit = qiota + jnp.where(jj == 0, 0, tk)
    for h in range(hps):
        scores(0, 0, h, buf_a, first_limit)
        scores(0, 1, h, buf_a, None)

    def main_body(i, carry):
        c = 2 * i
        stage(c, buf_a, buf_b, ((0, None), (1, None)))
        limit0 = qiota + jnp.where(i == jj - 1, 0, tk)
        stage(c + 1, buf_b, buf_a, ((0, limit0), (1, None)))
        return carry

    lax.fori_loop(0, jj, main_body, 0)
    last = 2 * jj
    stage(last, buf_a, buf_b, ((1, qiota),))
    stage(last + 1, buf_b, None, (), update_subs=(1,))
    for sub in range(2):
        for h in range(hps):
            o_ref[h * HEAD_DIM:(h + 1) * HEAD_DIM, sub * tq:(sub + 1) * tq] = (
                _normalised(acc_sc, sub * hps + h).astype(o_ref.dtype))


def _fox(qt, kaug, vt, b, s, tq):
    t = b * s
    nq = s // tq
    assert nq % 2 == 0, "fox_attention walks query blocks in pairs"
    hps = FOX_HEADS_PER_STEP
    nstate = 2 * hps
    return pl.pallas_call(
        functools.partial(_fox_kernel, tq=tq),
        grid=(b, B_HEADS // hps, nq // 2),
        in_specs=[
            pl.BlockSpec((hps * LANES, 2 * tq), lambda bi, p, j: (p, bi * (nq // 2) + j)),
            pl.BlockSpec((nq, hps * LANES, tq), lambda bi, p, j: (bi, p, 0)),
            pl.BlockSpec((nq, hps * V_ROWS, tq), lambda bi, p, j: (bi, p, 0)),
        ],
        out_specs=pl.BlockSpec((hps * HEAD_DIM, 2 * tq), lambda bi, p, j: (p, bi * (nq // 2) + j)),
        out_shape=jax.ShapeDtypeStruct((B_QK, t), BF16),
        scratch_shapes=[
            pltpu.VMEM((nstate, LANES, tq), BF16),
            pltpu.VMEM((nstate, tq, tq), F32),
            pltpu.VMEM((nstate, tq, tq), F32),
            pltpu.VMEM((nstate, SUBLANES, tq), F32),
            pltpu.VMEM((nstate, SUBLANES, tq), F32),
            pltpu.VMEM((nstate, 1, tq), F32),
            pltpu.VMEM((nstate, V_ROWS, tq), F32),
        ],
        compiler_params=_params("parallel", "parallel", "arbitrary"),
        name="fox_attention",
    )(qt, kaug, vt)


def _mlp_kernel(h_ref, o_ref, wo_ref, g_ref, wup_ref, wdn_ref, gf_ref, out_ref, *, tf, final):
    h1 = h_ref[...] + lax.dot_general(o_ref[...], wo_ref[...], (((0,), (0,)), ((), ())),
                                      preferred_element_type=F32)
    xn = _rms(h1, g_ref[...]).astype(BF16)
    acc = h1
    for c0 in range(0, wup_ref.shape[1], tf):
        u = jnp.dot(xn, wup_ref[:, c0:c0 + tf], preferred_element_type=F32)
        a = jnp.square(jnp.maximum(u, 0.0)).astype(BF16)
        acc = acc + jnp.dot(a, wdn_ref[c0:c0 + tf, :], preferred_element_type=F32)
    if final:
        acc = _rms(acc, gf_ref[...])
    out_ref[...] = acc


def _mlp(h, o, wo, g, wup, wdn, gf, final, tm, tf):
    t, d = h.shape
    row = pl.BlockSpec((tm, d), lambda i: (i, 0))
    return pl.pallas_call(
        functools.partial(_mlp_kernel, tf=tf, final=final),
        grid=(t // tm,),
        in_specs=[row, pl.BlockSpec((o.shape[0], tm), lambda i: (0, i)), _resident(wo.shape),
                  _resident((1, d)), _resident(wup.shape), _resident(wdn.shape), _resident((1, d))],
        out_specs=row,
        out_shape=jax.ShapeDtypeStruct((t, d), F32),
        compiler_params=_params("parallel"),
        name="oproj_mlp",
    )(h, o, wo, g, wup, wdn, gf)


def _dup_heads(w, heads):
    d = w.shape[0]
    w = w.reshape(d, heads, HEAD_DIM)
    return jnp.concatenate([w, w], axis=-1).reshape(d, heads * LANES)


def _prep_w_in_a(w):
    kv = A_KV_HEADS * HEAD_DIM
    o_k, o_v, o_qi = A_Q, A_Q + kv, A_Q + 2 * kv
    o_ki, o_wi = o_qi + A_QI, o_qi + A_QI + HEAD_DIM
    wi = w[:, o_wi:o_wi + IDX_HEADS]
    return jnp.concatenate([
        w[:, :A_Q],
        _dup_heads(w[:, o_k:o_v], A_KV_HEADS),
        w[:, o_v:o_qi],
        w[:, o_qi:o_ki],
        _dup_heads(w[:, o_ki:o_wi], 1),
        jnp.pad(wi, ((0, 0), (0, LANES - IDX_HEADS))),
    ], axis=1).astype(BF16)


def _rope_tables(seq):
    inv = 1.0 / (ROPE_THETA ** (jnp.arange(0, HEAD_DIM, 2, dtype=F32) / HEAD_DIM))
    ang = jnp.arange(seq, dtype=F32)[:, None] * inv[None, :]
    c, s = jnp.cos(ang), jnp.sin(ang)
    return jnp.concatenate([c, c, c, c], axis=1), jnp.concatenate([-s, s, -s, s], axis=1)


def _bias_placement():
    place = np.zeros((BIAS_PIECES * LANES, B_KAUG), np.float32)
    ones = np.zeros((1, B_KAUG), np.float32)
    for h in range(B_HEADS):
        for p in range(BIAS_PIECES):
            place[p * LANES + h, h * LANES + HEAD_DIM + p] = -1.0
            ones[0, h * LANES + HEAD_DIM + BIAS_PIECES + p] = 1.0
    return jnp.asarray(place, BF16), jnp.asarray(ones, F32)


def _query_bias_placement(tm):
    qplace = np.zeros((B_HEADS * BF16_ROWS, BIAS_PIECES * LANES), np.float32)
    qones = np.zeros((B_HEADS * BF16_ROWS, tm), np.float32)
    for h in range(B_HEADS):
        for p in range(BIAS_PIECES):
            qplace[h * BF16_ROWS + BIAS_PIECES + p, p * LANES + h] = 1.0
            qones[h * BF16_ROWS + p, :] = 1.0
    return jnp.asarray(qplace, BF16), jnp.asarray(qones, F32)


def kernel(x, g_attn_a, w_in_a, w_o_a, g_kv, w_kv_b, b_f, g_attn_b, w_q_b, w_o_b,
           g_mlp, w_up, w_down, g_final):
    b, s, d = x.shape
    t = b * s
    depth = g_mlp.shape[0]
    n_a = g_attn_a.shape[0]
    tm = min(512, s)
    tq = min(ATTN_BLOCK, s)
    tf = 512

    cos, sin = _rope_tables(s)
    gf = g_final.reshape(1, d)
    h = x.reshape(t, d)
    kaug = vt_b = bias_rows = None

    for layer in range(depth):
        if layer < n_a:
            qt, k, vt, qit, ki, wit = _aproj(h, g_attn_a[layer].reshape(1, d),
                                             _prep_w_in_a(w_in_a[layer]), cos, sin, s, tm, tq)
            o = _dsa(qt, qit, wit, ki, k, vt, b, s, tq)
            w_o = w_o_a[layer]
        else:
            jb = layer - n_a
            if jb == 0:
                wk = jnp.pad(w_kv_b[:, :B_QK].reshape(d, B_HEADS, HEAD_DIM),
                             ((0, 0), (0, 0), (0, LANES - HEAD_DIM))).reshape(d, B_KAUG)
                wf = jnp.pad(w_kv_b[:, 2 * B_QK:], ((0, 0), (0, LANES - B_HEADS)))
                bias = jnp.pad(b_f, (0, LANES - B_HEADS)).reshape(1, LANES)
                place, ones = _bias_placement()
                qplace, qones = _query_bias_placement(tm)
                kaug, vt_b, bias_rows = _kvproj(
                    h.reshape(b, s, d), g_kv.reshape(1, d), wk.astype(BF16),
                    w_kv_b[:, B_QK:2 * B_QK].astype(BF16), wf.astype(BF16), bias,
                    place, ones, qplace, qones, tm, tq)
            qt = _qproj(h, g_attn_b[jb].reshape(1, d), w_q_b[jb].astype(BF16), bias_rows, tm)
            o = _fox(qt, kaug, vt_b, b, s, tq)
            w_o = w_o_b[jb]
        h = _mlp(h, o, w_o.astype(BF16), g_mlp[layer].reshape(1, d),
                 w_up[layer].astype(BF16), w_down[layer].astype(BF16), gf,
                 layer == depth - 1, tm, tf)
    return h.reshape(b, s, d)
```

```python
import functools

import numpy as np
import jax
import jax.numpy as jnp
from jax import lax
from jax.experimental import pallas as pl
from jax.experimental.pallas import tpu as pltpu

LANES = 128
SUBLANES = 8
BF16_ROWS = 16
HEAD_DIM = 64
ROPE_HALF = HEAD_DIM // 2
CHUNK = 64
A_HEADS = 16
A_KV_HEADS = 4
A_GROUP = A_HEADS // A_KV_HEADS
IDX_HEADS = 8
TOPK_MAX = 256
B_HEADS = 16
ROPE_THETA = 10000.0
EPS = 1e-6
INT_MIN = -(2 ** 31)
MASKED = -1e30
LOG2E = 1.4426950408889634
QK_SCALE = HEAD_DIM ** -0.5 * LOG2E
VMEM_LIMIT_BYTES = 56 * 1024 * 1024
ATTN_BLOCK = 256

F32 = jnp.float32
BF16 = jnp.bfloat16
I32 = jnp.int32


def _params(*sem):
    return pltpu.CompilerParams(dimension_semantics=sem,
                                vmem_limit_bytes=VMEM_LIMIT_BYTES)


def _resident(shape):
    nd = len(shape)
    return pl.BlockSpec(shape, lambda *_: (0,) * nd, pipeline_mode=pl.Buffered(1))


def _rms(x, g):
    return x * lax.rsqrt(jnp.mean(x * x, axis=-1, keepdims=True) + EPS) * g


def _colsum8(x):
    n, w = x.shape
    return x.reshape(n // SUBLANES, SUBLANES, w).sum(axis=0)


def _colmax8(x):
    n, w = x.shape
    return x.reshape(n // SUBLANES, SUBLANES, w).max(axis=0)


V_ROWS = HEAD_DIM + BF16_ROWS


def _store_value_head(vt_ref, sb, head, yt):
    r = head * V_ROWS
    tk = yt.shape[1]
    vt_ref[sb, r:r + HEAD_DIM, :] = yt
    first_row = lax.broadcasted_iota(I32, (BF16_ROWS, tk), 0) == 0
    vt_ref[sb, r + HEAD_DIM:r + V_ROWS, :] = jnp.where(first_row, 1.0, 0.0).astype(BF16)


def _softmax_pv(s_buf, mx_buf, vt, h, m_sc, acc_sc):
    m_old = m_sc[h]
    m_new = jnp.maximum(m_old, jnp.max(mx_buf[h], axis=0, keepdims=True))
    alpha = jnp.exp2(m_old - m_new)
    p = jnp.exp2(s_buf[h] - m_new)
    acc_sc[h] = alpha * acc_sc[h] + jnp.dot(vt, p.astype(BF16), preferred_element_type=F32)
    m_sc[h] = m_new


def _normalised(acc_sc, h):
    acc = acc_sc[h]
    return acc[0:HEAD_DIM, :] / acc[HEAD_DIM:HEAD_DIM + 1, :]


A_Q = A_HEADS * HEAD_DIM
A_KD = A_KV_HEADS * LANES
A_V = A_KV_HEADS * HEAD_DIM
A_VT = A_KV_HEADS * V_ROWS
A_QI = IDX_HEADS * HEAD_DIM
A_OFF_K = A_Q
A_OFF_V = A_OFF_K + A_KD
A_OFF_QI = A_OFF_V + A_V
A_OFF_KI = A_OFF_QI + A_QI
A_OFF_WI = A_OFF_KI + LANES
A_COLS = A_OFF_WI + LANES


def _aproj_kernel(h_ref, g_ref, w_ref, cos_ref, sin_ref,
                  qt_ref, k_ref, vt_ref, qit_ref, ki_ref, wit_ref, *, wi_scale):
    tm = h_ref.shape[0]
    tk = vt_ref.shape[2]
    xn = _rms(h_ref[...], g_ref[...]).astype(BF16)
    cos = cos_ref[...]
    sin = sin_ref[...]
    lane = lax.broadcasted_iota(I32, (tm, LANES), 1)
    first_half = (lane & ROPE_HALF) == 0

    def rope(y):
        ys = jnp.where(first_half, pltpu.roll(y, LANES - ROPE_HALF, 1),
                       pltpu.roll(y, ROPE_HALF, 1))
        return y * cos + ys * sin

    def chunks(c0, n):
        step = 4 * LANES
        for s0 in range(0, n, step):
            sn = min(step, n - s0)
            y = jnp.dot(xn, w_ref[:, c0 + s0:c0 + s0 + sn], preferred_element_type=F32)
            for j in range(sn // LANES):
                yield (s0 // LANES + j), y[:, j * LANES:(j + 1) * LANES]

    for j, y in chunks(0, A_Q):
        qt_ref[j * LANES:(j + 1) * LANES, :] = (rope(y) * QK_SCALE).T.astype(BF16)
    for j, y in chunks(A_OFF_K, A_KD):
        k_ref[:, j * LANES:(j + 1) * LANES] = rope(y).astype(BF16)
    for j, y in chunks(A_OFF_V, A_V):
        for sb in range(tm // tk):
            yt = y[sb * tk:(sb + 1) * tk].T.astype(BF16)
            for e in range(2):
                _store_value_head(vt_ref, sb, 2 * j + e, yt[e * HEAD_DIM:(e + 1) * HEAD_DIM, :])
    for j, y in chunks(A_OFF_QI, A_QI):
        qit_ref[j * LANES:(j + 1) * LANES, :] = rope(y).T.astype(BF16)
    for j, y in chunks(A_OFF_KI, LANES):
        ki_ref[...] = rope(y).astype(BF16)
    for j, y in chunks(A_OFF_WI, LANES):
        wit_ref[...] = (y * wi_scale).T[:IDX_HEADS, :]


def _aproj(h, g, w, cos, sin, seq, tm, tk):
    t, d = h.shape
    npos = seq // tm
    row = lambda n: pl.BlockSpec((tm, n), lambda i: (i, 0))
    col = lambda n: pl.BlockSpec((n, tm), lambda i: (0, i))
    tab = pl.BlockSpec((tm, LANES), lambda i: (i % npos, 0))
    wi_scale = IDX_HEADS ** -0.5 * HEAD_DIM ** -0.5
    return pl.pallas_call(
        functools.partial(_aproj_kernel, wi_scale=wi_scale),
        grid=(t // tm,),
        in_specs=[row(d), _resident((1, d)), _resident((d, A_COLS)), tab, tab],
        out_specs=[col(A_Q), row(A_KD),
                   pl.BlockSpec((tm // tk, A_VT, tk), lambda i: (i, 0, 0)),
                   col(A_QI), row(LANES), col(IDX_HEADS)],
        out_shape=[jax.ShapeDtypeStruct((A_Q, t), BF16),
                   jax.ShapeDtypeStruct((t, A_KD), BF16),
                   jax.ShapeDtypeStruct((t // tk, A_VT, tk), BF16),
                   jax.ShapeDtypeStruct((A_QI, t), BF16),
                   jax.ShapeDtypeStruct((t, LANES), BF16),
                   jax.ShapeDtypeStruct((IDX_HEADS, t), F32)],
        compiler_params=_params("parallel"),
        name="a_proj",
    )(h, g, w, cos, sin)


KEY_BITS = 32
BIT_TRANSPOSE_STEPS = (16, 8, 4, 2, 1)
BIT_TRANSPOSE_MASKS = (0x0000FFFF, 0x00FF00FF, 0x0F0F0F0F, 0x33333333, 0x55555555)


def _dsa_kernel(qt_ref, qit_ref, wit_ref, ki_ref, k_ref, vt_ref, o_ref,
                key_sc, planes_sc, qiw_sc, qw_sc, sa_sc, sb_sc, mxa_sc, mxb_sc, m_sc, acc_sc,
                *, tq, n_sel, idx_bits):
    tk = tq
    jj = pl.program_id(1)
    kiota = lax.broadcasted_iota(I32, (tk, tq), 0)
    zeros_half = jnp.zeros((HEAD_DIM, tq), BF16)

    for sub in range(2):
        cols = slice(sub * tq, (sub + 1) * tq)
        for h in range(IDX_HEADS):
            qiw_sc[sub * IDX_HEADS + h, 0:HEAD_DIM, :] = qit_ref[h * HEAD_DIM:(h + 1) * HEAD_DIM, cols]
            qiw_sc[sub * IDX_HEADS + h, HEAD_DIM:LANES, :] = zeros_half
        for h in range(A_HEADS):
            qw_sc[sub * A_HEADS + h, 0:HEAD_DIM, :] = qt_ref[h * HEAD_DIM:(h + 1) * HEAD_DIM, cols]
            qw_sc[sub * A_HEADS + h, HEAD_DIM:LANES, :] = zeros_half

    def select(sub):
        j = 2 * jj + sub
        nkb = j + 1
        keys = key_sc.at[sub]
        qpos = j * tq + lax.broadcasted_iota(I32, (1, tq), 1)
        limit = (qpos // CHUNK + 1) * CHUNK

        def head_logits(c):
            kc = ki_ref[pl.ds(pl.multiple_of(c * tk, tk), tk), :]
            return [jnp.dot(kc, qiw_sc[sub * IDX_HEADS + h], preferred_element_type=F32)
                    for h in range(IDX_HEADS)]

        def store_keys(c, logits):
            sc = jnp.zeros((tk, tq), F32)
            for h in range(IDX_HEADS):
                sc = sc + jnp.maximum(logits[h], 0.0) * wit_ref[h:h + 1, sub * tq:(sub + 1) * tq]
            sc = jnp.where(sc == 0.0, 0.0, sc)
            bits = pltpu.bitcast(sc, I32)
            key = bits ^ ((bits >> 31) & 0x7FFFFFFF)
            keys[c] = jnp.where(c * tk + kiota < limit, key, INT_MIN)

        def bit_slice(c):
            x = keys[c] ^ INT_MIN
            tiles = [x[v * SUBLANES:(v + 1) * SUBLANES, :] for v in range(KEY_BITS)]
            for jbit, mask in zip(BIT_TRANSPOSE_STEPS, BIT_TRANSPOSE_MASKS):
                for v in range(KEY_BITS):
                    if v & jbit == 0:
                        lo, hi = tiles[v], tiles[v + jbit]
                        t = (lax.shift_right_logical(lo, jbit) ^ hi) & mask
                        tiles[v] = lo ^ jnp.left_shift(t, jbit)
                        tiles[v + jbit] = hi ^ t
            for b in range(KEY_BITS):
                planes_sc[b, c] = tiles[b]

        store_keys(0, head_logits(0))

        def score_body(c, carry):
            logits = head_logits(c)
            bit_slice(c - 1)
            store_keys(c, logits)
            return carry

        lax.fori_loop(1, nkb, score_body, 0)
        bit_slice(nkb - 1)

        def clear_body(c, carry):
            planes_sc[:, c] = jnp.zeros((KEY_BITS, SUBLANES, tq), I32)
            return carry

        lax.fori_loop(nkb, planes_sc.shape[1], clear_body, 0)

        @pl.when(nkb % 2 == 1)
        def _():
            keys[nkb] = jnp.full((tk, tq), INT_MIN, I32)

        def count(pred):
            def body(cc, acc):
                for u in range(2):
                    c = 2 * cc + u
                    acc = acc + _colsum8(jnp.where(pred(c, keys[c]), 1.0, 0.0))
                return acc
            acc = lax.fori_loop(0, (nkb + 1) // 2, body, jnp.zeros((SUBLANES, tq), F32))
            return jnp.sum(acc, axis=0, keepdims=True)

        nwords = planes_sc.shape[1] * SUBLANES
        word_block = lax.broadcasted_iota(I32, (nwords, 1), 0) // SUBLANES
        cand = jnp.broadcast_to(jnp.where(word_block < nkb, -1, 0), (nwords, tq))

        def ones_count(words):
            cnt = lax.population_count(words).reshape(nwords // SUBLANES, SUBLANES, tq).sum(axis=0)
            return jnp.sum(cnt.astype(F32), axis=0, keepdims=True)

        thr_u = jnp.zeros((1, tq), I32)
        above = jnp.zeros((1, tq), F32)
        for b in range(KEY_BITS - 1, -1, -1):
            plane = planes_sc[b].reshape(nwords, tq)
            ones = cand & plane
            cnt1 = ones_count(ones)
            take = above + cnt1 >= n_sel
            thr_u = jnp.where(take, thr_u | (INT_MIN if b == KEY_BITS - 1 else 1 << b), thr_u)
            above = jnp.where(take, above, above + cnt1)
            cand = jnp.where(take, ones, cand ^ ones)
        cnt_thr = jnp.where(thr_u == 0, 0.0, above + ones_count(cand))
        thr = jnp.maximum(thr_u ^ INT_MIN, INT_MIN + 1)

        @pl.when(jnp.max(cnt_thr) > n_sel)
        def _():
            need = n_sel - count(lambda c, key: key > thr)
            pos = jnp.zeros((1, tq), I32)
            for b in range(idx_bits - 1, -1, -1):
                cand = pos | (1 << b)
                cnt = count(lambda c, key: jnp.where(key == thr, c * tk + kiota, 2 ** 30) < cand)
                pos = jnp.where(cnt < need, cand, pos)

            def demote(c, carry):
                key = keys[c]
                tie_pos = jnp.where(key == thr, c * tk + kiota, -1)
                keys[c] = jnp.where(tie_pos > pos, INT_MIN, key)
                return carry

            lax.fori_loop(0, nkb, demote, 0)

        return thr

    thrs = [select(0), select(1)]

    m_sc[...] = jnp.full(m_sc.shape, MASKED, F32)
    acc_sc[...] = jnp.zeros(acc_sc.shape, F32)
    buf_a = (sa_sc, mxa_sc)
    buf_b = (sb_sc, mxb_sc)

    def select_bias(c, sub):
        return jnp.where(key_sc[sub, c] >= thrs[sub], 0.0, MASKED)

    def scores(c, sub, h, buf, bias):
        n = h // A_GROUP
        r0 = pl.multiple_of(c * tk, tk)
        s = jnp.dot(k_ref[pl.ds(r0, tk), n * LANES:(n + 1) * LANES], qw_sc[sub * A_HEADS + h],
                    preferred_element_type=F32) + bias
        buf[0][sub * A_HEADS + h] = s
        buf[1][sub * A_HEADS + h] = _colmax8(s)

    def update(c, sub, h, buf):
        n = h // A_GROUP
        _softmax_pv(buf[0], buf[1], vt_ref[c, n * V_ROWS:(n + 1) * V_ROWS, :],
                    sub * A_HEADS + h, m_sc, acc_sc)

    def stage(c, cur, nxt, score_subs, update_subs=(0, 1)):
        biases = {sub: select_bias(c + 1, sub) for sub in score_subs}
        for h in range(A_HEADS):
            for sub in score_subs:
                scores(c + 1, sub, h, nxt, biases[sub])
            for sub in update_subs:
                update(c, sub, h, cur)

    for sub in range(2):
        bias0 = select_bias(0, sub)
        for h in range(A_HEADS):
            scores(0, sub, h, buf_a, bias0)

    def attn_body(i, carry):
        stage(2 * i, buf_a, buf_b, (0, 1))
        stage(2 * i + 1, buf_b, buf_a, (0, 1))
        return carry

    lax.fori_loop(0, jj, attn_body, 0)
    last = 2 * jj
    stage(last, buf_a, buf_b, (1,))
    stage(last + 1, buf_b, None, (), update_subs=(1,))

    for sub in range(2):
        for h in range(A_HEADS):
            o_ref[h * HEAD_DIM:(h + 1) * HEAD_DIM, sub * tq:(sub + 1) * tq] = (
                _normalised(acc_sc, sub * A_HEADS + h).astype(o_ref.dtype))


def _dsa(qt, qit, wit, ki, k, vt, b, s, tq):
    t = b * s
    nq = s // tq
    assert nq % 2 == 0, "dsa_attention walks query blocks in pairs"
    n_sel = min(TOPK_MAX, s // 4)
    idx_bits = max(1, (s - 1).bit_length())
    assert tq == KEY_BITS * SUBLANES, "the bit-sliced search transposes 32 sublane tiles per key block"
    qcol = lambda n: pl.BlockSpec((n, 2 * tq), lambda bi, j: (0, bi * (nq // 2) + j))
    full = lambda n: pl.BlockSpec((s, n), lambda bi, j: (bi, 0))
    return pl.pallas_call(
        functools.partial(_dsa_kernel, tq=tq, n_sel=float(n_sel), idx_bits=idx_bits),
        grid=(b, nq // 2),
        in_specs=[qcol(A_Q), qcol(A_QI), qcol(IDX_HEADS), full(LANES), full(A_KD),
                  pl.BlockSpec((nq, A_VT, tq), lambda bi, j: (bi, 0, 0))],
        out_specs=pl.BlockSpec((A_Q, 2 * tq), lambda bi, j: (0, bi * (nq // 2) + j)),
        out_shape=jax.ShapeDtypeStruct((A_Q, t), BF16),
        scratch_shapes=[
            pltpu.VMEM((2, nq, tq, tq), I32),
            pltpu.VMEM((KEY_BITS, nq, SUBLANES, tq), I32),
            pltpu.VMEM((2 * IDX_HEADS, LANES, tq), BF16),
            pltpu.VMEM((2 * A_HEADS, LANES, tq), BF16),
            pltpu.VMEM((2 * A_HEADS, tq, tq), F32),
            pltpu.VMEM((2 * A_HEADS, tq, tq), F32),
            pltpu.VMEM((2 * A_HEADS, SUBLANES, tq), F32),
            pltpu.VMEM((2 * A_HEADS, SUBLANES, tq), F32),
            pltpu.VMEM((2 * A_HEADS, 1, tq), F32),
            pltpu.VMEM((2 * A_HEADS, V_ROWS, tq), F32),
        ],
        compiler_params=_params("parallel", "arbitrary"),
        name="dsa_attention",
    )(qt, qit, wit, ki, k, vt)


B_QK = B_HEADS * HEAD_DIM
K_ROWS = HEAD_DIM + BF16_ROWS
B_KAUG = B_HEADS * K_ROWS
B_VT = B_HEADS * V_ROWS
BIAS_PIECES = 3


def _split3(x):
    hi = x.astype(BF16)
    r1 = x - hi.astype(F32)
    mid = r1.astype(BF16)
    low = (r1 - mid.astype(F32)).astype(BF16)
    return hi, mid, low


def _kvproj_kernel(h_ref, g_ref, wk_ref, wv_ref, wf_ref, bf_ref, place_ref, ones_ref,
                   qplace_ref, qones_ref, kaug_ref, vt_ref, qrows_ref, carry_sc):
    tm = h_ref.shape[0]
    tk = vt_ref.shape[2]

    @pl.when(pl.program_id(1) == 0)
    def _():
        carry_sc[...] = jnp.zeros(carry_sc.shape, F32)

    xn = _rms(h_ref[...], g_ref[...]).astype(BF16)
    z = jnp.dot(xn, wf_ref[...], preferred_element_type=F32) + bf_ref[...]
    logf = jnp.minimum(z, 0.0) - jnp.log1p(jnp.exp(-jnp.abs(z)))
    tri = (lax.broadcasted_iota(I32, (tm, tm), 0) >=
           lax.broadcasted_iota(I32, (tm, tm), 1)).astype(BF16)
    cs = sum(jnp.dot(tri, piece, preferred_element_type=F32) for piece in _split3(logf))
    cum = cs + carry_sc[...]
    carry_sc[...] = cum[tm - 1:tm, :]
    hi, mid, low = _split3(cum * LOG2E)
    pieces = jnp.concatenate([hi, mid, low], axis=1)
    qrows = lax.dot_general(qplace_ref[...], pieces, (((1,), (1,)), ((), ())),
                            preferred_element_type=F32) + qones_ref[...]
    qrows_ref[...] = qrows.astype(BF16)

    kstep = 2 * LANES
    for c0 in range(0, B_KAUG, kstep):
        y = (jnp.dot(xn, wk_ref[:, c0:c0 + kstep], preferred_element_type=F32) +
             jnp.dot(pieces, place_ref[:, c0:c0 + kstep], preferred_element_type=F32) +
             ones_ref[:, c0:c0 + kstep])
        for jj in range(kstep // LANES):
            r = c0 + jj * LANES
            for sb in range(tm // tk):
                kaug_ref[sb, r:r + LANES, :] = (
                    y[sb * tk:(sb + 1) * tk, jj * LANES:(jj + 1) * LANES].T.astype(BF16))
    step = 4 * LANES
    for c0 in range(0, B_QK, step):
        y = jnp.dot(xn, wv_ref[:, c0:c0 + step], preferred_element_type=F32)
        for jj in range(step // LANES):
            head = (c0 + jj * LANES) // HEAD_DIM
            for sb in range(tm // tk):
                yt = y[sb * tk:(sb + 1) * tk, jj * LANES:(jj + 1) * LANES].T.astype(BF16)
                for e in range(2):
                    _store_value_head(vt_ref, sb, head + e, yt[e * HEAD_DIM:(e + 1) * HEAD_DIM, :])


def _kvproj(h3, g, wk, wv, wf, bf, place, ones, qplace, qones, tm, tk):
    b, s, d = h3.shape
    ns = s // tm
    blk = lambda n: pl.BlockSpec((None, tm, n), lambda bi, i: (bi, i, 0))
    return pl.pallas_call(
        _kvproj_kernel,
        grid=(b, ns),
        in_specs=[blk(d), _resident((1, d)), _resident(wk.shape), _resident(wv.shape),
                  _resident(wf.shape), _resident((1, LANES)), _resident(place.shape),
                  _resident(ones.shape), _resident(qplace.shape), _resident(qones.shape)],
        out_specs=[pl.BlockSpec((tm // tk, B_KAUG, tk), lambda bi, i: (bi * ns + i, 0, 0)),
                   pl.BlockSpec((tm // tk, B_VT, tk), lambda bi, i: (bi * ns + i, 0, 0)),
                   pl.BlockSpec((B_HEADS * BF16_ROWS, tm), lambda bi, i: (0, bi * ns + i))],
        out_shape=[jax.ShapeDtypeStruct((b * s // tk, B_KAUG, tk), BF16),
                   jax.ShapeDtypeStruct((b * s // tk, B_VT, tk), BF16),
                   jax.ShapeDtypeStruct((B_HEADS * BF16_ROWS, b * s), BF16)],
        scratch_shapes=[pltpu.VMEM((1, LANES), F32)],
        compiler_params=_params("parallel", "arbitrary"),
        name="kv_proj_cumsum",
    )(h3, g, wk, wv, wf, bf, place, ones, qplace, qones)


def _qproj_kernel(h_ref, g_ref, w_ref, rows_ref, qt_ref):
    tm = h_ref.shape[0]
    xn = _rms(h_ref[...], g_ref[...]).astype(BF16)
    step = 4 * LANES
    for c0 in range(0, B_QK, step):
        y = jnp.dot(xn, w_ref[:, c0:c0 + step], preferred_element_type=F32) * QK_SCALE
        for jj in range(step // LANES):
            yt = y[:, jj * LANES:(jj + 1) * LANES].T.astype(BF16)
            for e in range(2):
                h = (c0 + jj * LANES) // HEAD_DIM + e
                r = h * K_ROWS
                qt_ref[r:r + HEAD_DIM, :] = yt[e * HEAD_DIM:(e + 1) * HEAD_DIM, :]
                qt_ref[r + HEAD_DIM:r + K_ROWS, :] = rows_ref[h * BF16_ROWS:(h + 1) * BF16_ROWS, :]


def _qproj(h, g, w, rows, tm):
    t, d = h.shape
    return pl.pallas_call(
        _qproj_kernel,
        grid=(t // tm,),
        in_specs=[pl.BlockSpec((tm, d), lambda i: (i, 0)), _resident((1, d)), _resident(w.shape),
                  pl.BlockSpec((B_HEADS * BF16_ROWS, tm), lambda i: (0, i))],
        out_specs=pl.BlockSpec((B_KAUG, tm), lambda i: (0, i)),
        out_shape=jax.ShapeDtypeStruct((B_KAUG, t), BF16),
        compiler_params=_params("parallel"),
        name="q_proj",
    )(h, g, w, rows)


FOX_HEADS_PER_STEP = 8


def _fox_kernel(qt_ref, k_ref, vt_ref, o_ref, qw_sc, sa_sc, sb_sc, mxa_sc, mxb_sc, m_sc, acc_sc,
                *, tq):
    tk = tq
    hps = FOX_HEADS_PER_STEP
    jj = pl.program_id(2)
    m_sc[...] = jnp.full(m_sc.shape, MASKED, F32)
    acc_sc[...] = jnp.zeros(acc_sc.shape, F32)
    kiota = lax.broadcasted_iota(I32, (tk, tq), 0)
    qiota = lax.broadcasted_iota(I32, (1, tq), 1)
    buf_a = (sa_sc, mxa_sc)
    buf_b = (sb_sc, mxb_sc)
    for sub in range(2):
        for h in range(hps):
            qw_sc[sub * hps + h] = qt_ref[h * K_ROWS:(h + 1) * K_ROWS, sub * tq:(sub + 1) * tq]

    def scores(c, sub, h, buf, qlimit):
        s = lax.dot_general(k_ref[c, h * K_ROWS:(h + 1) * K_ROWS, :], qw_sc[sub * hps + h],
                            (((0,), (0,)), ((), ())), preferred_element_type=F32)
        if qlimit is not None:
            s = jnp.where(kiota <= qlimit, s, MASKED)
        buf[0][sub * hps + h] = s
        buf[1][sub * hps + h] = _colmax8(s)

    def update(c, sub, h, buf):
        _softmax_pv(buf[0], buf[1], vt_ref[c, h * V_ROWS:(h + 1) * V_ROWS, :],
                    sub * hps + h, m_sc, acc_sc)

    def stage(c, cur, nxt, limits, update_subs=(0, 1)):
        for h in range(hps):
            for sub, qlimit in limits:
                scores(c + 1, sub, h, nxt, qlimit)
            for sub in update_subs:
                update(c, sub, h, cur)

    first_limit = qiota + jnp.where(jj == 0, 0, tk)
    for h in range(hps):
        scores(0, 0, h, buf_a, first_limit)
        scores(0, 1, h, buf_a, None)

    def main_body(i, carry):
        c = 2 * i
        stage(c, buf_a, buf_b, ((0, None), (1, None)))
        limit0 = qiota + jnp.where(i == jj - 1, 0, tk)
        stage(c + 1, buf_b, buf_a, ((0, limit0), (1, None)))
        return carry

    lax.fori_loop(0, jj, main_body, 0)
    last = 2 * jj
    stage(last, buf_a, buf_b, ((1, qiota),))
    stage(last + 1, buf_b, None, (), update_subs=(1,))
    for sub in range(2):
        for h in range(hps):
            o_ref[h * HEAD_DIM:(h + 1) * HEAD_DIM, sub * tq:(sub + 1) * tq] = (
                _normalised(acc_sc, sub * hps + h).astype(o_ref.dtype))


def _fox(qt, kaug, vt, b, s, tq):
    t = b * s
    nq = s // tq
    assert nq % 2 == 0, "fox_attention walks query blocks in pairs"
    hps = FOX_HEADS_PER_STEP
    nstate = 2 * hps
    return pl.pallas_call(
        functools.partial(_fox_kernel, tq=tq),
        grid=(b, B_HEADS // hps, nq // 2),
        in_specs=[
            pl.BlockSpec((hps * K_ROWS, 2 * tq), lambda bi, p, j: (p, bi * (nq // 2) + j)),
            pl.BlockSpec((nq, hps * K_ROWS, tq), lambda bi, p, j: (bi, p, 0)),
            pl.BlockSpec((nq, hps * V_ROWS, tq), lambda bi, p, j: (bi, p, 0)),
        ],
        out_specs=pl.BlockSpec((hps * HEAD_DIM, 2 * tq), lambda bi, p, j: (p, bi * (nq // 2) + j)),
        out_shape=jax.ShapeDtypeStruct((B_QK, t), BF16),
        scratch_shapes=[
            pltpu.VMEM((nstate, K_ROWS, tq), BF16),
            pltpu.VMEM((nstate, tq, tq), F32),
            pltpu.VMEM((nstate, tq, tq), F32),
            pltpu.VMEM((nstate, SUBLANES, tq), F32),
            pltpu.VMEM((nstate, SUBLANES, tq), F32),
            pltpu.VMEM((nstate, 1, tq), F32),
            pltpu.VMEM((nstate, V_ROWS, tq), F32),
        ],
        compiler_params=_params("parallel", "parallel", "arbitrary"),
        name="fox_attention",
    )(qt, kaug, vt)


def _mlp_kernel(h_ref, o_ref, wo_ref, g_ref, wup_ref, wdn_ref, gf_ref, out_ref, *, tf, final):
    h1 = h_ref[...] + lax.dot_general(o_ref[...], wo_ref[...], (((0,), (0,)), ((), ())),
                                      preferred_element_type=F32)
    xn = _rms(h1, g_ref[...]).astype(BF16)
    acc = h1
    for c0 in range(0, wup_ref.shape[1], tf):
        u = jnp.dot(xn, wup_ref[:, c0:c0 + tf], preferred_element_type=F32)
        a = jnp.square(jnp.maximum(u, 0.0)).astype(BF16)
        acc = acc + jnp.dot(a, wdn_ref[c0:c0 + tf, :], preferred_element_type=F32)
    if final:
        acc = _rms(acc, gf_ref[...])
    out_ref[...] = acc


def _mlp(h, o, wo, g, wup, wdn, gf, final, tm, tf):
    t, d = h.shape
    row = pl.BlockSpec((tm, d), lambda i: (i, 0))
    return pl.pallas_call(
        functools.partial(_mlp_kernel, tf=tf, final=final),
        grid=(t // tm,),
        in_specs=[row, pl.BlockSpec((o.shape[0], tm), lambda i: (0, i)), _resident(wo.shape),
                  _resident((1, d)), _resident(wup.shape), _resident(wdn.shape), _resident((1, d))],
        out_specs=row,
        out_shape=jax.ShapeDtypeStruct((t, d), F32),
        compiler_params=_params("parallel"),
        name="oproj_mlp",
    )(h, o, wo, g, wup, wdn, gf)


def _dup_heads(w, heads):
    d = w.shape[0]
    w = w.reshape(d, heads, HEAD_DIM)
    return jnp.concatenate([w, w], axis=-1).reshape(d, heads * LANES)


def _prep_w_in_a(w):
    kv = A_KV_HEADS * HEAD_DIM
    o_k, o_v, o_qi = A_Q, A_Q + kv, A_Q + 2 * kv
    o_ki, o_wi = o_qi + A_QI, o_qi + A_QI + HEAD_DIM
    wi = w[:, o_wi:o_wi + IDX_HEADS]
    return jnp.concatenate([
        w[:, :A_Q],
        _dup_heads(w[:, o_k:o_v], A_KV_HEADS),
        w[:, o_v:o_qi],
        w[:, o_qi:o_ki],
        _dup_heads(w[:, o_ki:o_wi], 1),
        jnp.pad(wi, ((0, 0), (0, LANES - IDX_HEADS))),
    ], axis=1).astype(BF16)


def _rope_tables(seq):
    inv = 1.0 / (ROPE_THETA ** (jnp.arange(0, HEAD_DIM, 2, dtype=F32) / HEAD_DIM))
    ang = jnp.arange(seq, dtype=F32)[:, None] * inv[None, :]
    c, s = jnp.cos(ang), jnp.sin(ang)
    return jnp.concatenate([c, c, c, c], axis=1), jnp.concatenate([-s, s, -s, s], axis=1)


def _bias_placement():
    place = np.zeros((BIAS_PIECES * LANES, B_KAUG), np.float32)
    ones = np.zeros((1, B_KAUG), np.float32)
    for h in range(B_HEADS):
        for p in range(BIAS_PIECES):
            place[p * LANES + h, h * K_ROWS + HEAD_DIM + p] = -1.0
            ones[0, h * K_ROWS + HEAD_DIM + BIAS_PIECES + p] = 1.0
    return jnp.asarray(place, BF16), jnp.asarray(ones, F32)


def _query_bias_placement(tm):
    qplace = np.zeros((B_HEADS * BF16_ROWS, BIAS_PIECES * LANES), np.float32)
    qones = np.zeros((B_HEADS * BF16_ROWS, tm), np.float32)
    for h in range(B_HEADS):
        for p in range(BIAS_PIECES):
            qplace[h * BF16_ROWS + BIAS_PIECES + p, p * LANES + h] = 1.0
            qones[h * BF16_ROWS + p, :] = 1.0
    return jnp.asarray(qplace, BF16), jnp.asarray(qones, F32)


def kernel(x, g_attn_a, w_in_a, w_o_a, g_kv, w_kv_b, b_f, g_attn_b, w_q_b, w_o_b,
           g_mlp, w_up, w_down, g_final):
    b, s, d = x.shape
    t = b * s
    depth = g_mlp.shape[0]
    n_a = g_attn_a.shape[0]
    tm = min(512, s)
    tq = min(ATTN_BLOCK, s)
    tf = 512

    cos, sin = _rope_tables(s)
    gf = g_final.reshape(1, d)
    h = x.reshape(t, d)
    kaug = vt_b = bias_rows = None

    for layer in range(depth):
        if layer < n_a:
            qt, k, vt, qit, ki, wit = _aproj(h, g_attn_a[layer].reshape(1, d),
                                             _prep_w_in_a(w_in_a[layer]), cos, sin, s, tm, tq)
            o = _dsa(qt, qit, wit, ki, k, vt, b, s, tq)
            w_o = w_o_a[layer]
        else:
            jb = layer - n_a
            if jb == 0:
                wk = jnp.pad(w_kv_b[:, :B_QK].reshape(d, B_HEADS, HEAD_DIM),
                             ((0, 0), (0, 0), (0, K_ROWS - HEAD_DIM))).reshape(d, B_KAUG)
                wf = jnp.pad(w_kv_b[:, 2 * B_QK:], ((0, 0), (0, LANES - B_HEADS)))
                bias = jnp.pad(b_f, (0, LANES - B_HEADS)).reshape(1, LANES)
                place, ones = _bias_placement()
                qplace, qones = _query_bias_placement(tm)
                kaug, vt_b, bias_rows = _kvproj(
                    h.reshape(b, s, d), g_kv.reshape(1, d), wk.astype(BF16),
                    w_kv_b[:, B_QK:2 * B_QK].astype(BF16), wf.astype(BF16), bias,
                    place, ones, qplace, qones, tm, tq)
            qt = _qproj(h, g_attn_b[jb].reshape(1, d), w_q_b[jb].astype(BF16), bias_rows, tm)
            o = _fox(qt, kaug, vt_b, b, s, tq)
            w_o = w_o_b[jb]
        h = _mlp(h, o, w_o.astype(BF16), g_mlp[layer].reshape(1, d),
                 w_up[layer].astype(BF16), w_down[layer].astype(BF16), gf,
                 layer == depth - 1, tm, tf)
    return h.reshape(b, s, d)
```

```python
import functools

import numpy as np
import jax
import jax.numpy as jnp
from jax import lax
from jax.experimental import pallas as pl
from jax.experimental.pallas import tpu as pltpu

LANES = 128
SUBLANES = 8
BF16_ROWS = 16
HEAD_DIM = 64
ROPE_HALF = HEAD_DIM // 2
CHUNK = 64
A_HEADS = 16
A_KV_HEADS = 4
A_GROUP = A_HEADS // A_KV_HEADS
IDX_HEADS = 8
TOPK_MAX = 256
B_HEADS = 16
ROPE_THETA = 10000.0
EPS = 1e-6
INT_MIN = -(2 ** 31)
MASKED = -1e30
LOG2E = 1.4426950408889634
QK_SCALE = HEAD_DIM ** -0.5 * LOG2E
VMEM_LIMIT_BYTES = 56 * 1024 * 1024
ATTN_BLOCK = 256

F32 = jnp.float32
BF16 = jnp.bfloat16
I32 = jnp.int32

_DIM0 = (((0,), (0,)), ((), ()))


def _params(*sem):
    return pltpu.CompilerParams(dimension_semantics=sem,
                                vmem_limit_bytes=VMEM_LIMIT_BYTES)


def _resident(shape):
    nd = len(shape)
    return pl.BlockSpec(shape, lambda *_: (0,) * nd, pipeline_mode=pl.Buffered(1))


def _rms(x, g):
    return x * lax.rsqrt(jnp.mean(x * x, axis=-1, keepdims=True) + EPS) * g


def _colsum8(x):
    n, w = x.shape
    return x.reshape(n // SUBLANES, SUBLANES, w).sum(axis=0)


def _colmax8(x):
    n, w = x.shape
    return x.reshape(n // SUBLANES, SUBLANES, w).max(axis=0)


V_ROWS = HEAD_DIM + BF16_ROWS


def _store_value_head(vt_ref, sb, head, yt):
    r = head * V_ROWS
    tk = yt.shape[1]
    vt_ref[sb, r:r + HEAD_DIM, :] = yt
    first_row = lax.broadcasted_iota(I32, (BF16_ROWS, tk), 0) == 0
    vt_ref[sb, r + HEAD_DIM:r + V_ROWS, :] = jnp.where(first_row, 1.0, 0.0).astype(BF16)


def _softmax_pv(s_buf, mx_buf, vt, h, m_sc, acc_sc):
    m_old = m_sc[h]
    m_new = jnp.maximum(m_old, jnp.max(mx_buf[h], axis=0, keepdims=True))
    alpha = jnp.exp2(m_old - m_new)
    p = jnp.exp2(s_buf[h] - m_new)
    acc_sc[h] = alpha * acc_sc[h] + jnp.dot(vt, p.astype(BF16), preferred_element_type=F32)
    m_sc[h] = m_new


def _normalised(acc_sc, h):
    acc = acc_sc[h]
    return acc[0:HEAD_DIM, :] / acc[HEAD_DIM:HEAD_DIM + 1, :]


A_Q = A_HEADS * HEAD_DIM
A_KD = A_KV_HEADS * HEAD_DIM
A_V = A_KV_HEADS * HEAD_DIM
A_VT = A_KV_HEADS * V_ROWS
A_QI = IDX_HEADS * HEAD_DIM
A_OFF_K = A_Q
A_OFF_V = A_OFF_K + A_KD
A_OFF_QI = A_OFF_V + A_V
A_OFF_KI = A_OFF_QI + A_QI
A_OFF_WI = A_OFF_KI + LANES
A_COLS = A_OFF_WI + LANES


def _aproj_kernel(h_ref, g_ref, w_ref, cos_ref, sin_ref,
                  qt_ref, k_ref, vt_ref, qit_ref, ki_ref, wit_ref, *, wi_scale):
    tm = h_ref.shape[0]
    tk = vt_ref.shape[2]
    xn = _rms(h_ref[...], g_ref[...]).astype(BF16)
    cos = cos_ref[...]
    sin = sin_ref[...]
    lane = lax.broadcasted_iota(I32, (tm, LANES), 1)
    first_half = (lane & ROPE_HALF) == 0

    def rope(y):
        ys = jnp.where(first_half, pltpu.roll(y, LANES - ROPE_HALF, 1),
                       pltpu.roll(y, ROPE_HALF, 1))
        return y * cos + ys * sin

    def chunks(c0, n):
        step = 4 * LANES
        for s0 in range(0, n, step):
            sn = min(step, n - s0)
            y = jnp.dot(xn, w_ref[:, c0 + s0:c0 + s0 + sn], preferred_element_type=F32)
            for j in range(sn // LANES):
                yield (s0 // LANES + j), y[:, j * LANES:(j + 1) * LANES]

    for j, y in chunks(0, A_Q):
        qt_ref[j * LANES:(j + 1) * LANES, :] = (rope(y) * QK_SCALE).T.astype(BF16)
    for j, y in chunks(A_OFF_K, A_KD):
        r = rope(y)
        for sb in range(tm // tk):
            k_ref[sb, j * LANES:(j + 1) * LANES, :] = r[sb * tk:(sb + 1) * tk].T.astype(BF16)
    for j, y in chunks(A_OFF_V, A_V):
        for sb in range(tm // tk):
            yt = y[sb * tk:(sb + 1) * tk].T.astype(BF16)
            for e in range(2):
                _store_value_head(vt_ref, sb, 2 * j + e, yt[e * HEAD_DIM:(e + 1) * HEAD_DIM, :])
    for j, y in chunks(A_OFF_QI, A_QI):
        qit_ref[j * LANES:(j + 1) * LANES, :] = rope(y).T.astype(BF16)
    for j, y in chunks(A_OFF_KI, LANES):
        ki_ref[...] = rope(y).astype(BF16)
    for j, y in chunks(A_OFF_WI, LANES):
        wit_ref[...] = (y * wi_scale).T[:IDX_HEADS, :]


def _aproj(h, g, w, cos, sin, seq, tm, tk):
    t, d = h.shape
    npos = seq // tm
    row = lambda n: pl.BlockSpec((tm, n), lambda i: (i, 0))
    col = lambda n: pl.BlockSpec((n, tm), lambda i: (0, i))
    blocked = lambda n: pl.BlockSpec((tm // tk, n, tk), lambda i: (i, 0, 0))
    tab = pl.BlockSpec((tm, LANES), lambda i: (i % npos, 0))
    wi_scale = IDX_HEADS ** -0.5 * HEAD_DIM ** -0.5
    return pl.pallas_call(
        functools.partial(_aproj_kernel, wi_scale=wi_scale),
        grid=(t // tm,),
        in_specs=[row(d), _resident((1, d)), _resident((d, A_COLS)), tab, tab],
        out_specs=[col(A_Q), blocked(A_KD), blocked(A_VT), col(A_QI), row(LANES),
                   col(IDX_HEADS)],
        out_shape=[jax.ShapeDtypeStruct((A_Q, t), BF16),
                   jax.ShapeDtypeStruct((t // tk, A_KD, tk), BF16),
                   jax.ShapeDtypeStruct((t // tk, A_VT, tk), BF16),
                   jax.ShapeDtypeStruct((A_QI, t), BF16),
                   jax.ShapeDtypeStruct((t, LANES), BF16),
                   jax.ShapeDtypeStruct((IDX_HEADS, t), F32)],
        compiler_params=_params("parallel"),
        name="a_proj",
    )(h, g, w, cos, sin)


KEY_BITS = 32
BIT_TRANSPOSE_STEPS = (16, 8, 4, 2, 1)
BIT_TRANSPOSE_MASKS = (0x0000FFFF, 0x00FF00FF, 0x0F0F0F0F, 0x33333333, 0x55555555)


def _dsa_kernel(qt_ref, qit_ref, wit_ref, ki_ref, k_ref, vt_ref, o_ref,
                key_sc, planes_sc, qiw_sc, qw_sc, sa_sc, sb_sc, mxa_sc, mxb_sc, m_sc, acc_sc,
                *, tq, n_sel, idx_bits):
    tk = tq
    jj = pl.program_id(1)
    kiota = lax.broadcasted_iota(I32, (tk, tq), 0)

    zeros_half = jnp.zeros((HEAD_DIM, tq), BF16)
    for sub in range(2):
        cols = slice(sub * tq, (sub + 1) * tq)
        for h in range(IDX_HEADS):
            qiw_sc[sub * IDX_HEADS + h, 0:HEAD_DIM, :] = qit_ref[h * HEAD_DIM:(h + 1) * HEAD_DIM, cols]
            qiw_sc[sub * IDX_HEADS + h, HEAD_DIM:LANES, :] = zeros_half
        for h in range(A_HEADS):
            qw_sc[sub * A_HEADS + h] = qt_ref[h * HEAD_DIM:(h + 1) * HEAD_DIM, cols]

    def select(sub):
        j = 2 * jj + sub
        nkb = j + 1
        keys = key_sc.at[sub]
        qpos = j * tq + lax.broadcasted_iota(I32, (1, tq), 1)
        limit = (qpos // CHUNK + 1) * CHUNK

        def head_logits(c):
            kc = ki_ref[pl.ds(pl.multiple_of(c * tk, tk), tk), :]
            return [jnp.dot(kc, qiw_sc[sub * IDX_HEADS + h], preferred_element_type=F32)
                    for h in range(IDX_HEADS)]

        def store_keys(c, logits):
            sc = jnp.zeros((tk, tq), F32)
            for h in range(IDX_HEADS):
                sc = sc + jnp.maximum(logits[h], 0.0) * wit_ref[h:h + 1, sub * tq:(sub + 1) * tq]
            sc = jnp.where(sc == 0.0, 0.0, sc)
            bits = pltpu.bitcast(sc, I32)
            key = bits ^ ((bits >> 31) & 0x7FFFFFFF)
            keys[c] = jnp.where(c * tk + kiota < limit, key, INT_MIN)

        def bit_slice(c):
            x = keys[c] ^ INT_MIN
            tiles = [x[v * SUBLANES:(v + 1) * SUBLANES, :] for v in range(KEY_BITS)]
            for jbit, mask in zip(BIT_TRANSPOSE_STEPS, BIT_TRANSPOSE_MASKS):
                for v in range(KEY_BITS):
                    if v & jbit == 0:
                        lo, hi = tiles[v], tiles[v + jbit]
                        t = (lax.shift_right_logical(lo, jbit) ^ hi) & mask
                        tiles[v] = lo ^ jnp.left_shift(t, jbit)
                        tiles[v + jbit] = hi ^ t
            for b in range(KEY_BITS):
                planes_sc[b, c] = tiles[b]

        store_keys(0, head_logits(0))

        def score_body(c, carry):
            logits = head_logits(c)
            bit_slice(c - 1)
            store_keys(c, logits)
            return carry

        lax.fori_loop(1, nkb, score_body, 0)
        bit_slice(nkb - 1)

        def clear_body(c, carry):
            planes_sc[:, c] = jnp.zeros((KEY_BITS, SUBLANES, tq), I32)
            return carry

        lax.fori_loop(nkb, planes_sc.shape[1], clear_body, 0)

        @pl.when(nkb % 2 == 1)
        def _():
            keys[nkb] = jnp.full((tk, tq), INT_MIN, I32)

        def count(pred):
            def body(cc, acc):
                for u in range(2):
                    c = 2 * cc + u
                    acc = acc + _colsum8(jnp.where(pred(c, keys[c]), 1.0, 0.0))
                return acc
            acc = lax.fori_loop(0, (nkb + 1) // 2, body, jnp.zeros((SUBLANES, tq), F32))
            return jnp.sum(acc, axis=0, keepdims=True)

        nwords = planes_sc.shape[1] * SUBLANES
        word_block = lax.broadcasted_iota(I32, (nwords, 1), 0) // SUBLANES
        cand = jnp.broadcast_to(jnp.where(word_block < nkb, -1, 0), (nwords, tq))

        def ones_count(words):
            cnt = lax.population_count(words).reshape(nwords // SUBLANES, SUBLANES, tq).sum(axis=0)
            return jnp.sum(cnt.astype(F32), axis=0, keepdims=True)

        thr_u = jnp.zeros((1, tq), I32)
        above = jnp.zeros((1, tq), F32)
        for b in range(KEY_BITS - 1, -1, -1):
            plane = planes_sc[b].reshape(nwords, tq)
            ones = cand & plane
            cnt1 = ones_count(ones)
            take = above + cnt1 >= n_sel
            thr_u = jnp.where(take, thr_u | (INT_MIN if b == KEY_BITS - 1 else 1 << b), thr_u)
            above = jnp.where(take, above, above + cnt1)
            cand = jnp.where(take, ones, cand ^ ones)
        cnt_thr = jnp.where(thr_u == 0, 0.0, above + ones_count(cand))
        thr = jnp.maximum(thr_u ^ INT_MIN, INT_MIN + 1)

        @pl.when(jnp.max(cnt_thr) > n_sel)
        def _():
            need = n_sel - count(lambda c, key: key > thr)
            pos = jnp.zeros((1, tq), I32)
            for b in range(idx_bits - 1, -1, -1):
                cand = pos | (1 << b)
                cnt = count(lambda c, key: jnp.where(key == thr, c * tk + kiota, 2 ** 30) < cand)
                pos = jnp.where(cnt < need, cand, pos)

            def demote(c, carry):
                key = keys[c]
                tie_pos = jnp.where(key == thr, c * tk + kiota, -1)
                keys[c] = jnp.where(tie_pos > pos, INT_MIN, key)
                return carry

            lax.fori_loop(0, nkb, demote, 0)

        return thr

    thrs = [select(0), select(1)]

    m_sc[...] = jnp.full(m_sc.shape, MASKED, F32)
    acc_sc[...] = jnp.zeros(acc_sc.shape, F32)
    buf_a = (sa_sc, mxa_sc)
    buf_b = (sb_sc, mxb_sc)

    def select_bias(c, sub):
        return jnp.where(key_sc[sub, c] >= thrs[sub], 0.0, MASKED)

    def scores(c, sub, h, buf, bias):
        n = h // A_GROUP
        s = lax.dot_general(k_ref[c, n * HEAD_DIM:(n + 1) * HEAD_DIM, :], qw_sc[sub * A_HEADS + h],
                            _DIM0, preferred_element_type=F32) + bias
        buf[0][sub * A_HEADS + h] = s
        buf[1][sub * A_HEADS + h] = _colmax8(s)

    def update(c, sub, h, buf):
        n = h // A_GROUP
        _softmax_pv(buf[0], buf[1], vt_ref[c, n * V_ROWS:(n + 1) * V_ROWS, :],
                    sub * A_HEADS + h, m_sc, acc_sc)

    def stage(c, cur, nxt, score_subs, update_subs=(0, 1)):
        biases = {sub: select_bias(c + 1, sub) for sub in score_subs}
        for h in range(A_HEADS):
            for sub in score_subs:
                scores(c + 1, sub, h, nxt, biases[sub])
            for sub in update_subs:
                update(c, sub, h, cur)

    for sub in range(2):
        bias0 = select_bias(0, sub)
        for h in range(A_HEADS):
            scores(0, sub, h, buf_a, bias0)

    def attn_body(i, carry):
        stage(2 * i, buf_a, buf_b, (0, 1))
        stage(2 * i + 1, buf_b, buf_a, (0, 1))
        return carry

    lax.fori_loop(0, jj, attn_body, 0)
    last = 2 * jj
    stage(last, buf_a, buf_b, (1,))
    stage(last + 1, buf_b, None, (), update_subs=(1,))

    for sub in range(2):
        for h in range(A_HEADS):
            o_ref[h * HEAD_DIM:(h + 1) * HEAD_DIM, sub * tq:(sub + 1) * tq] = (
                _normalised(acc_sc, sub * A_HEADS + h).astype(o_ref.dtype))


def _dsa(qt, qit, wit, ki, k, vt, b, s, tq):
    t = b * s
    nq = s // tq
    assert nq % 2 == 0, "dsa_attention walks query blocks in pairs"
    n_sel = min(TOPK_MAX, s // 4)
    idx_bits = max(1, (s - 1).bit_length())
    assert tq == KEY_BITS * SUBLANES, "the bit-sliced search transposes 32 sublane tiles per key block"
    qcol = lambda n: pl.BlockSpec((n, 2 * tq), lambda bi, j: (0, bi * (nq // 2) + j))
    full = lambda n: pl.BlockSpec((nq, n, tq), lambda bi, j: (bi, 0, 0))
    return pl.pallas_call(
        functools.partial(_dsa_kernel, tq=tq, n_sel=float(n_sel), idx_bits=idx_bits),
        grid=(b, nq // 2),
        in_specs=[qcol(A_Q), qcol(A_QI), qcol(IDX_HEADS),
                  pl.BlockSpec((s, LANES), lambda bi, j: (bi, 0)), full(A_KD), full(A_VT)],
        out_specs=pl.BlockSpec((A_Q, 2 * tq), lambda bi, j: (0, bi * (nq // 2) + j)),
        out_shape=jax.ShapeDtypeStruct((A_Q, t), BF16),
        scratch_shapes=[
            pltpu.VMEM((2, nq, tq, tq), I32),
            pltpu.VMEM((KEY_BITS, nq, SUBLANES, tq), I32),
            pltpu.VMEM((2 * IDX_HEADS, LANES, tq), BF16),
            pltpu.VMEM((2 * A_HEADS, HEAD_DIM, tq), BF16),
            pltpu.VMEM((2 * A_HEADS, tq, tq), F32),
            pltpu.VMEM((2 * A_HEADS, tq, tq), F32),
            pltpu.VMEM((2 * A_HEADS, SUBLANES, tq), F32),
            pltpu.VMEM((2 * A_HEADS, SUBLANES, tq), F32),
            pltpu.VMEM((2 * A_HEADS, 1, tq), F32),
            pltpu.VMEM((2 * A_HEADS, V_ROWS, tq), F32),
        ],
        compiler_params=_params("parallel", "arbitrary"),
        name="dsa_attention",
    )(qt, qit, wit, ki, k, vt)


B_QK = B_HEADS * HEAD_DIM
K_ROWS = HEAD_DIM + BF16_ROWS
B_KAUG = B_HEADS * K_ROWS
B_VT = B_HEADS * V_ROWS
BIAS_PIECES = 3


def _split3(x):
    hi = x.astype(BF16)
    r1 = x - hi.astype(F32)
    mid = r1.astype(BF16)
    low = (r1 - mid.astype(F32)).astype(BF16)
    return hi, mid, low


def _kvproj_kernel(h_ref, g_ref, wk_ref, wv_ref, wf_ref, bf_ref, place_ref, ones_ref,
                   qplace_ref, qones_ref, kaug_ref, vt_ref, qrows_ref, carry_sc):
    tm = h_ref.shape[0]
    tk = vt_ref.shape[2]

    @pl.when(pl.program_id(1) == 0)
    def _():
        carry_sc[...] = jnp.zeros(carry_sc.shape, F32)

    xn = _rms(h_ref[...], g_ref[...]).astype(BF16)
    z = jnp.dot(xn, wf_ref[...], preferred_element_type=F32) + bf_ref[...]
    logf = jnp.minimum(z, 0.0) - jnp.log1p(jnp.exp(-jnp.abs(z)))
    tri = (lax.broadcasted_iota(I32, (tm, tm), 0) >=
           lax.broadcasted_iota(I32, (tm, tm), 1)).astype(BF16)
    cs = sum(jnp.dot(tri, piece, preferred_element_type=F32) for piece in _split3(logf))
    cum = cs + carry_sc[...]
    carry_sc[...] = cum[tm - 1:tm, :]
    hi, mid, low = _split3(cum * LOG2E)
    pieces = jnp.concatenate([hi, mid, low], axis=1)
    qrows = lax.dot_general(qplace_ref[...], pieces, (((1,), (1,)), ((), ())),
                            preferred_element_type=F32) + qones_ref[...]
    qrows_ref[...] = qrows.astype(BF16)

    kstep = 2 * LANES
    for c0 in range(0, B_KAUG, kstep):
        y = (jnp.dot(xn, wk_ref[:, c0:c0 + kstep], preferred_element_type=F32) +
             jnp.dot(pieces, place_ref[:, c0:c0 + kstep], preferred_element_type=F32) +
             ones_ref[:, c0:c0 + kstep])
        for jj in range(kstep // LANES):
            r = c0 + jj * LANES
            for sb in range(tm // tk):
                kaug_ref[sb, r:r + LANES, :] = (
                    y[sb * tk:(sb + 1) * tk, jj * LANES:(jj + 1) * LANES].T.astype(BF16))
    step = 4 * LANES
    for c0 in range(0, B_QK, step):
        y = jnp.dot(xn, wv_ref[:, c0:c0 + step], preferred_element_type=F32)
        for jj in range(step // LANES):
            head = (c0 + jj * LANES) // HEAD_DIM
            for sb in range(tm // tk):
                yt = y[sb * tk:(sb + 1) * tk, jj * LANES:(jj + 1) * LANES].T.astype(BF16)
                for e in range(2):
                    _store_value_head(vt_ref, sb, head + e, yt[e * HEAD_DIM:(e + 1) * HEAD_DIM, :])


def _kvproj(h3, g, wk, wv, wf, bf, place, ones, qplace, qones, tm, tk):
    b, s, d = h3.shape
    ns = s // tm
    blk = lambda n: pl.BlockSpec((None, tm, n), lambda bi, i: (bi, i, 0))
    return pl.pallas_call(
        _kvproj_kernel,
        grid=(b, ns),
        in_specs=[blk(d), _resident((1, d)), _resident(wk.shape), _resident(wv.shape),
                  _resident(wf.shape), _resident((1, LANES)), _resident(place.shape),
                  _resident(ones.shape), _resident(qplace.shape), _resident(qones.shape)],
        out_specs=[pl.BlockSpec((tm // tk, B_KAUG, tk), lambda bi, i: (bi * ns + i, 0, 0)),
                   pl.BlockSpec((tm // tk, B_VT, tk), lambda bi, i: (bi * ns + i, 0, 0)),
                   pl.BlockSpec((B_HEADS * BF16_ROWS, tm), lambda bi, i: (0, bi * ns + i))],
        out_shape=[jax.ShapeDtypeStruct((b * s // tk, B_KAUG, tk), BF16),
                   jax.ShapeDtypeStruct((b * s // tk, B_VT, tk), BF16),
                   jax.ShapeDtypeStruct((B_HEADS * BF16_ROWS, b * s), BF16)],
        scratch_shapes=[pltpu.VMEM((1, LANES), F32)],
        compiler_params=_params("parallel", "arbitrary"),
        name="kv_proj_cumsum",
    )(h3, g, wk, wv, wf, bf, place, ones, qplace, qones)


def _qproj_kernel(h_ref, g_ref, w_ref, rows_ref, qt_ref):
    tm = h_ref.shape[0]
    xn = _rms(h_ref[...], g_ref[...]).astype(BF16)
    step = 4 * LANES
    for c0 in range(0, B_QK, step):
        y = jnp.dot(xn, w_ref[:, c0:c0 + step], preferred_element_type=F32) * QK_SCALE
        for jj in range(step // LANES):
            yt = y[:, jj * LANES:(jj + 1) * LANES].T.astype(BF16)
            for e in range(2):
                h = (c0 + jj * LANES) // HEAD_DIM + e
                r = h * K_ROWS
                qt_ref[r:r + HEAD_DIM, :] = yt[e * HEAD_DIM:(e + 1) * HEAD_DIM, :]
                qt_ref[r + HEAD_DIM:r + K_ROWS, :] = rows_ref[h * BF16_ROWS:(h + 1) * BF16_ROWS, :]


def _qproj(h, g, w, rows, tm):
    t, d = h.shape
    return pl.pallas_call(
        _qproj_kernel,
        grid=(t // tm,),
        in_specs=[pl.BlockSpec((tm, d), lambda i: (i, 0)), _resident((1, d)), _resident(w.shape),
                  pl.BlockSpec((B_HEADS * BF16_ROWS, tm), lambda i: (0, i))],
        out_specs=pl.BlockSpec((B_KAUG, tm), lambda i: (0, i)),
        out_shape=jax.ShapeDtypeStruct((B_KAUG, t), BF16),
        compiler_params=_params("parallel"),
        name="q_proj",
    )(h, g, w, rows)


FOX_HEADS_PER_STEP = 8


def _fox_kernel(qt_ref, k_ref, vt_ref, o_ref, qw_sc, sa_sc, sb_sc, mxa_sc, mxb_sc, m_sc, acc_sc,
                *, tq):
    tk = tq
    hps = FOX_HEADS_PER_STEP
    jj = pl.program_id(2)
    m_sc[...] = jnp.full(m_sc.shape, MASKED, F32)
    acc_sc[...] = jnp.zeros(acc_sc.shape, F32)
    kiota = lax.broadcasted_iota(I32, (tk, tq), 0)
    qiota = lax.broadcasted_iota(I32, (1, tq), 1)
    buf_a = (sa_sc, mxa_sc)
    buf_b = (sb_sc, mxb_sc)
    for sub in range(2):
        for h in range(hps):
            qw_sc[sub * hps + h] = qt_ref[h * K_ROWS:(h + 1) * K_ROWS, sub * tq:(sub + 1) * tq]

    def scores(c, sub, h, buf, qlimit):
        s = lax.dot_general(k_ref[c, h * K_ROWS:(h + 1) * K_ROWS, :], qw_sc[sub * hps + h],
                            (((0,), (0,)), ((), ())), preferred_element_type=F32)
        if qlimit is not None:
            s = jnp.where(kiota <= qlimit, s, MASKED)
        buf[0][sub * hps + h] = s
        buf[1][sub * hps + h] = _colmax8(s)

    def update(c, sub, h, buf):
        _softmax_pv(buf[0], buf[1], vt_ref[c, h * V_ROWS:(h + 1) * V_ROWS, :],
                    sub * hps + h, m_sc, acc_sc)

    def stage(c, cur, nxt, limits, update_subs=(0, 1)):
        for h in range(hps):
            for sub, qlimit in limits:
                scores(c + 1, sub, h, nxt, qlimit)
            for sub in update_subs:
                update(c, sub, h, cur)

    first_limit = qiota + jnp.where(jj == 0, 0, tk)
    for h in range(hps):
        scores(0, 0, h, buf_a, first_limit)
        scores(0, 1, h, buf_a, None)

    def main_body(i, carry):
        c = 2 * i
        stage(c, buf_a, buf_b, ((0, None), (1, None)))
        limit0 = qiota + jnp.where(i == jj - 1, 0, tk)
        stage(c + 1, buf_b, buf_a, ((0, limit0), (1, None)))
        return carry

    lax.fori_loop(0, jj, main_body, 0)
    last = 2 * jj
    stage(last, buf_a, buf_b, ((1, qiota),))
    stage(last + 1, buf_b, None, (), update_subs=(1,))
    for sub in range(2):
        for h in range(hps):
            o_ref[h * HEAD_DIM:(h + 1) * HEAD_DIM, sub * tq:(sub + 1) * tq] = (
                _normalised(acc_sc, sub * hps + h).astype(o_ref.dtype))


def _fox(qt, kaug, vt, b, s, tq):
    t = b * s
    nq = s // tq
    assert nq % 2 == 0, "fox_attention walks query blocks in pairs"
    hps = FOX_HEADS_PER_STEP
    nstate = 2 * hps
    return pl.pallas_call(
        functools.partial(_fox_kernel, tq=tq),
        grid=(b, B_HEADS // hps, nq // 2),
        in_specs=[
            pl.BlockSpec((hps * K_ROWS, 2 * tq), lambda bi, p, j: (p, bi * (nq // 2) + j)),
            pl.BlockSpec((nq, hps * K_ROWS, tq), lambda bi, p, j: (bi, p, 0)),
            pl.BlockSpec((nq, hps * V_ROWS, tq), lambda bi, p, j: (bi, p, 0)),
        ],
        out_specs=pl.BlockSpec((hps * HEAD_DIM, 2 * tq), lambda bi, p, j: (p, bi * (nq // 2) + j)),
        out_shape=jax.ShapeDtypeStruct((B_QK, t), BF16),
        scratch_shapes=[
            pltpu.VMEM((nstate, K_ROWS, tq), BF16),
            pltpu.VMEM((nstate, tq, tq), F32),
            pltpu.VMEM((nstate, tq, tq), F32),
            pltpu.VMEM((nstate, SUBLANES, tq), F32),
            pltpu.VMEM((nstate, SUBLANES, tq), F32),
            pltpu.VMEM((nstate, 1, tq), F32),
            pltpu.VMEM((nstate, V_ROWS, tq), F32),
        ],
        compiler_params=_params("parallel", "parallel", "arbitrary"),
        name="fox_attention",
    )(qt, kaug, vt)


def _mlp_kernel(h_ref, o_ref, wo_ref, g_ref, wup_ref, wdn_ref, gf_ref, out_ref, *, tf, final):
    h1 = h_ref[...] + lax.dot_general(o_ref[...], wo_ref[...], (((0,), (0,)), ((), ())),
                                      preferred_element_type=F32)
    xn = _rms(h1, g_ref[...]).astype(BF16)
    acc = h1
    for c0 in range(0, wup_ref.shape[1], tf):
        u = jnp.dot(xn, wup_ref[:, c0:c0 + tf], preferred_element_type=F32)
        a = jnp.square(jnp.maximum(u, 0.0)).astype(BF16)
        acc = acc + jnp.dot(a, wdn_ref[c0:c0 + tf, :], preferred_element_type=F32)
    if final:
        acc = _rms(acc, gf_ref[...])
    out_ref[...] = acc


def _mlp(h, o, wo, g, wup, wdn, gf, final, tm, tf):
    t, d = h.shape
    row = pl.BlockSpec((tm, d), lambda i: (i, 0))
    return pl.pallas_call(
        functools.partial(_mlp_kernel, tf=tf, final=final),
        grid=(t // tm,),
        in_specs=[row, pl.BlockSpec((o.shape[0], tm), lambda i: (0, i)), _resident(wo.shape),
                  _resident((1, d)), _resident(wup.shape), _resident(wdn.shape), _resident((1, d))],
        out_specs=row,
        out_shape=jax.ShapeDtypeStruct((t, d), F32),
        compiler_params=_params("parallel"),
        name="oproj_mlp",
    )(h, o, wo, g, wup, wdn, gf)


def _prep_w_in_a(w):
    kv = A_KV_HEADS * HEAD_DIM
    o_k, o_v, o_qi = A_Q, A_Q + kv, A_Q + 2 * kv
    o_ki, o_wi = o_qi + A_QI, o_qi + A_QI + HEAD_DIM
    wi = w[:, o_wi:o_wi + IDX_HEADS]
    return jnp.concatenate([
        w[:, :o_ki],
        w[:, o_ki:o_wi], w[:, o_ki:o_wi],
        jnp.pad(wi, ((0, 0), (0, LANES - IDX_HEADS))),
    ], axis=1).astype(BF16)


def _rope_tables(seq):
    inv = 1.0 / (ROPE_THETA ** (jnp.arange(0, HEAD_DIM, 2, dtype=F32) / HEAD_DIM))
    ang = jnp.arange(seq, dtype=F32)[:, None] * inv[None, :]
    c, s = jnp.cos(ang), jnp.sin(ang)
    return jnp.concatenate([c, c, c, c], axis=1), jnp.concatenate([-s, s, -s, s], axis=1)


def _bias_placement():
    place = np.zeros((BIAS_PIECES * LANES, B_KAUG), np.float32)
    ones = np.zeros((1, B_KAUG), np.float32)
    for h in range(B_HEADS):
        for p in range(BIAS_PIECES):
            place[p * LANES + h, h * K_ROWS + HEAD_DIM + p] = -1.0
            ones[0, h * K_ROWS + HEAD_DIM + BIAS_PIECES + p] = 1.0
    return jnp.asarray(place, BF16), jnp.asarray(ones, F32)


def _query_bias_placement(tm):
    qplace = np.zeros((B_HEADS * BF16_ROWS, BIAS_PIECES * LANES), np.float32)
    qones = np.zeros((B_HEADS * BF16_ROWS, tm), np.float32)
    for h in range(B_HEADS):
        for p in range(BIAS_PIECES):
            qplace[h * BF16_ROWS + BIAS_PIECES + p, p * LANES + h] = 1.0
            qones[h * BF16_ROWS + p, :] = 1.0
    return jnp.asarray(qplace, BF16), jnp.asarray(qones, F32)


def kernel(x, g_attn_a, w_in_a, w_o_a, g_kv, w_kv_b, b_f, g_attn_b, w_q_b, w_o_b,
           g_mlp, w_up, w_down, g_final):
    b, s, d = x.shape
    t = b * s
    depth = g_mlp.shape[0]
    n_a = g_attn_a.shape[0]
    tm = min(512, s)
    tq = min(ATTN_BLOCK, s)
    tf = 512

    cos, sin = _rope_tables(s)
    gf = g_final.reshape(1, d)
    h = x.reshape(t, d)
    kaug = vt_b = bias_rows = None

    for layer in range(depth):
        if layer < n_a:
            qt, k, vt, qit, ki, wit = _aproj(h, g_attn_a[layer].reshape(1, d),
                                             _prep_w_in_a(w_in_a[layer]), cos, sin, s, tm, tq)
            o = _dsa(qt, qit, wit, ki, k, vt, b, s, tq)
            w_o = w_o_a[layer]
        else:
            jb = layer - n_a
            if jb == 0:
                wk = jnp.pad(w_kv_b[:, :B_QK].reshape(d, B_HEADS, HEAD_DIM),
                             ((0, 0), (0, 0), (0, K_ROWS - HEAD_DIM))).reshape(d, B_KAUG)
                wf = jnp.pad(w_kv_b[:, 2 * B_QK:], ((0, 0), (0, LANES - B_HEADS)))
                bias = jnp.pad(b_f, (0, LANES - B_HEADS)).reshape(1, LANES)
                place, ones = _bias_placement()
                qplace, qones = _query_bias_placement(tm)
                kaug, vt_b, bias_rows = _kvproj(
                    h.reshape(b, s, d), g_kv.reshape(1, d), wk.astype(BF16),
                    w_kv_b[:, B_QK:2 * B_QK].astype(BF16), wf.astype(BF16), bias,
                    place, ones, qplace, qones, tm, tq)
            qt = _qproj(h, g_attn_b[jb].reshape(1, d), w_q_b[jb].astype(BF16), bias_rows, tm)
            o = _fox(qt, kaug, vt_b, b, s, tq)
            w_o = w_o_b[jb]
        h = _mlp(h, o, w_o.astype(BF16), g_mlp[layer].reshape(1, d),
                 w_up[layer].astype(BF16), w_down[layer].astype(BF16), gf,
                 layer == depth - 1, tm, tf)
    return h.reshape(b, s, d)
```

```python
import functools

import numpy as np
import jax
import jax.numpy as jnp
from jax import lax
from jax.experimental import pallas as pl
from jax.experimental.pallas import tpu as pltpu

LANES = 128
SUBLANES = 8
BF16_ROWS = 16
HEAD_DIM = 64
ROPE_HALF = HEAD_DIM // 2
CHUNK = 64
A_HEADS = 16
A_KV_HEADS = 4
A_GROUP = A_HEADS // A_KV_HEADS
IDX_HEADS = 8
TOPK_MAX = 256
B_HEADS = 16
ROPE_THETA = 10000.0
EPS = 1e-6
INT_MIN = -(2 ** 31)
MASKED = -1e30
LOG2E = 1.4426950408889634
QK_SCALE = HEAD_DIM ** -0.5 * LOG2E
VMEM_LIMIT_BYTES = 56 * 1024 * 1024
ATTN_BLOCK = 256

F32 = jnp.float32
BF16 = jnp.bfloat16
I32 = jnp.int32

_DIM0 = (((0,), (0,)), ((), ()))


def _params(*sem):
    return pltpu.CompilerParams(dimension_semantics=sem,
                                vmem_limit_bytes=VMEM_LIMIT_BYTES)


def _resident(shape):
    nd = len(shape)
    return pl.BlockSpec(shape, lambda *_: (0,) * nd, pipeline_mode=pl.Buffered(1))


def _rms(x, g):
    return x * lax.rsqrt(jnp.mean(x * x, axis=-1, keepdims=True) + EPS) * g


def _colsum8(x):
    n, w = x.shape
    return x.reshape(n // SUBLANES, SUBLANES, w).sum(axis=0)


def _colmax8(x):
    n, w = x.shape
    return x.reshape(n // SUBLANES, SUBLANES, w).max(axis=0)


V_ROWS = HEAD_DIM + BF16_ROWS


def _store_value_head(vt_ref, sb, head, yt):
    r = head * V_ROWS
    tk = yt.shape[1]
    vt_ref[sb, r:r + HEAD_DIM, :] = yt
    first_row = lax.broadcasted_iota(I32, (BF16_ROWS, tk), 0) == 0
    vt_ref[sb, r + HEAD_DIM:r + V_ROWS, :] = jnp.where(first_row, 1.0, 0.0).astype(BF16)


def _softmax_pv(s_buf, mx_buf, vt, h, m_sc, acc_sc):
    m_old = m_sc[h]
    m_new = jnp.maximum(m_old, jnp.max(mx_buf[h], axis=0, keepdims=True))
    alpha = jnp.exp2(m_old - m_new)
    p = jnp.exp2(s_buf[h] - m_new)
    acc_sc[h] = alpha * acc_sc[h] + jnp.dot(vt, p.astype(BF16), preferred_element_type=F32)
    m_sc[h] = m_new


def _normalised(acc_sc, h):
    acc = acc_sc[h]
    return acc[0:HEAD_DIM, :] / acc[HEAD_DIM:HEAD_DIM + 1, :]


A_Q = A_HEADS * HEAD_DIM
A_KD = A_KV_HEADS * HEAD_DIM
A_V = A_KV_HEADS * HEAD_DIM
A_VT = A_KV_HEADS * V_ROWS
A_QI = IDX_HEADS * HEAD_DIM
A_OFF_K = A_Q
A_OFF_V = A_OFF_K + A_KD
A_OFF_QI = A_OFF_V + A_V
A_OFF_KI = A_OFF_QI + A_QI
A_OFF_WI = A_OFF_KI + LANES
A_COLS = A_OFF_WI + LANES


def _aproj_kernel(h_ref, g_ref, w_ref, cos_ref, sin_ref, cost_ref, sint_ref,
                  qt_ref, k_ref, vt_ref, qit_ref, ki_ref, wit_ref, *, wi_scale):
    tm = h_ref.shape[0]
    tk = vt_ref.shape[2]
    xn = _rms(h_ref[...], g_ref[...]).astype(BF16)
    cos = cos_ref[...]
    sin = sin_ref[...]
    cos_t = cost_ref[...]
    sin_t = sint_ref[...]
    lane = lax.broadcasted_iota(I32, (tm, LANES), 1)
    first_half = (lane & ROPE_HALF) == 0

    def rope(y):
        ys = jnp.where(first_half, pltpu.roll(y, LANES - ROPE_HALF, 1),
                       pltpu.roll(y, ROPE_HALF, 1))
        return y * cos + ys * sin

    def rope_t(yt):
        h = ROPE_HALF
        partner = jnp.concatenate([yt[h:2 * h], yt[0:h], yt[3 * h:4 * h], yt[2 * h:3 * h]], axis=0)
        return yt * cos_t + partner * sin_t

    def chunks(c0, n):
        step = 4 * LANES
        for s0 in range(0, n, step):
            sn = min(step, n - s0)
            y = jnp.dot(xn, w_ref[:, c0 + s0:c0 + s0 + sn], preferred_element_type=F32)
            for j in range(sn // LANES):
                yield (s0 // LANES + j), y[:, j * LANES:(j + 1) * LANES]

    for j, y in chunks(0, A_Q):
        qt_ref[j * LANES:(j + 1) * LANES, :] = (rope_t(y.T) * QK_SCALE).astype(BF16)
    for j, y in chunks(A_OFF_K, A_KD):
        r = rope_t(y.T).astype(BF16)
        for sb in range(tm // tk):
            k_ref[sb, j * LANES:(j + 1) * LANES, :] = r[:, sb * tk:(sb + 1) * tk]
    for j, y in chunks(A_OFF_V, A_V):
        for sb in range(tm // tk):
            yt = y[sb * tk:(sb + 1) * tk].T.astype(BF16)
            for e in range(2):
                _store_value_head(vt_ref, sb, 2 * j + e, yt[e * HEAD_DIM:(e + 1) * HEAD_DIM, :])
    for j, y in chunks(A_OFF_QI, A_QI):
        qit_ref[j * LANES:(j + 1) * LANES, :] = rope_t(y.T).astype(BF16)
    for j, y in chunks(A_OFF_KI, LANES):
        ki_ref[...] = rope(y).astype(BF16)
    for j, y in chunks(A_OFF_WI, LANES):
        wit_ref[...] = (y * wi_scale).T[:IDX_HEADS, :]


def _aproj(h, g, w, cos, sin, seq, tm, tk):
    t, d = h.shape
    npos = seq // tm
    row = lambda n: pl.BlockSpec((tm, n), lambda i: (i, 0))
    col = lambda n: pl.BlockSpec((n, tm), lambda i: (0, i))
    blocked = lambda n: pl.BlockSpec((tm // tk, n, tk), lambda i: (i, 0, 0))
    tab = pl.BlockSpec((tm, LANES), lambda i: (i % npos, 0))
    tab_t = pl.BlockSpec((LANES, tm), lambda i: (0, i % npos))
    wi_scale = IDX_HEADS ** -0.5 * HEAD_DIM ** -0.5
    return pl.pallas_call(
        functools.partial(_aproj_kernel, wi_scale=wi_scale),
        grid=(t // tm,),
        in_specs=[row(d), _resident((1, d)), _resident((d, A_COLS)), tab, tab, tab_t, tab_t],
        out_specs=[col(A_Q), blocked(A_KD), blocked(A_VT), col(A_QI), row(LANES),
                   col(IDX_HEADS)],
        out_shape=[jax.ShapeDtypeStruct((A_Q, t), BF16),
                   jax.ShapeDtypeStruct((t // tk, A_KD, tk), BF16),
                   jax.ShapeDtypeStruct((t // tk, A_VT, tk), BF16),
                   jax.ShapeDtypeStruct((A_QI, t), BF16),
                   jax.ShapeDtypeStruct((t, LANES), BF16),
                   jax.ShapeDtypeStruct((IDX_HEADS, t), F32)],
        compiler_params=_params("parallel"),
        name="a_proj",
    )(h, g, w, cos, sin, cos.T, sin.T)


KEY_BITS = 32
BIT_TRANSPOSE_STEPS = (16, 8, 4, 2, 1)
BIT_TRANSPOSE_MASKS = (0x0000FFFF, 0x00FF00FF, 0x0F0F0F0F, 0x33333333, 0x55555555)


def _dsa_kernel(qt_ref, qit_ref, wit_ref, ki_ref, k_ref, vt_ref, o_ref,
                key_sc, planes_sc, qiw_sc, qw_sc, sa_sc, sb_sc, mxa_sc, mxb_sc, m_sc, acc_sc,
                *, tq, n_sel, idx_bits):
    tk = tq
    jj = pl.program_id(1)
    kiota = lax.broadcasted_iota(I32, (tk, tq), 0)

    zeros_half = jnp.zeros((HEAD_DIM, tq), BF16)
    for sub in range(2):
        cols = slice(sub * tq, (sub + 1) * tq)
        for h in range(IDX_HEADS):
            qiw_sc[sub * IDX_HEADS + h, 0:HEAD_DIM, :] = qit_ref[h * HEAD_DIM:(h + 1) * HEAD_DIM, cols]
            qiw_sc[sub * IDX_HEADS + h, HEAD_DIM:LANES, :] = zeros_half
        for h in range(A_HEADS):
            qw_sc[sub * A_HEADS + h] = qt_ref[h * HEAD_DIM:(h + 1) * HEAD_DIM, cols]

    def select(sub):
        j = 2 * jj + sub
        nkb = j + 1
        keys = key_sc.at[sub]
        qpos = j * tq + lax.broadcasted_iota(I32, (1, tq), 1)
        limit = (qpos // CHUNK + 1) * CHUNK

        def head_logits(c):
            kc = ki_ref[pl.ds(pl.multiple_of(c * tk, tk), tk), :]
            return [jnp.dot(kc, qiw_sc[sub * IDX_HEADS + h], preferred_element_type=F32)
                    for h in range(IDX_HEADS)]

        def store_keys(c, logits):
            sc = jnp.zeros((tk, tq), F32)
            for h in range(IDX_HEADS):
                sc = sc + jnp.maximum(logits[h], 0.0) * wit_ref[h:h + 1, sub * tq:(sub + 1) * tq]
            sc = jnp.where(sc == 0.0, 0.0, sc)
            bits = pltpu.bitcast(sc, I32)
            key = bits ^ ((bits >> 31) & 0x7FFFFFFF)
            keys[c] = jnp.where(c * tk + kiota < limit, key, INT_MIN)

        def bit_slice(c):
            x = keys[c] ^ INT_MIN
            tiles = [x[v * SUBLANES:(v + 1) * SUBLANES, :] for v in range(KEY_BITS)]
            for jbit, mask in zip(BIT_TRANSPOSE_STEPS, BIT_TRANSPOSE_MASKS):
                for v in range(KEY_BITS):
                    if v & jbit == 0:
                        lo, hi = tiles[v], tiles[v + jbit]
                        t = (lax.shift_right_logical(lo, jbit) ^ hi) & mask
                        tiles[v] = lo ^ jnp.left_shift(t, jbit)
                        tiles[v + jbit] = hi ^ t
            for b in range(KEY_BITS):
                planes_sc[b, c] = tiles[b]

        store_keys(0, head_logits(0))

        def score_body(c, carry):
            logits = head_logits(c)
            bit_slice(c - 1)
            store_keys(c, logits)
            return carry

        lax.fori_loop(1, nkb, score_body, 0)
        bit_slice(nkb - 1)

        def clear_body(c, carry):
            planes_sc[:, c] = jnp.zeros((KEY_BITS, SUBLANES, tq), I32)
            return carry

        lax.fori_loop(nkb, planes_sc.shape[1], clear_body, 0)

        @pl.when(nkb % 2 == 1)
        def _():
            keys[nkb] = jnp.full((tk, tq), INT_MIN, I32)

        def count(pred):
            def body(cc, acc):
                for u in range(2):
                    c = 2 * cc + u
                    acc = acc + _colsum8(jnp.where(pred(c, keys[c]), 1.0, 0.0))
                return acc
            acc = lax.fori_loop(0, (nkb + 1) // 2, body, jnp.zeros((SUBLANES, tq), F32))
            return jnp.sum(acc, axis=0, keepdims=True)

        nwords = planes_sc.shape[1] * SUBLANES
        word_block = lax.broadcasted_iota(I32, (nwords, 1), 0) // SUBLANES
        cand = jnp.broadcast_to(jnp.where(word_block < nkb, -1, 0), (nwords, tq))

        def ones_count(words):
            cnt = lax.population_count(words).reshape(nwords // SUBLANES, SUBLANES, tq).sum(axis=0)
            return jnp.sum(cnt.astype(F32), axis=0, keepdims=True)

        thr_u = jnp.zeros((1, tq), I32)
        above = jnp.zeros((1, tq), F32)
        for b in range(KEY_BITS - 1, -1, -1):
            plane = planes_sc[b].reshape(nwords, tq)
            ones = cand & plane
            cnt1 = ones_count(ones)
            take = above + cnt1 >= n_sel
            thr_u = jnp.where(take, thr_u | (INT_MIN if b == KEY_BITS - 1 else 1 << b), thr_u)
            above = jnp.where(take, above, above + cnt1)
            cand = jnp.where(take, ones, cand ^ ones)
        cnt_thr = jnp.where(thr_u == 0, 0.0, above + ones_count(cand))
        thr = jnp.maximum(thr_u ^ INT_MIN, INT_MIN + 1)

        @pl.when(jnp.max(cnt_thr) > n_sel)
        def _():
            need = n_sel - count(lambda c, key: key > thr)
            pos = jnp.zeros((1, tq), I32)
            for b in range(idx_bits - 1, -1, -1):
                cand = pos | (1 << b)
                cnt = count(lambda c, key: jnp.where(key == thr, c * tk + kiota, 2 ** 30) < cand)
                pos = jnp.where(cnt < need, cand, pos)

            def demote(c, carry):
                key = keys[c]
                tie_pos = jnp.where(key == thr, c * tk + kiota, -1)
                keys[c] = jnp.where(tie_pos > pos, INT_MIN, key)
                return carry

            lax.fori_loop(0, nkb, demote, 0)

        return thr

    thrs = [select(0), select(1)]

    m_sc[...] = jnp.full(m_sc.shape, MASKED, F32)
    acc_sc[...] = jnp.zeros(acc_sc.shape, F32)
    buf_a = (sa_sc, mxa_sc)
    buf_b = (sb_sc, mxb_sc)

    def select_bias(c, sub):
        return jnp.where(key_sc[sub, c] >= thrs[sub], 0.0, MASKED)

    def scores(c, sub, h, buf, bias):
        n = h // A_GROUP
        s = lax.dot_general(k_ref[c, n * HEAD_DIM:(n + 1) * HEAD_DIM, :], qw_sc[sub * A_HEADS + h],
                            _DIM0, preferred_element_type=F32) + bias
        buf[0][sub * A_HEADS + h] = s
        buf[1][sub * A_HEADS + h] = _colmax8(s)

    def update(c, sub, h, buf):
        n = h // A_GROUP
        _softmax_pv(buf[0], buf[1], vt_ref[c, n * V_ROWS:(n + 1) * V_ROWS, :],
                    sub * A_HEADS + h, m_sc, acc_sc)

    def stage(c, cur, nxt, score_subs, update_subs=(0, 1)):
        biases = {sub: select_bias(c + 1, sub) for sub in score_subs}
        for h in range(A_HEADS):
            for sub in score_subs:
                scores(c + 1, sub, h, nxt, biases[sub])
            for sub in update_subs:
                update(c, sub, h, cur)

    for sub in range(2):
        bias0 = select_bias(0, sub)
        for h in range(A_HEADS):
            scores(0, sub, h, buf_a, bias0)

    def attn_body(i, carry):
        stage(2 * i, buf_a, buf_b, (0, 1))
        stage(2 * i + 1, buf_b, buf_a, (0, 1))
        return carry

    lax.fori_loop(0, jj, attn_body, 0)
    last = 2 * jj
    stage(last, buf_a, buf_b, (1,))
    stage(last + 1, buf_b, None, (), update_subs=(1,))

    for sub in range(2):
        for h in range(A_HEADS):
            o_ref[h * HEAD_DIM:(h + 1) * HEAD_DIM, sub * tq:(sub + 1) * tq] = (
                _normalised(acc_sc, sub * A_HEADS + h).astype(o_ref.dtype))


def _dsa(qt, qit, wit, ki, k, vt, b, s, tq):
    t = b * s
    nq = s // tq
    assert nq % 2 == 0, "dsa_attention walks query blocks in pairs"
    n_sel = min(TOPK_MAX, s // 4)
    idx_bits = max(1, (s - 1).bit_length())
    assert tq == KEY_BITS * SUBLANES, "the bit-sliced search transposes 32 sublane tiles per key block"
    qcol = lambda n: pl.BlockSpec((n, 2 * tq), lambda bi, j: (0, bi * (nq // 2) + j))
    full = lambda n: pl.BlockSpec((nq, n, tq), lambda bi, j: (bi, 0, 0))
    return pl.pallas_call(
        functools.partial(_dsa_kernel, tq=tq, n_sel=float(n_sel), idx_bits=idx_bits),
        grid=(b, nq // 2),
        in_specs=[qcol(A_Q), qcol(A_QI), qcol(IDX_HEADS),
                  pl.BlockSpec((s, LANES), lambda bi, j: (bi, 0)), full(A_KD), full(A_VT)],
        out_specs=pl.BlockSpec((A_Q, 2 * tq), lambda bi, j: (0, bi * (nq // 2) + j)),
        out_shape=jax.ShapeDtypeStruct((A_Q, t), BF16),
        scratch_shapes=[
            pltpu.VMEM((2, nq, tq, tq), I32),
            pltpu.VMEM((KEY_BITS, nq, SUBLANES, tq), I32),
            pltpu.VMEM((2 * IDX_HEADS, LANES, tq), BF16),
            pltpu.VMEM((2 * A_HEADS, HEAD_DIM, tq), BF16),
            pltpu.VMEM((2 * A_HEADS, tq, tq), F32),
            pltpu.VMEM((2 * A_HEADS, tq, tq), F32),
            pltpu.VMEM((2 * A_HEADS, SUBLANES, tq), F32),
            pltpu.VMEM((2 * A_HEADS, SUBLANES, tq), F32),
            pltpu.VMEM((2 * A_HEADS, 1, tq), F32),
            pltpu.VMEM((2 * A_HEADS, V_ROWS, tq), F32),
        ],
        compiler_params=_params("parallel", "arbitrary"),
        name="dsa_attention",
    )(qt, qit, wit, ki, k, vt)


B_QK = B_HEADS * HEAD_DIM
K_ROWS = HEAD_DIM + BF16_ROWS
B_KAUG = B_HEADS * K_ROWS
B_VT = B_HEADS * V_ROWS
BIAS_PIECES = 3


def _split3(x):
    hi = x.astype(BF16)
    r1 = x - hi.astype(F32)
    mid = r1.astype(BF16)
    low = (r1 - mid.astype(F32)).astype(BF16)
    return hi, mid, low


def _kvproj_kernel(h_ref, g_ref, wk_ref, wv_ref, wf_ref, bf_ref, place_ref, ones_ref,
                   qplace_ref, qones_ref, kaug_ref, vt_ref, qrows_ref, carry_sc):
    tm = h_ref.shape[0]
    tk = vt_ref.shape[2]

    @pl.when(pl.program_id(1) == 0)
    def _():
        carry_sc[...] = jnp.zeros(carry_sc.shape, F32)

    xn = _rms(h_ref[...], g_ref[...]).astype(BF16)
    z = jnp.dot(xn, wf_ref[...], preferred_element_type=F32) + bf_ref[...]
    logf = jnp.minimum(z, 0.0) - jnp.log1p(jnp.exp(-jnp.abs(z)))
    tri = (lax.broadcasted_iota(I32, (tm, tm), 0) >=
           lax.broadcasted_iota(I32, (tm, tm), 1)).astype(BF16)
    cs = sum(jnp.dot(tri, piece, preferred_element_type=F32) for piece in _split3(logf))
    cum = cs + carry_sc[...]
    carry_sc[...] = cum[tm - 1:tm, :]
    hi, mid, low = _split3(cum * LOG2E)
    pieces = jnp.concatenate([hi, mid, low], axis=1)
    qrows = lax.dot_general(qplace_ref[...], pieces, (((1,), (1,)), ((), ())),
                            preferred_element_type=F32) + qones_ref[...]
    qrows_ref[...] = qrows.astype(BF16)

    kstep = 2 * LANES
    for c0 in range(0, B_KAUG, kstep):
        y = (jnp.dot(xn, wk_ref[:, c0:c0 + kstep], preferred_element_type=F32) +
             jnp.dot(pieces, place_ref[:, c0:c0 + kstep], preferred_element_type=F32) +
             ones_ref[:, c0:c0 + kstep])
        for jj in range(kstep // LANES):
            r = c0 + jj * LANES
            for sb in range(tm // tk):
                kaug_ref[sb, r:r + LANES, :] = (
                    y[sb * tk:(sb + 1) * tk, jj * LANES:(jj + 1) * LANES].T.astype(BF16))
    step = 4 * LANES
    for c0 in range(0, B_QK, step):
        y = jnp.dot(xn, wv_ref[:, c0:c0 + step], preferred_element_type=F32)
        for jj in range(step // LANES):
            head = (c0 + jj * LANES) // HEAD_DIM
            for sb in range(tm // tk):
                yt = y[sb * tk:(sb + 1) * tk, jj * LANES:(jj + 1) * LANES].T.astype(BF16)
                for e in range(2):
                    _store_value_head(vt_ref, sb, head + e, yt[e * HEAD_DIM:(e + 1) * HEAD_DIM, :])


def _kvproj(h3, g, wk, wv, wf, bf, place, ones, qplace, qones, tm, tk):
    b, s, d = h3.shape
    ns = s // tm
    blk = lambda n: pl.BlockSpec((None, tm, n), lambda bi, i: (bi, i, 0))
    return pl.pallas_call(
        _kvproj_kernel,
        grid=(b, ns),
        in_specs=[blk(d), _resident((1, d)), _resident(wk.shape), _resident(wv.shape),
                  _resident(wf.shape), _resident((1, LANES)), _resident(place.shape),
                  _resident(ones.shape), _resident(qplace.shape), _resident(qones.shape)],
        out_specs=[pl.BlockSpec((tm // tk, B_KAUG, tk), lambda bi, i: (bi * ns + i, 0, 0)),
                   pl.BlockSpec((tm // tk, B_VT, tk), lambda bi, i: (bi * ns + i, 0, 0)),
                   pl.BlockSpec((B_HEADS * BF16_ROWS, tm), lambda bi, i: (0, bi * ns + i))],
        out_shape=[jax.ShapeDtypeStruct((b * s // tk, B_KAUG, tk), BF16),
                   jax.ShapeDtypeStruct((b * s // tk, B_VT, tk), BF16),
                   jax.ShapeDtypeStruct((B_HEADS * BF16_ROWS, b * s), BF16)],
        scratch_shapes=[pltpu.VMEM((1, LANES), F32)],
        compiler_params=_params("parallel", "arbitrary"),
        name="kv_proj_cumsum",
    )(h3, g, wk, wv, wf, bf, place, ones, qplace, qones)


def _qproj_kernel(h_ref, g_ref, w_ref, rows_ref, qt_ref):
    tm = h_ref.shape[0]
    xn = _rms(h_ref[...], g_ref[...]).astype(BF16)
    step = 4 * LANES
    for c0 in range(0, B_QK, step):
        y = jnp.dot(xn, w_ref[:, c0:c0 + step], preferred_element_type=F32) * QK_SCALE
        for jj in range(step // LANES):
            yt = y[:, jj * LANES:(jj + 1) * LANES].T.astype(BF16)
            for e in range(2):
                h = (c0 + jj * LANES) // HEAD_DIM + e
                r = h * K_ROWS
                qt_ref[r:r + HEAD_DIM, :] = yt[e * HEAD_DIM:(e + 1) * HEAD_DIM, :]
                qt_ref[r + HEAD_DIM:r + K_ROWS, :] = rows_ref[h * BF16_ROWS:(h + 1) * BF16_ROWS, :]


def _qproj(h, g, w, rows, tm):
    t, d = h.shape
    return pl.pallas_call(
        _qproj_kernel,
        grid=(t // tm,),
        in_specs=[pl.BlockSpec((tm, d), lambda i: (i, 0)), _resident((1, d)), _resident(w.shape),
                  pl.BlockSpec((B_HEADS * BF16_ROWS, tm), lambda i: (0, i))],
        out_specs=pl.BlockSpec((B_KAUG, tm), lambda i: (0, i)),
        out_shape=jax.ShapeDtypeStruct((B_KAUG, t), BF16),
        compiler_params=_params("parallel"),
        name="q_proj",
    )(h, g, w, rows)


FOX_HEADS_PER_STEP = 8


def _fox_kernel(qt_ref, k_ref, vt_ref, o_ref, qw_sc, sa_sc, sb_sc, mxa_sc, mxb_sc, m_sc, acc_sc,
                *, tq):
    tk = tq
    hps = FOX_HEADS_PER_STEP
    jj = pl.program_id(2)
    m_sc[...] = jnp.full(m_sc.shape, MASKED, F32)
    acc_sc[...] = jnp.zeros(acc_sc.shape, F32)
    kiota = lax.broadcasted_iota(I32, (tk, tq), 0)
    qiota = lax.broadcasted_iota(I32, (1, tq), 1)
    buf_a = (sa_sc, mxa_sc)
    buf_b = (sb_sc, mxb_sc)
    for sub in range(2):
        for h in range(hps):
            qw_sc[sub * hps + h] = qt_ref[h * K_ROWS:(h + 1) * K_ROWS, sub * tq:(sub + 1) * tq]

    def scores(c, sub, h, buf, qlimit):
        s = lax.dot_general(k_ref[c, h * K_ROWS:(h + 1) * K_ROWS, :], qw_sc[sub * hps + h],
                            (((0,), (0,)), ((), ())), preferred_element_type=F32)
        if qlimit is not None:
            s = jnp.where(kiota <= qlimit, s, MASKED)
        buf[0][sub * hps + h] = s
        buf[1][sub * hps + h] = _colmax8(s)

    def update(c, sub, h, buf):
        _softmax_pv(buf[0], buf[1], vt_ref[c, h * V_ROWS:(h + 1) * V_ROWS, :],
                    sub * hps + h, m_sc, acc_sc)

    def stage(c, cur, nxt, limits, update_subs=(0, 1)):
        for h in range(hps):
            for sub, qlimit in limits:
                scores(c + 1, sub, h, nxt, qlimit)
            for sub in update_subs:
                update(c, sub, h, cur)

    first_limit = qiota + jnp.where(jj == 0, 0, tk)
    for h in range(hps):
        scores(0, 0, h, buf_a, first_limit)
        scores(0, 1, h, buf_a, None)

    def main_body(i, carry):
        c = 2 * i
        stage(c, buf_a, buf_b, ((0, None), (1, None)))
        limit0 = qiota + jnp.where(i == jj - 1, 0, tk)
        stage(c + 1, buf_b, buf_a, ((0, limit0), (1, None)))
        return carry

    lax.fori_loop(0, jj, main_body, 0)
    last = 2 * jj
    stage(last, buf_a, buf_b, ((1, qiota),))
    stage(last + 1, buf_b, None, (), update_subs=(1,))
    for sub in range(2):
        for h in range(hps):
            o_ref[h * HEAD_DIM:(h + 1) * HEAD_DIM, sub * tq:(sub + 1) * tq] = (
                _normalised(acc_sc, sub * hps + h).astype(o_ref.dtype))


def _fox(qt, kaug, vt, b, s, tq):
    t = b * s
    nq = s // tq
    assert nq % 2 == 0, "fox_attention walks query blocks in pairs"
    hps = FOX_HEADS_PER_STEP
    nstate = 2 * hps
    return pl.pallas_call(
        functools.partial(_fox_kernel, tq=tq),
        grid=(b, B_HEADS // hps, nq // 2),
        in_specs=[
            pl.BlockSpec((hps * K_ROWS, 2 * tq), lambda bi, p, j: (p, bi * (nq // 2) + j)),
            pl.BlockSpec((nq, hps * K_ROWS, tq), lambda bi, p, j: (bi, p, 0)),
            pl.BlockSpec((nq, hps * V_ROWS, tq), lambda bi, p, j: (bi, p, 0)),
        ],
        out_specs=pl.BlockSpec((hps * HEAD_DIM, 2 * tq), lambda bi, p, j: (p, bi * (nq // 2) + j)),
        out_shape=jax.ShapeDtypeStruct((B_QK, t), BF16),
        scratch_shapes=[
            pltpu.VMEM((nstate, K_ROWS, tq), BF16),
            pltpu.VMEM((nstate, tq, tq), F32),
            pltpu.VMEM((nstate, tq, tq), F32),
            pltpu.VMEM((nstate, SUBLANES, tq), F32),
            pltpu.VMEM((nstate, SUBLANES, tq), F32),
            pltpu.VMEM((nstate, 1, tq), F32),
            pltpu.VMEM((nstate, V_ROWS, tq), F32),
        ],
        compiler_params=_params("parallel", "parallel", "arbitrary"),
        name="fox_attention",
    )(qt, kaug, vt)


def _mlp_kernel(h_ref, o_ref, wo_ref, g_ref, wup_ref, wdn_ref, gf_ref, out_ref, *, tf, final):
    h1 = h_ref[...] + lax.dot_general(o_ref[...], wo_ref[...], (((0,), (0,)), ((), ())),
                                      preferred_element_type=F32)
    xn = _rms(h1, g_ref[...]).astype(BF16)
    acc = h1
    for c0 in range(0, wup_ref.shape[1], tf):
        u = jnp.dot(xn, wup_ref[:, c0:c0 + tf], preferred_element_type=F32)
        a = jnp.square(jnp.maximum(u, 0.0)).astype(BF16)
        acc = acc + jnp.dot(a, wdn_ref[c0:c0 + tf, :], preferred_element_type=F32)
    if final:
        acc = _rms(acc, gf_ref[...])
    out_ref[...] = acc


def _mlp(h, o, wo, g, wup, wdn, gf, final, tm, tf):
    t, d = h.shape
    row = pl.BlockSpec((tm, d), lambda i: (i, 0))
    return pl.pallas_call(
        functools.partial(_mlp_kernel, tf=tf, final=final),
        grid=(t // tm,),
        in_specs=[row, pl.BlockSpec((o.shape[0], tm), lambda i: (0, i)), _resident(wo.shape),
                  _resident((1, d)), _resident(wup.shape), _resident(wdn.shape), _resident((1, d))],
        out_specs=row,
        out_shape=jax.ShapeDtypeStruct((t, d), F32),
        compiler_params=_params("parallel"),
        name="oproj_mlp",
    )(h, o, wo, g, wup, wdn, gf)


def _prep_w_in_a(w):
    kv = A_KV_HEADS * HEAD_DIM
    o_k, o_v, o_qi = A_Q, A_Q + kv, A_Q + 2 * kv
    o_ki, o_wi = o_qi + A_QI, o_qi + A_QI + HEAD_DIM
    wi = w[:, o_wi:o_wi + IDX_HEADS]
    return jnp.concatenate([
        w[:, :o_ki],
        w[:, o_ki:o_wi], w[:, o_ki:o_wi],
        jnp.pad(wi, ((0, 0), (0, LANES - IDX_HEADS))),
    ], axis=1).astype(BF16)


def _rope_tables(seq):
    inv = 1.0 / (ROPE_THETA ** (jnp.arange(0, HEAD_DIM, 2, dtype=F32) / HEAD_DIM))
    ang = jnp.arange(seq, dtype=F32)[:, None] * inv[None, :]
    c, s = jnp.cos(ang), jnp.sin(ang)
    return jnp.concatenate([c, c, c, c], axis=1), jnp.concatenate([-s, s, -s, s], axis=1)


def _bias_placement():
    place = np.zeros((BIAS_PIECES * LANES, B_KAUG), np.float32)
    ones = np.zeros((1, B_KAUG), np.float32)
    for h in range(B_HEADS):
        for p in range(BIAS_PIECES):
            place[p * LANES + h, h * K_ROWS + HEAD_DIM + p] = -1.0
            ones[0, h * K_ROWS + HEAD_DIM + BIAS_PIECES + p] = 1.0
    return jnp.asarray(place, BF16), jnp.asarray(ones, F32)


def _query_bias_placement(tm):
    qplace = np.zeros((B_HEADS * BF16_ROWS, BIAS_PIECES * LANES), np.float32)
    qones = np.zeros((B_HEADS * BF16_ROWS, tm), np.float32)
    for h in range(B_HEADS):
        for p in range(BIAS_PIECES):
            qplace[h * BF16_ROWS + BIAS_PIECES + p, p * LANES + h] = 1.0
            qones[h * BF16_ROWS + p, :] = 1.0
    return jnp.asarray(qplace, BF16), jnp.asarray(qones, F32)


def kernel(x, g_attn_a, w_in_a, w_o_a, g_kv, w_kv_b, b_f, g_attn_b, w_q_b, w_o_b,
           g_mlp, w_up, w_down, g_final):
    b, s, d = x.shape
    t = b * s
    depth = g_mlp.shape[0]
    n_a = g_attn_a.shape[0]
    tm = min(512, s)
    tq = min(ATTN_BLOCK, s)
    tf = 512

    cos, sin = _rope_tables(s)
    gf = g_final.reshape(1, d)
    h = x.reshape(t, d)
    kaug = vt_b = bias_rows = None

    for layer in range(depth):
        if layer < n_a:
            qt, k, vt, qit, ki, wit = _aproj(h, g_attn_a[layer].reshape(1, d),
                                             _prep_w_in_a(w_in_a[layer]), cos, sin, s, tm, tq)
            o = _dsa(qt, qit, wit, ki, k, vt, b, s, tq)
            w_o = w_o_a[layer]
        else:
            jb = layer - n_a
            if jb == 0:
                wk = jnp.pad(w_kv_b[:, :B_QK].reshape(d, B_HEADS, HEAD_DIM),
                             ((0, 0), (0, 0), (0, K_ROWS - HEAD_DIM))).reshape(d, B_KAUG)
                wf = jnp.pad(w_kv_b[:, 2 * B_QK:], ((0, 0), (0, LANES - B_HEADS)))
                bias = jnp.pad(b_f, (0, LANES - B_HEADS)).reshape(1, LANES)
                place, ones = _bias_placement()
                qplace, qones = _query_bias_placement(tm)
                kaug, vt_b, bias_rows = _kvproj(
                    h.reshape(b, s, d), g_kv.reshape(1, d), wk.astype(BF16),
                    w_kv_b[:, B_QK:2 * B_QK].astype(BF16), wf.astype(BF16), bias,
                    place, ones, qplace, qones, tm, tq)
            qt = _qproj(h, g_attn_b[jb].reshape(1, d), w_q_b[jb].astype(BF16), bias_rows, tm)
            o = _fox(qt, kaug, vt_b, b, s, tq)
            w_o = w_o_b[jb]
        h = _mlp(h, o, w_o.astype(BF16), g_mlp[layer].reshape(1, d),
                 w_up[layer].astype(BF16), w_down[layer].astype(BF16), gf,
                 layer == depth - 1, tm, tf)
    return h.reshape(b, s, d)
```
